```python
import math
import jax
import jax.numpy as jnp
from jax import lax
import numpy as np

D_MODEL = 1024
BATCH = 8
SEQ = 2048
DEPTH = 4
DEC_BATCH = 32
DEC_SEQ = 8
PAST_LEN = 8192
PAGE_SIZE = 128

N_MIXERS = 3
ALPHA = (2 * DEPTH) ** 0.25
BETA = (8 * DEPTH) ** -0.25
LN_EPS = 1e-5
NEG = -1e30
N_A = (DEPTH + 2) // 3
N_B = (DEPTH + 1) // 3
N_C = DEPTH // 3
N_DENSE = (DEPTH + 1) // 2
N_MOE = DEPTH // 2

SSD_DI = 2 * D_MODEL
SSD_P = 64
SSD_H = SSD_DI // SSD_P
SSD_N = 128
SSD_G = 4
SSD_CONV = 4
SSD_CHUNK = 128
SSD_CONV_DIM = SSD_DI + 2 * SSD_G * SSD_N
SSD_IN = SSD_DI + SSD_CONV_DIM + SSD_H

GM_D = D_MODEL
GM_G = 8
GM_GD = GM_D // GM_G
GM_CHUNK = 128

NSA_H = 16
NSA_KV = 4
NSA_R = NSA_H // NSA_KV
NSA_DH = D_MODEL // NSA_H
CMP_BLK = 32
CMP_STRIDE = 16
CMP_R = CMP_BLK // CMP_STRIDE
CMP_HID = 128
SEL_BLK = 64
N_SEL = 16
WINDOW = 512
WIN_QBLK = 128
SEL_QBLK = 16
FORCE = 1e4
NSA_IN = NSA_H * NSA_DH + 6 * NSA_KV * NSA_DH + 3 * NSA_H

D_FF = 2816
N_EXPERTS = 8
TOP_K = 2
MOE_BLK = 128

kernel_name = 'hybrid_ssd_gmlp_nsa_decode_step'


def _layernorm(x, g, b):
    xf = x.astype(jnp.float32)
    mu = jnp.mean(xf, -1, keepdims=True)
    var = jnp.mean(jnp.square(xf - mu), -1, keepdims=True)
    return ((xf - mu) * lax.rsqrt(var + LN_EPS) * g + b).astype(x.dtype)


def _alibi_slopes():
    return jnp.exp2(-8.0 * (jnp.arange(NSA_H, dtype=jnp.float32) + 1.0) / NSA_H)


def _causal_conv(xbc, prefix, w, b):
    L = xbc.shape[1]
    xp = jnp.concatenate([prefix.astype(xbc.dtype), xbc], axis=1)
    out = b
    for k in range(SSD_CONV):
        out = out + xp[:, k:k + L] * w[k]
    return out, xp[:, -(SSD_CONV - 1):]


def _ssd_scan(xdt, adt, B, C, h0):
    b, L, H, P = xdt.shape
    Q = min(SSD_CHUNK, L)
    nc = -(-L // Q)
    Lp = nc * Q
    pad = Lp - L
    if pad:
        xdt = jnp.pad(xdt, ((0, 0), (0, pad), (0, 0), (0, 0)))
        adt = jnp.pad(adt, ((0, 0), (0, pad), (0, 0)))
        B = jnp.pad(B, ((0, 0), (0, pad), (0, 0), (0, 0)))
        C = jnp.pad(C, ((0, 0), (0, pad), (0, 0), (0, 0)))
    hpg = H // SSD_G
    xc = xdt.reshape(b, nc, Q, SSD_G, hpg, P)
    ac = adt.reshape(b, nc, Q, SSD_G, hpg)
    Bc = B.reshape(b, nc, Q, SSD_G, SSD_N)
    Cc = C.reshape(b, nc, Q, SSD_G, SSD_N)
    acum = jnp.cumsum(ac, axis=2)
    i = jnp.arange(Q)
    causal = (i[:, None] >= i[None, :])[None, None, :, :, None, None]
    seg = acum[:, :, :, None] - acum[:, :, None, :]
    decay = jnp.exp(jnp.where(causal, seg, -jnp.inf))
    cb = jnp.einsum('bcign,bcjgn->bcijg', Cc, Bc)
    y_diag = jnp.einsum('bcijgh,bcjghp->bcighp', cb[..., None] * decay, xc)
    decay_end = jnp.exp(acum[:, :, -1:] - acum)
    states = jnp.einsum('bcjgn,bcjghp->bcghpn', Bc, xc * decay_end[..., None])
    chunk_decay = jnp.exp(acum[:, :, -1])

    def step(h, inp):
        s, d = inp
        return h * d[..., None, None] + s, h

    h_final, h_prev = lax.scan(step, h0.reshape(b, SSD_G, hpg, P, SSD_N),
                               (jnp.moveaxis(states, 1, 0), jnp.moveaxis(chunk_decay, 1, 0)))
    h_prev = jnp.moveaxis(h_prev, 0, 1)
    y_off = jnp.einsum('bcign,bcghpn->bcighp', Cc, h_prev) * jnp.exp(acum)[..., None]
    y = (y_diag + y_off).reshape(b, Lp, H, P)[:, :L]
    return y, h_final.reshape(b, H, P, SSD_N)


def _ssd_mixer(x, conv_prefix, h0, w_in, conv_w, conv_b, dt_bias, a_log, d_skip, norm_w, w_out):
    b, L, _ = x.shape
    f32 = jnp.float32
    proj = x @ w_in
    z, xbc, dt = jnp.split(proj, [SSD_DI, SSD_DI + SSD_CONV_DIM], axis=-1)
    xbc, new_conv = _causal_conv(xbc, conv_prefix, conv_w, conv_b)
    xbc = jax.nn.silu(xbc)
    xs, Bm, Cm = jnp.split(xbc, [SSD_DI, SSD_DI + SSD_G * SSD_N], axis=-1)
    dt = jax.nn.softplus(dt.astype(f32) + dt_bias.astype(f32))
    A = -jnp.exp(a_log.astype(f32))
    xh = xs.reshape(b, L, SSD_H, SSD_P).astype(f32)
    y, h_new = _ssd_scan(xh * dt[..., None], dt * A,
                         Bm.reshape(b, L, SSD_G, SSD_N).astype(f32),
                         Cm.reshape(b, L, SSD_G, SSD_N).astype(f32), h0.astype(f32))
    y = y + d_skip.astype(f32)[:, None] * xh
    y = y.reshape(b, L, SSD_DI) * jax.nn.silu(z.astype(f32))
    yg = y.reshape(b, L, SSD_G, SSD_DI // SSD_G)
    yg = yg * lax.rsqrt(jnp.mean(jnp.square(yg), -1, keepdims=True) + LN_EPS)
    y = (yg.reshape(b, L, SSD_DI) * norm_w).astype(x.dtype)
    return y @ w_out, new_conv, h_new.astype(h0.dtype)


def _gmlp_mixer(x, w_in, b_in, ln_g, ln_b, w_s, b_s, w_out):
    b, L, _ = x.shape
    h = jax.nn.gelu(x @ w_in + b_in)
    u, v = jnp.split(h, 2, axis=-1)
    v = _layernorm(v, ln_g, ln_b)
    nc = -(-L // GM_CHUNK)
    Lp = nc * GM_CHUNK
    vp = jnp.pad(v, ((0, 0), (0, Lp - L), (0, 0))).reshape(b, nc, GM_CHUNK, GM_G, GM_GD)
    ws = jnp.tril(w_s)
    mixed = jnp.einsum('gts,bcsgd->bctgd', ws, vp) + jnp.transpose(b_s)[None, None, :, :, None]
    mixed = mixed.reshape(b, Lp, GM_D)[:, :L]
    return (u * mixed) @ w_out, v


def _compress(k, pos, w1, b1, w2, b2):
    b, T = k.shape[:2]
    nsub = T // CMP_STRIDE
    nbc = nsub - CMP_R + 1
    sub = k[:, :nsub * CMP_STRIDE].reshape(b, nsub, CMP_STRIDE, NSA_KV, NSA_DH)
    pre = b1
    for r in range(CMP_R):
        sl = slice(r * CMP_STRIDE, (r + 1) * CMP_STRIDE)
        pre = pre + jnp.einsum('bjogd,odh->bjgh', sub[:, r:r + nbc] + pos[sl][None, None, :, None, :], w1[sl])
    return jax.nn.gelu(pre) @ w2 + b2


def _cmp_branch(q, kc, vc, q0, slopes):
    L = q.shape[1]
    nbc = kc.shape[1]
    tq = q0 + jnp.arange(L)
    e = jnp.arange(nbc) * CMP_STRIDE + (CMP_BLK - 1)
    valid = e[None, :] <= tq[:, None]
    dist = (tq[:, None] - e[None, :]).astype(jnp.float32)
    s = jnp.einsum('bqgrd,bjgd->bgrqj', q, kc).astype(jnp.float32) * NSA_DH ** -0.5
    s = s - slopes[None, :, :, None, None] * dist
    p = jax.nn.softmax(jnp.where(valid, s, NEG), axis=-1) * valid
    o = jnp.einsum('bgrqj,bjgd->bqgrd', p.astype(vc.dtype), vc)
    return o, p.sum(axis=2)


def _sel_branch(q, ks, vs, imp, q0, slopes):
    b, L, G, R, dh = q.shape
    T = ks.shape[1]
    nbc = imp.shape[-1]
    nbs = -(-T // SEL_BLK)
    ci = jnp.arange(nbc)[:, None] * CMP_STRIDE
    sj = jnp.arange(nbs)[None, :] * SEL_BLK
    overlap = ((ci < sj + SEL_BLK) & (ci + CMP_BLK > sj)).astype(jnp.float32)
    score = jnp.einsum('bgqi,ij->bgqj', imp, overlap)
    tq = q0 + jnp.arange(L)
    jj = jnp.arange(nbs)[None, :]
    cur = (tq // SEL_BLK)[:, None]
    ok = jj * SEL_BLK <= tq[:, None]
    forced = (jj == 0) | (jj == cur) | (jj == cur - 1)
    score = jnp.where(ok, score + FORCE * forced, NEG)
    kk = min(N_SEL, nbs)
    top, idx = lax.top_k(score, kk)
    blk_ok = top > 0.5 * NEG
    pad = nbs * SEL_BLK - T
    kb = jnp.pad(ks, ((0, 0), (0, pad), (0, 0), (0, 0))).reshape(b, nbs, SEL_BLK, G, dh).transpose(0, 3, 1, 2, 4)
    vb = jnp.pad(vs, ((0, 0), (0, pad), (0, 0), (0, 0))).reshape(b, nbs, SEL_BLK, G, dh).transpose(0, 3, 1, 2, 4)
    QB = min(SEL_QBLK, L)
    nq = -(-L // QB)
    Lp = nq * QB
    qpad = Lp - L
    qp = jnp.pad(q, ((0, 0), (0, qpad), (0, 0), (0, 0), (0, 0))).reshape(b, nq, QB, G, R, dh)
    idxp = jnp.pad(idx, ((0, 0), (0, 0), (0, qpad), (0, 0))).reshape(b, G, nq, QB, kk)
    okp = jnp.pad(blk_ok, ((0, 0), (0, 0), (0, qpad), (0, 0))).reshape(b, G, nq, QB, kk)
    tqp = (q0 + jnp.arange(Lp)).reshape(nq, QB)
    bI = jnp.arange(b)[:, None, None, None]
    gI = jnp.arange(G)[None, :, None, None]
    sl = slopes[None, :, :, None, None, None]
    offs = jnp.arange(SEL_BLK)

    def blk(args):
        qi, ii, oi, ti = args
        kg = kb[bI, gI, ii]
        vg = vb[bI, gI, ii]
        kpos = ii[..., None] * SEL_BLK + offs
        tt = ti[None, None, :, None, None]
        valid = oi[..., None] & (kpos <= tt)
        dist = (tt - kpos).astype(jnp.float32)
        s = jnp.einsum('bqgrd,bgqkld->bgrqkl', qi, kg).astype(jnp.float32) * NSA_DH ** -0.5
        s = jnp.where(valid[:, :, None], s - sl * dist[:, :, None], NEG)
        p = jax.nn.softmax(s.reshape(s.shape[:4] + (-1,)), axis=-1).reshape(s.shape)
        return jnp.einsum('bgrqkl,bgqkld->bqgrd', p.astype(vg.dtype), vg)

    out = lax.map(blk, (jnp.moveaxis(qp, 1, 0), jnp.moveaxis(idxp, 2, 0), jnp.moveaxis(okp, 2, 0), tqp))
    return jnp.moveaxis(out, 0, 1).reshape(b, Lp, G, R, dh)[:, :L]


def _win_branch(q, kw, vw, q0, k0, slopes):
    b, L, G, R, dh = q.shape
    Tk = kw.shape[1]
    QB = min(WIN_QBLK, L)
    nq = -(-L // QB)
    Lp = nq * QB
    span = WINDOW - 1 + QB
    off = q0 - k0
    need = off + (nq - 1) * QB + 1 + span
    back = max(0, need - (WINDOW + Tk))
    padw = ((0, 0), (WINDOW, back), (0, 0), (0, 0))
    kp = jnp.pad(kw, padw)
    vp = jnp.pad(vw, padw)
    qp = jnp.pad(q, ((0, 0), (0, Lp - L), (0, 0), (0, 0), (0, 0))).reshape(b, nq, QB, G, R, dh)
    sl = slopes[None, :, :, None, None]
    ar_q = jnp.arange(QB)
    ar_k = jnp.arange(span)

    def blk(args):
        i, qi = args
        st = off + i * QB + 1
        kb = lax.dynamic_slice_in_dim(kp, st, span, axis=1)
        vb = lax.dynamic_slice_in_dim(vp, st, span, axis=1)
        tq = q0 + i * QB + ar_q
        kidx = st - WINDOW + ar_k
        kpos = k0 + kidx
        valid = (((kidx >= 0) & (kidx < Tk))[None, :] & (kpos[None, :] <= tq[:, None])
                 & (kpos[None, :] > tq[:, None] - WINDOW))
        dist = (tq[:, None] - kpos[None, :]).astype(jnp.float32)
        s = jnp.einsum('bqgrd,bkgd->bgrqk', qi, kb).astype(jnp.float32) * NSA_DH ** -0.5 - sl * dist
        p = jax.nn.softmax(jnp.where(valid, s, NEG), axis=-1)
        return jnp.einsum('bgrqk,bkgd->bqgrd', p.astype(vb.dtype), vb)

    out = lax.map(blk, (jnp.arange(nq), jnp.moveaxis(qp, 1, 0)))
    return jnp.moveaxis(out, 0, 1).reshape(b, Lp, G, R, dh)[:, :L]


def _nsa_mixer(x, q0, past_kv, win_buf, w_in, cmp_pos, cmp_w1, cmp_b1, cmp_w2, cmp_b2, w_out):
    b, L, _ = x.shape
    G, R, dh = NSA_KV, NSA_R, NSA_DH
    proj = x @ w_in
    q, kv, gates = jnp.split(proj, [NSA_H * dh, NSA_H * dh + 6 * G * dh], axis=-1)
    q = q.reshape(b, L, G, R, dh)
    kv = kv.reshape(b, L, 6, G, dh)
    gates = jax.nn.sigmoid(gates.astype(jnp.float32)).reshape(b, L, 3, G, R, 1)
    new_kv = kv[:, :, :4]
    new_win = kv[:, :, 4:]
    if past_kv is None:
        full = new_kv
        wkv = new_win
        k0 = q0
    else:
        full = jnp.concatenate([past_kv.astype(kv.dtype), new_kv], axis=1)
        wkv = jnp.concatenate([win_buf.astype(kv.dtype), new_win], axis=1)
        k0 = q0 - win_buf.shape[1]
    slopes = _alibi_slopes().reshape(G, R)
    kc = _compress(full[:, :, 0], cmp_pos[0], cmp_w1[0], cmp_b1[0], cmp_w2[0], cmp_b2[0])
    vc = _compress(full[:, :, 1], cmp_pos[1], cmp_w1[1], cmp_b1[1], cmp_w2[1], cmp_b2[1])
    o_cmp, imp = _cmp_branch(q, kc, vc, q0, slopes)
    o_sel = _sel_branch(q, full[:, :, 2], full[:, :, 3], imp, q0, slopes)
    o_win = _win_branch(q, wkv[:, :, 0], wkv[:, :, 1], q0, k0, slopes)
    o = gates[:, :, 0] * o_cmp + gates[:, :, 1] * o_sel + gates[:, :, 2] * o_win
    y = o.reshape(b, L, NSA_H * dh).astype(x.dtype) @ w_out
    win_new = new_win[:, -min(WINDOW, L):] if past_kv is None else new_win
    return y, new_kv, win_new


def _swiglu(x, wg, wu, wd):
    return (jax.nn.silu(x @ wg) * (x @ wu)) @ wd


def _moe(x, w_router, b_router, w_gate, w_up, w_down):
    b, L, D = x.shape
    xt = x.reshape(-1, D)
    T = xt.shape[0]
    logits = (xt @ w_router).astype(jnp.float32) + b_router.astype(jnp.float32)
    top, eid = lax.top_k(logits, TOP_K)
    gate = jax.nn.softmax(top, axis=-1)
    TK = T * TOP_K
    ef = eid.reshape(-1)
    tok = jnp.arange(TK) // TOP_K
    wf = gate.reshape(-1)
    order = jnp.argsort(ef * TK + jnp.arange(TK))
    es = ef[order]
    counts = jnp.bincount(ef, length=N_EXPERTS)
    padded = (counts + MOE_BLK - 1) // MOE_BLK * MOE_BLK
    pend = jnp.cumsum(padded)
    pstart = pend - padded
    start = jnp.cumsum(counts) - counts
    dest = pstart[es] + jnp.arange(TK) - start[es]
    nblk = -(-TK // MOE_BLK) + N_EXPERTS
    buf_tok = jnp.full((nblk * MOE_BLK,), T, jnp.int32).at[dest].set(tok[order])
    buf_w = jnp.zeros((nblk * MOE_BLK,), jnp.float32).at[dest].set(wf[order])
    blk_e = jnp.minimum(jnp.searchsorted(pend, jnp.arange(nblk) * MOE_BLK, side='right'), N_EXPERTS - 1)
    xpad = jnp.concatenate([xt, jnp.zeros((1, D), xt.dtype)], axis=0)

    def blk(args):
        ti, e = args
        return _swiglu(xpad[ti], w_gate[e], w_up[e], w_down[e])

    yb = lax.map(blk, (buf_tok.reshape(nblk, MOE_BLK), blk_e))
    y = jnp.zeros((T + 1, D), jnp.float32).at[buf_tok].add(yb.reshape(-1, D).astype(jnp.float32) * buf_w[:, None])
    return y[:T].astype(x.dtype).reshape(b, L, D)


def setup_inputs(seed: int = 0) -> dict:
    key = jax.random.key(seed)
    keys = iter(jax.random.split(key, 64))
    f32 = jnp.float32

    def nrm(shape, scale=1.0):
        return jax.random.normal(next(keys), shape, f32) * scale

    n_pages = PAST_LEN // PAGE_SIZE
    n_pool = (DEC_BATCH * n_pages * 5) // 4
    win_buf = min(WINDOW, PAST_LEN)
    page_table = jax.random.permutation(next(keys), n_pool)[:DEC_BATCH * n_pages].reshape(DEC_BATCH, n_pages).astype(jnp.int32)
    dt0 = jnp.exp(jax.random.uniform(next(keys), (N_A, SSD_H), f32, math.log(1e-3), math.log(1e-1)))
    a_log = jnp.log(jax.random.uniform(next(keys), (N_A, SSD_H), f32, 1.0, 16.0))
    return {
        'x_prompt': nrm((BATCH, SEQ, D_MODEL)),
        'x_sample': nrm((DEC_BATCH, DEC_SEQ, D_MODEL)),
        'state_ssm': nrm((N_A, DEC_BATCH, SSD_H, SSD_P, SSD_N), 0.1),
        'state_conv': nrm((N_A, DEC_BATCH, SSD_CONV - 1, SSD_CONV_DIM)),
        'cache_kv': nrm((N_C, n_pool, PAGE_SIZE, 4, NSA_KV, NSA_DH)),
        'state_win': nrm((N_C, DEC_BATCH, win_buf, 2, NSA_KV, NSA_DH)),
        'page_table': page_table,
        'ln_mix_g': 1.0 + nrm((DEPTH, D_MODEL), 0.02),
        'ln_mix_b': nrm((DEPTH, D_MODEL), 0.02),
        'ln_ffn_g': 1.0 + nrm((DEPTH, D_MODEL), 0.02),
        'ln_ffn_b': nrm((DEPTH, D_MODEL), 0.02),
        'ssd_w_in': nrm((N_A, D_MODEL, SSD_IN), D_MODEL ** -0.5),
        'ssd_conv_w': nrm((N_A, SSD_CONV, SSD_CONV_DIM), SSD_CONV ** -0.5),
        'ssd_conv_b': nrm((N_A, SSD_CONV_DIM), 0.02),
        'ssd_dt_bias': dt0 + jnp.log(-jnp.expm1(-dt0)),
        'ssd_a_log': a_log,
        'ssd_d': 1.0 + nrm((N_A, SSD_H), 0.1),
        'ssd_norm_w': 1.0 + nrm((N_A, SSD_DI), 0.02),
        'ssd_w_out': nrm((N_A, SSD_DI, D_MODEL), BETA * SSD_DI ** -0.5),
        'gm_w_in': nrm((N_B, D_MODEL, 2 * GM_D), D_MODEL ** -0.5),
        'gm_b_in': nrm((N_B, 2 * GM_D), 0.02),
        'gm_ln_g': 1.0 + nrm((N_B, GM_D), 0.02),
        'gm_ln_b': nrm((N_B, GM_D), 0.02),
        'gm_w_s': nrm((N_B, GM_G, GM_CHUNK, GM_CHUNK), GM_CHUNK ** -0.5),
        'gm_b_s': 1.0 + nrm((N_B, GM_G, GM_CHUNK), 0.1),
        'gm_w_out': nrm((N_B, GM_D, D_MODEL), BETA * GM_D ** -0.5),
        'nsa_w_in': nrm((N_C, D_MODEL, NSA_IN), D_MODEL ** -0.5),
        'nsa_cmp_pos': nrm((N_C, 2, CMP_BLK, NSA_DH), 0.1),
        'nsa_cmp_w1': nrm((N_C, 2, CMP_BLK, NSA_DH, CMP_HID), (CMP_BLK * NSA_DH) ** -0.5),
        'nsa_cmp_b1': nrm((N_C, 2, CMP_HID), 0.02),
        'nsa_cmp_w2': nrm((N_C, 2, CMP_HID, NSA_DH), CMP_HID ** -0.5),
        'nsa_cmp_b2': nrm((N_C, 2, NSA_DH), 0.02),
        'nsa_w_out': nrm((N_C, NSA_H * NSA_DH, D_MODEL), BETA * (NSA_H * NSA_DH) ** -0.5),
        'ffn_w_gate': nrm((N_DENSE, D_MODEL, D_FF), D_MODEL ** -0.5),
        'ffn_w_up': nrm((N_DENSE, D_MODEL, D_FF), D_MODEL ** -0.5),
        'ffn_w_down': nrm((N_DENSE, D_FF, D_MODEL), BETA * D_FF ** -0.5),
        'moe_w_router': nrm((N_MOE, D_MODEL, N_EXPERTS), D_MODEL ** -0.5),
        'moe_b_router': nrm((N_MOE, N_EXPERTS), 0.01),
        'moe_w_gate': nrm((N_MOE, N_EXPERTS, D_MODEL, D_FF), D_MODEL ** -0.5),
        'moe_w_up': nrm((N_MOE, N_EXPERTS, D_MODEL, D_FF), D_MODEL ** -0.5),
        'moe_w_down': nrm((N_MOE, N_EXPERTS, D_FF, D_MODEL), BETA * D_FF ** -0.5),
    }


def reference(x_prompt, x_sample, state_ssm, state_conv, cache_kv, state_win, page_table,
              ln_mix_g, ln_mix_b, ln_ffn_g, ln_ffn_b,
              ssd_w_in, ssd_conv_w, ssd_conv_b, ssd_dt_bias, ssd_a_log, ssd_d, ssd_norm_w, ssd_w_out,
              gm_w_in, gm_b_in, gm_ln_g, gm_ln_b, gm_w_s, gm_b_s, gm_w_out,
              nsa_w_in, nsa_cmp_pos, nsa_cmp_w1, nsa_cmp_b1, nsa_cmp_w2, nsa_cmp_b2, nsa_w_out,
              ffn_w_gate, ffn_w_up, ffn_w_down,
              moe_w_router, moe_b_router, moe_w_gate, moe_w_up, moe_w_down):
    bp = x_prompt.shape[0]
    db = x_sample.shape[0]
    past_len = page_table.shape[1] * cache_kv.shape[2]
    xp, xs = x_prompt, x_sample
    ssm_p, conv_p, ssm_s, conv_s, gmv_s, kv_p, win_p, kv_s, win_s = [], [], [], [], [], [], [], [], []
    for l in range(DEPTH):
        kind = l % N_MIXERS
        j = l // N_MIXERS
        if kind == 0:
            prm = (ssd_w_in[j], ssd_conv_w[j], ssd_conv_b[j], ssd_dt_bias[j], ssd_a_log[j],
                   ssd_d[j], ssd_norm_w[j], ssd_w_out[j])
            h0p = jnp.zeros((bp, SSD_H, SSD_P, SSD_N), xp.dtype)
            c0p = jnp.zeros((bp, SSD_CONV - 1, SSD_CONV_DIM), xp.dtype)
            yp, cp, hp = _ssd_mixer(xp, c0p, h0p, *prm)
            ys, cs, hs = _ssd_mixer(xs, state_conv[j], state_ssm[j], *prm)
            ssm_p.append(hp)
            conv_p.append(cp)
            ssm_s.append(hs)
            conv_s.append(cs)
        elif kind == 1:
            prm = (gm_w_in[j], gm_b_in[j], gm_ln_g[j], gm_ln_b[j], gm_w_s[j], gm_b_s[j], gm_w_out[j])
            yp, _ = _gmlp_mixer(xp, *prm)
            ys, vs = _gmlp_mixer(xs, *prm)
            gmv_s.append(vs)
        else:
            prm = (nsa_w_in[j], nsa_cmp_pos[j], nsa_cmp_w1[j], nsa_cmp_b1[j], nsa_cmp_w2[j],
                   nsa_cmp_b2[j], nsa_w_out[j])
            past = cache_kv[j][page_table].reshape(db, past_len, 4, NSA_KV, NSA_DH)
            yp, kvp, wp = _nsa_mixer(xp, 0, None, None, *prm)
            ys, kvs, wsn = _nsa_mixer(xs, past_len, past, state_win[j], *prm)
            kv_p.append(kvp)
            win_p.append(wp)
            kv_s.append(kvs)
            win_s.append(wsn)
        xp = _layernorm(ALPHA * xp + yp, ln_mix_g[l], ln_mix_b[l])
        xs = _layernorm(ALPHA * xs + ys, ln_mix_g[l], ln_mix_b[l])
        f = l // 2
        if l % 2 == 0:
            fp = _swiglu(xp, ffn_w_gate[f], ffn_w_up[f], ffn_w_down[f])
            fs = _swiglu(xs, ffn_w_gate[f], ffn_w_up[f], ffn_w_down[f])
        else:
            fp = _moe(xp, moe_w_router[f], moe_b_router[f], moe_w_gate[f], moe_w_up[f], moe_w_down[f])
            fs = _moe(xs, moe_w_router[f], moe_b_router[f], moe_w_gate[f], moe_w_up[f], moe_w_down[f])
        xp = _layernorm(ALPHA * xp + fp, ln_ffn_g[l], ln_ffn_b[l])
        xs = _layernorm(ALPHA * xs + fs, ln_ffn_g[l], ln_ffn_b[l])
    return (xp, xs, jnp.stack(ssm_p), jnp.stack(conv_p), jnp.stack(ssm_s), jnp.stack(conv_s),
            jnp.stack(gmv_s), jnp.stack(kv_p), jnp.stack(win_p), jnp.stack(kv_s), jnp.stack(win_s))
```

```python
import functools

import jax
import jax.numpy as jnp
from jax import lax
from jax.experimental import pallas as pl
from jax.experimental.pallas import tpu as pltpu

F32 = jnp.float32
BF16 = jnp.bfloat16
I32 = jnp.int32

LN_EPS = 1e-5
NEG = -1e30
N_MIXERS = 3
SSD_P = 64
SSD_N = 128
SSD_G = 4
SSD_CONV = 4
SSD_CHUNK = 128
GM_CHUNK = 128
GM_G = 8
NSA_KV = 4
NSA_DH = 64
CMP_BLK = 32
CMP_STRIDE = 16
SEL_BLK = 64
N_SEL = 16
WINDOW = 512
FORCE = 1e4
TOP_K = 2

LANES = 128
SUBLANES = 8
MIB = 1 << 20


def _cparams(sem, vmem_mib):
    return pltpu.CompilerParams(dimension_semantics=sem, vmem_limit_bytes=vmem_mib * MIB)


def _row_tile(m, cands=(1024, 512, 256, 128, 64, 32, 16, 8)):
    for c in cands:
        if m % c == 0:
            return c
    raise ValueError(f"no row tile for {m}")


def _ln(v, g, b):
    mu = jnp.mean(v, -1, keepdims=True)
    d = v - mu
    var = jnp.mean(d * d, -1, keepdims=True)
    return d * lax.rsqrt(var + LN_EPS) * g + b


def _split(x, n):
    out = []
    r = x
    for k in range(n):
        h = r.astype(BF16)
        out.append(h)
        if k + 1 < n:
            r = r - h.astype(F32)
    return out


def _dot(a, b):
    return jnp.dot(a, b, preferred_element_type=F32)


def _dot_nt(a, b):
    return lax.dot_general(a, b, (((1,), (1,)), ((), ())), preferred_element_type=F32)


def _dot_tn(a, b):
    return lax.dot_general(a, b, (((0,), (0,)), ((), ())), preferred_element_type=F32)


def _silu(x):
    return x * jax.nn.sigmoid(x)


def _gelu_tanh(x):
    return 0.5 * x * (1.0 + jnp.tanh(0.7978845608028654 * (x + 0.044715 * (x * x * x))))


def _mm_kernel(x_ref, w_ref, b_ref, o_ref, *, act):
    acc = _dot(x_ref[...].astype(BF16), w_ref[...]) + b_ref[...]
    if act == "sigmoid":
        acc = jax.nn.sigmoid(acc)
    o_ref[...] = acc.astype(o_ref.dtype)


def _matmul(x, w, b=None, act=None, tn=None):
    m, k = x.shape
    n = w.shape[1]
    tm = _row_tile(m)
    tn = n if tn is None else tn
    if b is None:
        b = jnp.zeros((1, n), F32)
    return pl.pallas_call(
        functools.partial(_mm_kernel, act=act),
        grid=(m // tm, n // tn),
        in_specs=[pl.BlockSpec((tm, k), lambda i, j: (i, 0)),
                  pl.BlockSpec((k, tn), lambda i, j: (0, j)),
                  pl.BlockSpec((1, tn), lambda i, j: (0, j))],
        out_specs=pl.BlockSpec((tm, tn), lambda i, j: (i, j)),
        out_shape=jax.ShapeDtypeStruct((m, n), F32),
        compiler_params=_cparams(("parallel", "arbitrary"), 48),
        name="matmul",
    )(x, w, b)


def _mm_res_ln_kernel(y_ref, x_ref, w_ref, g_ref, b_ref, o_ref, *, alpha):
    f = _dot(y_ref[...].astype(BF16), w_ref[...])
    o_ref[...] = _ln(alpha * x_ref[...] + f, g_ref[...], b_ref[...])


def _mm_res_ln(y, x, w, g, b, alpha):
    m, k = y.shape
    d = x.shape[1]
    tm = _row_tile(m, (512, 256, 128))
    return pl.pallas_call(
        functools.partial(_mm_res_ln_kernel, alpha=alpha),
        grid=(m // tm,),
        in_specs=[pl.BlockSpec((tm, k), lambda i: (i, 0)),
                  pl.BlockSpec((tm, d), lambda i: (i, 0)),
                  pl.BlockSpec((k, d), lambda i: (0, 0)),
                  pl.BlockSpec((1, d), lambda i: (0, 0)),
                  pl.BlockSpec((1, d), lambda i: (0, 0))],
        out_specs=pl.BlockSpec((tm, d), lambda i: (i, 0)),
        out_shape=jax.ShapeDtypeStruct((m, d), F32),
        compiler_params=_cparams(("parallel",), 48),
        name="mm_res_ln",
    )(y, x, w, g, b)


def _swiglu_kernel(x_ref, wg_ref, wu_ref, wd_ref, g_ref, b_ref, o_ref, acc_ref, xb_ref, *, alpha):
    f = pl.program_id(1)

    @pl.when(f == 0)
    def _():
        acc_ref[...] = jnp.zeros_like(acc_ref)
        xb_ref[...] = x_ref[...].astype(BF16)

    xb = xb_ref[...]
    h = _silu(_dot(xb, wg_ref[...])) * _dot(xb, wu_ref[...])
    acc_ref[...] += _dot(h.astype(BF16), wd_ref[...])

    @pl.when(f == pl.num_programs(1) - 1)
    def _():
        o_ref[...] = _ln(alpha * x_ref[...] + acc_ref[...], g_ref[...], b_ref[...])


def _ff_tile(dff):
    for c in (512, 256, 128):
        if dff % c == 0:
            return c
    return dff


def _swiglu_res_ln(x, wg, wu, wd, g, b, alpha):
    m, d = x.shape
    dff = wg.shape[1]
    tm = _row_tile(m)
    tf = _ff_tile(dff)
    return pl.pallas_call(
        functools.partial(_swiglu_kernel, alpha=alpha),
        grid=(m // tm, dff // tf),
        in_specs=[pl.BlockSpec((tm, d), lambda i, f: (i, 0)),
                  pl.BlockSpec((d, tf), lambda i, f: (0, f)),
                  pl.BlockSpec((d, tf), lambda i, f: (0, f)),
                  pl.BlockSpec((tf, d), lambda i, f: (f, 0)),
                  pl.BlockSpec((1, d), lambda i, f: (0, 0)),
                  pl.BlockSpec((1, d), lambda i, f: (0, 0))],
        out_specs=pl.BlockSpec((tm, d), lambda i, f: (i, 0)),
        out_shape=jax.ShapeDtypeStruct((m, d), F32),
        scratch_shapes=[pltpu.VMEM((tm, d), F32), pltpu.VMEM((tm, d), BF16)],
        compiler_params=_cparams(("parallel", "arbitrary"), 48),
        name="swiglu_res_ln",
    )(x, wg, wu, wd, g, b)


def _ssd_kernel(xbc_ref, z_ref, dt_ref, conv0_ref, h0_ref, cw_ref, cb_ref, dtb_ref, alog_ref,
                dexp_ref, nw_ref, e_ref, ltri_ref, y_ref, h_ref, xp_scr, dtp_scr, yacc_scr,
                *, lv, n_heads):
    q = SSD_CHUNK
    di = n_heads * SSD_P
    gn = SSD_G * SSD_N
    hpg = n_heads // SSD_G
    gw = hpg * SSD_P
    ci = pl.program_id(1)

    @pl.when(ci == 0)
    def _():
        xp_scr[0:8, :] = conv0_ref[0]
        h_ref[0] = h0_ref[0]

    @pl.when(ci > 0)
    def _():
        xp_scr[0:8, :] = xp_scr[q:q + 8, :]

    if lv < q:
        xp_scr[8 + lv:, :] = jnp.zeros((q - lv, xp_scr.shape[1]), F32)
        dtp_scr[...] = jnp.zeros_like(dtp_scr)
        dtp_scr[0:lv, :] = dt_ref[...]
        dt_raw = dtp_scr[...]
    else:
        dt_raw = dt_ref[...]
    xp_scr[8:8 + lv, :] = xbc_ref[...]

    conv = cb_ref[...]
    for k in range(SSD_CONV):
        conv = conv + cw_ref[k:k + 1, :] * xp_scr[5 + k:5 + k + q, :]
    xc = _silu(conv)
    row = lax.broadcasted_iota(I32, (q, LANES), 0)
    lane = lax.broadcasted_iota(I32, (q, LANES), 1)
    if lv < q:
        xc = jnp.where(lax.broadcasted_iota(I32, xc.shape, 0) < lv, xc, 0.0)

    v = dt_raw + dtb_ref[...]
    dt = jnp.maximum(v, 0.0) + jnp.log1p(jnp.exp(-jnp.abs(v)))
    dt = jnp.where((row < lv) & (lane < n_heads), dt, 0.0)
    adt = dt * (-jnp.exp(alog_ref[...]))
    acum = sum(_dot(ltri_ref[...], t) for t in _split(adt, 3))
    acum_t = acum.T
    dt_t = dt.T
    ea = jnp.exp(acum)
    w = dt * jnp.exp(acum[q - 1:q, :] - acum)
    w_exp = sum(_dot(t, e_ref[...]) for t in _split(w, 2))
    ea_exp = sum(_dot(t, e_ref[...]) for t in _split(ea[:lv], 2))
    cdm = jnp.exp(jnp.broadcast_to(acum_t[:, q - 1:q], (LANES, LANES)))

    xs = xc[:, :di]
    xd = (xs * w_exp).astype(BF16)
    causal = (lax.broadcasted_iota(I32, (lv, q), 0) >= lax.broadcasted_iota(I32, (lv, q), 1))
    lo = lane < SSD_P

    for g in range(SSD_G):
        bg = xc[:, di + g * SSD_N:di + (g + 1) * SSD_N].astype(BF16)
        cg = xc[:lv, di + gn + g * SSD_N:di + gn + (g + 1) * SSD_N].astype(BF16)
        cbm = _dot_nt(cg, bg)
        hg = h_ref[0, g * gw:(g + 1) * gw, :].astype(BF16)
        yoff = _dot_nt(cg, hg)
        st = _dot_tn(xd[:, g * gw:(g + 1) * gw], bg)
        for pr in range(hpg // 2):
            col = g * gw + pr * LANES
            ms = []
            for hh in (2 * pr, 2 * pr + 1):
                h = g * hpg + hh
                seg = acum[:lv, h:h + 1] - acum_t[h:h + 1, :]
                dec = jnp.exp(jnp.where(causal, seg, -jnp.inf))
                ms.append((cbm * dec * dt_t[h:h + 1, :]).astype(BF16))
            xpair = xs[:, col:col + LANES]
            rhs = jnp.concatenate([jnp.where(lo, xpair, 0.0), jnp.where(lo, 0.0, xpair)], axis=0).astype(BF16)
            yd = _dot(jnp.concatenate(ms, axis=1), rhs)
            yacc_scr[:, col:col + LANES] = (yd + yoff[:, pr * LANES:(pr + 1) * LANES] * ea_exp[:, col:col + LANES]
                                            + dexp_ref[:, col:col + LANES] * xs[:lv, col:col + LANES])
        for hh in range(hpg):
            h = g * hpg + hh
            r0 = h * SSD_P
            h_ref[0, r0:r0 + SSD_P, :] = (h_ref[0, r0:r0 + SSD_P, :] * cdm[h:h + 1, :]
                                          + st[hh * SSD_P:(hh + 1) * SSD_P, :])

    y = yacc_scr[...] * _silu(z_ref[...])
    ng = di // SSD_G
    for g in range(SSD_G):
        yg = y[:, g * ng:(g + 1) * ng]
        ms_ = jnp.mean(yg * yg, -1, keepdims=True)
        y_ref[:, g * ng:(g + 1) * ng] = yg * lax.rsqrt(ms_ + LN_EPS) * nw_ref[:, g * ng:(g + 1) * ng]


def _ssd_scan(xbc, z, dt, conv0, h0, cw, cb, dtb, alog, dexp, nw, e, ltri, nb, lv):
    m, cdim = xbc.shape
    di = z.shape[1]
    n_heads = di // SSD_P
    nc = m // (nb * lv)
    return pl.pallas_call(
        functools.partial(_ssd_kernel, lv=lv, n_heads=n_heads),
        grid=(nb, nc),
        in_specs=[pl.BlockSpec((lv, cdim), lambda b, c: (b * nc + c, 0)),
                  pl.BlockSpec((lv, di), lambda b, c: (b * nc + c, 0)),
                  pl.BlockSpec((lv, LANES), lambda b, c: (b * nc + c, 0)),
                  pl.BlockSpec((1, 8, cdim), lambda b, c: (b, 0, 0)),
                  pl.BlockSpec((1, di, SSD_N), lambda b, c: (b, 0, 0)),
                  pl.BlockSpec((SSD_CONV, cdim), lambda b, c: (0, 0)),
                  pl.BlockSpec((1, cdim), lambda b, c: (0, 0)),
                  pl.BlockSpec((1, LANES), lambda b, c: (0, 0)),
                  pl.BlockSpec((1, LANES), lambda b, c: (0, 0)),
                  pl.BlockSpec((1, di), lambda b, c: (0, 0)),
                  pl.BlockSpec((1, di), lambda b, c: (0, 0)),
                  pl.BlockSpec((LANES, di), lambda b, c: (0, 0)),
                  pl.BlockSpec((SSD_CHUNK, SSD_CHUNK), lambda b, c: (0, 0))],
        out_specs=[pl.BlockSpec((lv, di), lambda b, c: (b * nc + c, 0)),
                   pl.BlockSpec((1, di, SSD_N), lambda b, c: (b, 0, 0))],
        out_shape=[jax.ShapeDtypeStruct((m, di), F32),
                   jax.ShapeDtypeStruct((nb, di, SSD_N), F32)],
        scratch_shapes=[pltpu.VMEM((SSD_CHUNK + 8, cdim), F32),
                        pltpu.VMEM((SSD_CHUNK, LANES), F32),
                        pltpu.VMEM((lv, di), F32)],
        compiler_params=_cparams(("parallel", "arbitrary"), 56),
        name="ssd_scan",
    )(xbc, z, dt, conv0, h0, cw, cb, dtb, alog, dexp, nw, e, ltri)


def _ssd_mixer(x, nb, lv, conv0, h0, prm):
    w_in, conv_w, conv_b, dt_bias, a_log, d_skip, norm_w = prm
    di = norm_w.shape[0]
    cdim = conv_w.shape[1]
    n_heads = di // SSD_P
    seq = x.shape[0] // nb
    w_z = w_in[:, :di].astype(BF16)
    w_x = w_in[:, di:di + cdim].astype(BF16)
    w_dt = jnp.pad(w_in[:, di + cdim:], ((0, 0), (0, LANES - n_heads))).astype(BF16)
    z = _matmul(x, w_z, tn=1024)
    xbc = _matmul(x, w_x, tn=1024)
    dt = _matmul(x, w_dt)
    pad1 = lambda a: jnp.pad(a.astype(F32), (0, LANES - n_heads))[None, :]
    e = (jnp.arange(di)[None, :] // SSD_P == jnp.arange(LANES)[:, None]).astype(BF16)
    ltri = (jnp.arange(SSD_CHUNK)[:, None] >= jnp.arange(SSD_CHUNK)[None, :]).astype(BF16)
    y, h_new = _ssd_scan(xbc, z, dt, conv0, h0.reshape(nb, di, SSD_N), conv_w, conv_b[None, :],
                         pad1(dt_bias), pad1(a_log), jnp.repeat(d_skip, SSD_P)[None, :], norm_w[None, :],
                         e, ltri, nb, lv)
    conv_new = xbc.reshape(nb, seq, cdim)[:, seq - (SSD_CONV - 1):]
    return y, conv_new, h_new.reshape(nb, n_heads, SSD_P, SSD_N)


def _gm_in_kernel(x_ref, w_ref, b_ref, g_ref, bb_ref, u_ref, v_ref):
    h = _gelu_tanh(_dot(x_ref[...].astype(BF16), w_ref[...]) + b_ref[...])
    d = u_ref.shape[1]
    u_ref[...] = h[:, :d]
    v_ref[...] = _ln(h[:, d:], g_ref[...], bb_ref[...])


def _gm_in(x, w, b, g, bb):
    m, k = x.shape
    d = w.shape[1] // 2
    tm = _row_tile(m, (512, 256))
    return pl.pallas_call(
        _gm_in_kernel,
        grid=(m // tm,),
        in_specs=[pl.BlockSpec((tm, k), lambda i: (i, 0)),
                  pl.BlockSpec((k, 2 * d), lambda i: (0, 0)),
                  pl.BlockSpec((1, 2 * d), lambda i: (0, 0)),
                  pl.BlockSpec((1, d), lambda i: (0, 0)),
                  pl.BlockSpec((1, d), lambda i: (0, 0))],
        out_specs=[pl.BlockSpec((tm, d), lambda i: (i, 0)), pl.BlockSpec((tm, d), lambda i: (i, 0))],
        out_shape=[jax.ShapeDtypeStruct((m, d), F32), jax.ShapeDtypeStruct((m, d), F32)],
        compiler_params=_cparams(("parallel",), 48),
        name="gm_in",
    )(x, w, b, g, bb)


def _gm_out_kernel(u_ref, v_ref, x_ref, s_ref, sb_ref, w_ref, g_ref, b_ref, o_ref, gated_scr, *, alpha, r):
    tb = u_ref.shape[0]
    gd = u_ref.shape[1] // GM_G
    for s in range(tb // r):
        rows = slice(s * r, (s + 1) * r)
        for g in range(GM_G):
            cols = slice(g * gd, (g + 1) * gd)
            mixed = _dot(s_ref[g], v_ref[rows, cols].astype(BF16)) + sb_ref[rows, cols]
            gated_scr[rows, cols] = (u_ref[rows, cols] * mixed).astype(BF16)
    f = _dot(gated_scr[...], w_ref[...])
    o_ref[...] = _ln(alpha * x_ref[...] + f, g_ref[...], b_ref[...])


def _gm_out(u, v, x, smat, sbias, w, g, b, alpha, tb):
    m, d = u.shape
    r = smat.shape[1]
    return pl.pallas_call(
        functools.partial(_gm_out_kernel, alpha=alpha, r=r),
        grid=(m // tb,),
        in_specs=[pl.BlockSpec((tb, d), lambda i: (i, 0)),
                  pl.BlockSpec((tb, d), lambda i: (i, 0)),
                  pl.BlockSpec((tb, d), lambda i: (i, 0)),
                  pl.BlockSpec((GM_G, r, r), lambda i: (0, 0, 0)),
                  pl.BlockSpec((tb, d), lambda i: (0, 0)),
                  pl.BlockSpec((d, d), lambda i: (0, 0)),
                  pl.BlockSpec((1, d), lambda i: (0, 0)),
                  pl.BlockSpec((1, d), lambda i: (0, 0))],
        out_specs=pl.BlockSpec((tb, d), lambda i: (i, 0)),
        out_shape=jax.ShapeDtypeStruct((m, d), F32),
        scratch_shapes=[pltpu.VMEM((tb, d), BF16)],
        compiler_params=_cparams(("parallel",), 48),
        name="gm_out",
    )(u, v, x, smat, sbias, w, g, b)


def _gmlp_mixer(x, nb, seq, prm, ln_g, ln_b, alpha):
    w_in, b_in, g_in, bb_in, w_s, b_s, w_out = prm
    d = w_out.shape[0]
    gd = d // GM_G
    u, v = _gm_in(x, w_in.astype(BF16), b_in[None, :], g_in[None, :], bb_in[None, :])
    ws = jnp.tril(w_s)
    if seq % GM_CHUNK == 0:
        tb = 512
        smat = ws.astype(BF16)
        bias_rows = jnp.repeat(b_s.T, gd, axis=1)
        sbias = jnp.tile(bias_rows, (tb // GM_CHUNK, 1))
    else:
        tb = nb * seq
        smat = jnp.einsum("ab,gts->gatbs", jnp.eye(nb, dtype=F32), ws[:, :seq, :seq]).reshape(GM_G, tb, tb).astype(BF16)
        sbias = jnp.tile(jnp.repeat(b_s.T[:seq], gd, axis=1), (nb, 1))
    out = _gm_out(u, v, x, smat, sbias, w_out.astype(BF16), ln_g, ln_b, alpha, tb)
    return out, v


def _router_kernel(x_ref, wh_ref, wl_ref, b_ref, eid_ref, gate_ref):
    x = x_ref[...]
    xh, xl = _split(x, 2)
    logits = _dot(xh, wh_ref[...]) + _dot(xl, wh_ref[...]) + _dot(xh, wl_ref[...]) + b_ref[...]
    lane = lax.broadcasted_iota(I32, logits.shape, 1)
    m1 = jnp.max(logits, -1, keepdims=True)
    i1 = jnp.min(jnp.where(logits == m1, lane, LANES), -1, keepdims=True)
    rest = jnp.where(lane == i1, NEG * 2, logits)
    m2 = jnp.max(rest, -1, keepdims=True)
    i2 = jnp.min(jnp.where(rest == m2, lane, LANES), -1, keepdims=True)
    e = jnp.exp(m2 - m1)
    g1 = 1.0 / (1.0 + e)
    eid_ref[...] = jnp.where(lane == 0, i1, jnp.where(lane == 1, i2, 0))
    gate_ref[...] = jnp.where(lane == 0, g1, jnp.where(lane == 1, e * g1, 0.0))


def _router(x, w_router, b_router):
    m, d = x.shape
    ne = w_router.shape[1]
    w = jnp.pad(w_router, ((0, 0), (0, LANES - ne)))
    wh = w.astype(BF16)
    wl = (w - wh.astype(F32)).astype(BF16)
    b = jnp.pad(b_router.astype(F32), (0, LANES - ne), constant_values=NEG)[None, :]
    tm = _row_tile(m, (512, 256))
    return pl.pallas_call(
        _router_kernel,
        grid=(m // tm,),
        in_specs=[pl.BlockSpec((tm, d), lambda i: (i, 0)),
                  pl.BlockSpec((d, LANES), lambda i: (0, 0)),
                  pl.BlockSpec((d, LANES), lambda i: (0, 0)),
                  pl.BlockSpec((1, LANES), lambda i: (0, 0))],
        out_specs=[pl.BlockSpec((tm, LANES), lambda i: (i, 0)), pl.BlockSpec((tm, LANES), lambda i: (i, 0))],
        out_shape=[jax.ShapeDtypeStruct((m, LANES), I32), jax.ShapeDtypeStruct((m, LANES), F32)],
        compiler_params=_cparams(("parallel",), 32),
        name="router",
    )(x, wh, wl, b)


def _row_copy(src_hbm, dst, s_row, d_row, sem):
    return pltpu.make_async_copy(src_hbm.at[pl.ds(s_row, 1), :], dst.at[pl.ds(d_row, 1), :], sem)


def _gather_kernel(idx_ref, x_hbm, o_hbm, sem, *, tb):
    base = pl.program_id(0) * tb

    def start(j, c):
        _row_copy(x_hbm, o_hbm, idx_ref[0, 0, j], base + j, sem).start()
        return c

    lax.fori_loop(0, tb, start, 0)

    def wait(j, c):
        _row_copy(x_hbm, o_hbm, 0, base + j, sem).wait()
        return c

    lax.fori_loop(0, tb, wait, 0)


def _gather_rows(x, idx, tb):
    n = idx.shape[0]
    d = x.shape[1]
    return pl.pallas_call(
        functools.partial(_gather_kernel, tb=tb),
        grid=(n // tb,),
        in_specs=[pl.BlockSpec((1, 1, tb), lambda i: (i, 0, 0), memory_space=pltpu.SMEM),
                  pl.BlockSpec(memory_space=pl.ANY)],
        out_specs=pl.BlockSpec(memory_space=pl.ANY),
        out_shape=jax.ShapeDtypeStruct((n, d), x.dtype),
        scratch_shapes=[pltpu.SemaphoreType.DMA(())],
        compiler_params=_cparams(("arbitrary",), 32),
        name="gather_rows",
    )(idx.reshape(n // tb, 1, tb), x)


def _moe_ffn_kernel(be_ref, nu_ref, x_ref, wg_ref, wu_ref, wd_ref, o_ref, acc_ref, xb_ref):
    i = pl.program_id(0)
    f = pl.program_id(1)
    last = pl.num_programs(1) - 1
    used = i < nu_ref[0]

    @pl.when(used & (f == 0))
    def _():
        acc_ref[...] = jnp.zeros_like(acc_ref)
        xb_ref[...] = x_ref[...].astype(BF16)

    @pl.when(used)
    def _():
        xb = xb_ref[...]
        h = _silu(_dot(xb, wg_ref[0])) * _dot(xb, wu_ref[0])
        acc_ref[...] += _dot(h.astype(BF16), wd_ref[0])

    @pl.when(used & (f == last))
    def _():
        o_ref[...] = acc_ref[...]

    @pl.when(jnp.logical_not(used) & (f == last))
    def _():
        o_ref[...] = jnp.zeros_like(o_ref)


def _moe_ffn(xs, blk_e, n_used, wg, wu, wd, tb):
    n, d = xs.shape
    dff = wg.shape[2]
    tf = _ff_tile(dff)
    nf = dff // tf

    def fe(i, f, be, nu):
        return jnp.where(i < nu[0], f, nf - 1)

    grid_spec = pltpu.PrefetchScalarGridSpec(
        num_scalar_prefetch=2,
        grid=(n // tb, nf),
        in_specs=[pl.BlockSpec((tb, d), lambda i, f, be, nu: (i, 0)),
                  pl.BlockSpec((1, d, tf), lambda i, f, be, nu: (be[i], 0, fe(i, f, be, nu))),
                  pl.BlockSpec((1, d, tf), lambda i, f, be, nu: (be[i], 0, fe(i, f, be, nu))),
                  pl.BlockSpec((1, tf, d), lambda i, f, be, nu: (be[i], fe(i, f, be, nu), 0))],
        out_specs=pl.BlockSpec((tb, d), lambda i, f, be, nu: (i, 0)),
        scratch_shapes=[pltpu.VMEM((tb, d), F32), pltpu.VMEM((tb, d), BF16)])
    return pl.pallas_call(
        _moe_ffn_kernel,
        grid_spec=grid_spec,
        out_shape=jax.ShapeDtypeStruct((n, d), F32),
        compiler_params=_cparams(("arbitrary", "arbitrary"), 48),
        name="moe_ffn",
    )(blk_e, n_used, xs, wg, wu, wd)


def _combine_kernel(idx_ref, yb_hbm, x_ref, gate_ref, g_ref, b_ref, o_ref, ybuf, sem, *, alpha, tm):
    def start(j, c):
        for k in range(TOP_K):
            _row_copy(yb_hbm, ybuf.at[k], idx_ref[0, 0, TOP_K * j + k], j, sem).start()
        return c

    lax.fori_loop(0, tm, start, 0)

    def wait(j, c):
        for k in range(TOP_K):
            _row_copy(yb_hbm, ybuf.at[k], 0, j, sem).wait()
        return c

    lax.fori_loop(0, tm, wait, 0)
    gate = gate_ref[...]
    y = gate[:, 0:1] * ybuf[0] + gate[:, 1:2] * ybuf[1]
    o_ref[...] = _ln(alpha * x_ref[...] + y, g_ref[...], b_ref[...])


def _moe_combine(yb, dest, x, gate, g, b, alpha):
    m, d = x.shape
    tm = _row_tile(m, (256, 128))
    return pl.pallas_call(
        functools.partial(_combine_kernel, alpha=alpha, tm=tm),
        grid=(m // tm,),
        in_specs=[pl.BlockSpec((1, 1, TOP_K * tm), lambda i: (i, 0, 0), memory_space=pltpu.SMEM),
                  pl.BlockSpec(memory_space=pl.ANY),
                  pl.BlockSpec((tm, d), lambda i: (i, 0)),
                  pl.BlockSpec((tm, LANES), lambda i: (i, 0)),
                  pl.BlockSpec((1, d), lambda i: (0, 0)),
                  pl.BlockSpec((1, d), lambda i: (0, 0))],
        out_specs=pl.BlockSpec((tm, d), lambda i: (i, 0)),
        out_shape=jax.ShapeDtypeStruct((m, d), F32),
        scratch_shapes=[pltpu.VMEM((TOP_K, tm, d), F32), pltpu.SemaphoreType.DMA(())],
        compiler_params=_cparams(("arbitrary",), 32),
        name="moe_combine",
    )(dest.reshape(m // tm, 1, TOP_K * tm), yb, x, gate, g, b)


def _moe_res_ln(x, w_router, b_router, wg, wu, wd, g, b, alpha):
    m, d = x.shape
    ne = w_router.shape[1]
    eid, gate = _router(x, w_router, b_router)
    tk = m * TOP_K
    tb = 512 if tk >= 8192 else 128
    ef = eid[:, :TOP_K].reshape(-1)
    onehot = (ef[:, None] == jnp.arange(ne, dtype=I32)[None, :]).astype(I32)
    csum = jnp.cumsum(onehot, axis=0)
    rank = jnp.sum(csum * onehot, axis=1) - 1
    counts = csum[-1]
    padded = (counts + tb - 1) // tb * tb
    pend = jnp.cumsum(padded)
    dest = (pend - padded)[ef] + rank
    nblk = -(-tk // tb) + ne
    buf_tok = jnp.zeros((nblk * tb,), I32).at[dest].set(jnp.arange(tk, dtype=I32) // TOP_K)
    blk_e = jnp.minimum(jnp.searchsorted(pend, jnp.arange(nblk, dtype=I32) * tb, side="right"), ne - 1).astype(I32)
    n_used = (pend[-1:] // tb).astype(I32)
    xs = _gather_rows(x, buf_tok, tb)
    yb = _moe_ffn(xs, blk_e, n_used, wg, wu, wd, tb)
    return _moe_combine(yb, dest.astype(I32), x, gate, g, b, alpha)


PAGE = 128
ATT_KT = 512


def _compress_kernel(pt_ref, pages_hbm, w1_ref, pos_ref, b1_ref, w2_ref, b2_ref, o_ref, scr, acc_scr, cv_scr, sem,
                     *, npg):
    b = pl.program_id(0)
    pw = NSA_KV * NSA_DH
    nsub = npg * PAGE // CMP_STRIDE
    hid = b1_ref.shape[2]

    nslab = 2 * pw // LANES
    gps = LANES // NSA_DH

    def copies(p, pg):
        row = pl.multiple_of(p * PAGE, PAGE)
        return [pltpu.make_async_copy(pages_hbm.at[pg, :, pl.ds(sl * LANES, LANES)],
                                      scr.at[sl, pl.ds(row, PAGE), :], sem) for sl in range(nslab)]

    def start(p, c_):
        for cp in copies(p, pt_ref[b, p]):
            cp.start()
        return c_

    lax.fori_loop(0, npg, start, 0)

    @pl.when(b == 0)
    def _():
        for c in range(2):
            cv = jnp.zeros((SUBLANES, hid), F32) + b1_ref[c]
            for o in range(CMP_STRIDE):
                cv = cv + _dot(pos_ref[c, o].astype(BF16), w1_ref[c, o])[:, :hid]
                cv = cv + _dot(pos_ref[c, CMP_STRIDE + o].astype(BF16), w1_ref[c, o])[:, hid:]
            cv_scr[c] = cv

    def wait(p, c_):
        for cp in copies(p, 0):
            cp.wait()
        return c_

    lax.fori_loop(0, npg, wait, 0)

    for c in range(2):
        acc_scr[...] = jnp.zeros_like(acc_scr)
        for o in range(CMP_STRIDE):
            for sl in range(nslab // 2):
                xo = scr[c * (nslab // 2) + sl, pl.ds(o, nsub, stride=CMP_STRIDE), :].astype(BF16)
                for gg in range(gps):
                    acc_scr[sl * gps + gg] += _dot(xo[:, gg * NSA_DH:(gg + 1) * NSA_DH], w1_ref[c, o])
        for g in range(NSA_KV):
            acc = acc_scr[g]
            pre = acc[:, :hid] + pltpu.roll(acc[:, hid:], nsub - 1, 0) + cv_scr[c, 0:1, :]
            o_ref[0, c, g] = _dot(_gelu_tanh(pre).astype(BF16), w2_ref[c]) + b2_ref[c]


def _compress(pages, page_table, pos, w1, b1, w2, b2):
    nb, npg = page_table.shape
    nsub = npg * PAGE // CMP_STRIDE
    hid = w1.shape[-1]
    pw = NSA_KV * NSA_DH
    w1cat = jnp.concatenate([w1[:, :CMP_STRIDE], w1[:, CMP_STRIDE:]], axis=-1).astype(BF16)
    posb = jnp.broadcast_to(pos[:, :, None, :], (2, CMP_BLK, SUBLANES, NSA_DH))
    grid_spec = pltpu.PrefetchScalarGridSpec(
        num_scalar_prefetch=1,
        grid=(nb,),
        in_specs=[pl.BlockSpec(memory_space=pl.ANY),
                  pl.BlockSpec((2, CMP_STRIDE, NSA_DH, 2 * hid), lambda b, pt: (0, 0, 0, 0)),
                  pl.BlockSpec((2, CMP_BLK, SUBLANES, NSA_DH), lambda b, pt: (0, 0, 0, 0)),
                  pl.BlockSpec((2, 1, hid), lambda b, pt: (0, 0, 0)),
                  pl.BlockSpec((2, hid, NSA_DH), lambda b, pt: (0, 0, 0)),
                  pl.BlockSpec((2, 1, NSA_DH), lambda b, pt: (0, 0, 0))],
        out_specs=pl.BlockSpec((1, 2, NSA_KV, nsub, NSA_DH), lambda b, pt: (b, 0, 0, 0, 0)),
        scratch_shapes=[pltpu.VMEM((2 * pw // LANES, npg * PAGE, LANES), F32),
                        pltpu.VMEM((NSA_KV, nsub, 2 * hid), F32),
                        pltpu.VMEM((2, SUBLANES, hid), F32),
                        pltpu.SemaphoreType.DMA(())])
    return pl.pallas_call(
        functools.partial(_compress_kernel, npg=npg),
        grid_spec=grid_spec,
        out_shape=jax.ShapeDtypeStruct((nb, 2, NSA_KV, nsub, NSA_DH), F32),
        compiler_params=_cparams(("arbitrary",), 56),
        name="nsa_compress",
    )(page_table, pages, w1cat, posb, b1[:, None, :], w2.astype(BF16), b2[:, None, :])


def _masked_softmax(s, valid):
    s = jnp.where(valid, s, NEG)
    p = jnp.exp(s - jnp.max(s, -1, keepdims=True))
    return p / jnp.sum(p, -1, keepdims=True)


def _select_blocks(imp, ovl, tq, nbs):
    score = sum(_dot(t, ovl) for t in _split(imp, 3))
    jb = lax.broadcasted_iota(I32, score.shape, 1)
    ok = jb * SEL_BLK <= tq
    cur = tq // SEL_BLK
    forced = (jb == 0) | (jb == cur) | (jb == cur - 1)
    score = jnp.where(ok, score + jnp.where(forced, FORCE, 0.0), NEG)
    rank = jnp.zeros(score.shape, F32)
    for j2 in range(nbs):
        cj = score[:, j2:j2 + 1]
        beats = (cj > score) | ((cj == score) & (j2 < jb))
        rank = rank + jnp.where(beats, 1.0, 0.0)
    return jnp.where(ok & (rank < N_SEL), 1.0, 0.0)


def _expand_sel(sel, first_blk, nkeys):
    jj = lax.broadcasted_iota(I32, (sel.shape[1], nkeys), 0)
    kk = lax.broadcasted_iota(I32, (sel.shape[1], nkeys), 1)
    expand = jnp.where(kk // SEL_BLK + first_blk == jj, 1.0, 0.0).astype(BF16)
    return _dot(sel.astype(BF16), expand)


def _nsa_prompt_kernel(q_ref, gate_ref, slope_ref, kc_ref, vc_ref, ks_ref, vs_ref, kw_ref, vw_ref, ovl_ref, o_ref,
                       *, nbs, qblk, nrep):
    qb = pl.program_id(2)
    rows = qblk * nrep
    scale = NSA_DH ** -0.5
    q = q_ref[0, 0, 0]
    q16 = q.astype(BF16)
    slope = slope_ref[0][:, 0:1]
    tq = qb * qblk + lax.broadcasted_iota(I32, (rows, 1), 0) % qblk
    tqf = tq.astype(F32)

    kc = kc_ref[0, 0]
    qh, ql = _split(q, 2)
    kh, kl = _split(kc, 2)
    s = (_dot_nt(qh, kh) + _dot_nt(ql, kh) + _dot_nt(qh, kl)) * scale
    e = lax.broadcasted_iota(I32, (1, kc.shape[0]), 1) * CMP_STRIDE + (CMP_BLK - 1)
    valid = e <= tq
    p = jnp.where(valid, _masked_softmax(s - slope * (tqf - e.astype(F32)), valid), 0.0)
    o_cmp = _dot(p.astype(BF16), vc_ref[0, 0].astype(BF16))
    imp = p[0:qblk]
    for r in range(1, nrep):
        imp = imp + p[r * qblk:(r + 1) * qblk]
    sel = _select_blocks(imp, ovl_ref[...], tq[0:qblk], nbs)

    def attend(k_ref, v_ref, kt_lo, kt_hi, mask_fn):
        def body(kt, carry):
            m, l, acc = carry
            off = pl.multiple_of(kt * ATT_KT, ATT_KT)
            k = k_ref[0, 0, pl.ds(off, ATT_KT), :]
            v = v_ref[0, 0, pl.ds(off, ATT_KT), :]
            kpos = off + lax.broadcasted_iota(I32, (1, ATT_KT), 1)
            s_ = _dot_nt(q16, k) * scale - slope * (tqf - kpos.astype(F32))
            s_ = jnp.where(mask_fn(kt, kpos), s_, NEG)
            m_new = jnp.maximum(m, jnp.max(s_, -1, keepdims=True))
            a = jnp.exp(m - m_new)
            p_ = jnp.exp(s_ - m_new)
            return m_new, l * a + jnp.sum(p_, -1, keepdims=True), acc * a + _dot(p_.astype(BF16), v)

        init = (jnp.full((rows, 1), -jnp.inf, F32), jnp.zeros((rows, 1), F32), jnp.zeros((rows, NSA_DH), F32))
        m, l, acc = lax.fori_loop(kt_lo, kt_hi, body, init)
        return acc / l

    def sel_mask(kt, kpos):
        selk = _expand_sel(sel, kt * (ATT_KT // SEL_BLK), ATT_KT)
        return (jnp.concatenate([selk] * nrep, axis=0) > 0.5) & (kpos <= tq)

    def win_mask(kt, kpos):
        return (kpos <= tq) & (kpos > tq - WINDOW)

    kt_hi = (qb * qblk + qblk - 1) // ATT_KT + 1
    o_sel = attend(ks_ref, vs_ref, 0, kt_hi, sel_mask)
    o_win = attend(kw_ref, vw_ref, jnp.maximum(qb * qblk - (WINDOW - 1), 0) // ATT_KT, kt_hi, win_mask)
    gate = gate_ref[0, 0, 0]
    o_ref[0, 0, 0] = gate[:, 0:1] * o_cmp + gate[:, 1:2] * o_sel + gate[:, 2:3] * o_win


def _nsa_prompt(q, gates, slopes, kc, vc, ks, vs, kw, vw, ovl, nbs):
    nb, ng, nqb, rows, dh = q.shape
    seq = ks.shape[2]
    nbc = kc.shape[2]
    qblk = seq // nqb
    nrep = rows // qblk
    qspec = lambda w: pl.BlockSpec((1, 1, 1, rows, w), lambda b, g, i: (b, g, i, 0, 0))
    kvspec = lambda n: pl.BlockSpec((1, 1, n, dh), lambda b, g, i: (b, g, 0, 0))
    return pl.pallas_call(
        functools.partial(_nsa_prompt_kernel, nbs=nbs, qblk=qblk, nrep=nrep),
        grid=(nb, ng, nqb),
        in_specs=[qspec(dh), qspec(SUBLANES),
                  pl.BlockSpec((1, rows, SUBLANES), lambda b, g, i: (g, 0, 0)),
                  kvspec(nbc), kvspec(nbc), kvspec(seq), kvspec(seq), kvspec(seq), kvspec(seq),
                  pl.BlockSpec(ovl.shape, lambda b, g, i: (0, 0))],
        out_specs=qspec(dh),
        out_shape=jax.ShapeDtypeStruct(q.shape, F32),
        compiler_params=_cparams(("parallel", "parallel", "arbitrary"), 48),
        name="nsa_prompt",
    )(q, gates, slopes, kc, vc, ks, vs, kw, vw, ovl)


def _nsa_sample_kernel(pt_ref, q_ref, gate_ref, slope_ref, kc_ref, vc_ref,
                       ks0, ks1, ks2, ks3, vs0, vs1, vs2, vs3, ksn_ref, vsn_ref, kwp_ref, vwp_ref, kwn_ref, vwn_ref,
                       ovl_ref, o_ref, sel_scr, m_scr, l_scr, acc_scr, part_scr, *, nbs, past, nq, nrep):
    t = pl.program_id(1)
    rows = q_ref.shape[1]
    pw = q_ref.shape[2]
    grows = rows // NSA_KV
    scale = NSA_DH ** -0.5
    q = q_ref[0]
    q16 = q.astype(BF16)
    slope = slope_ref[:, 0:1]
    tq = past + lax.broadcasted_iota(I32, (rows, 1), 0) % nq
    tqf = tq.astype(F32)
    gate = gate_ref[0]

    def scores(k16, kpos):
        return _dot_nt(q16, k16) * scale - slope * (tqf - kpos.astype(F32))

    @pl.when(t == 0)
    def _():
        kc = kc_ref[0]
        qh, ql = _split(q, 2)
        kh, kl = _split(kc, 2)
        s = (_dot_nt(qh, kh) + _dot_nt(ql, kh) + _dot_nt(qh, kl)) * scale
        e = lax.broadcasted_iota(I32, (1, kc.shape[0]), 1) * CMP_STRIDE + (CMP_BLK - 1)
        valid = e <= tq
        p = jnp.where(valid, _masked_softmax(s - slope * (tqf - e.astype(F32)), valid), 0.0)
        o_cmp = _dot(p.astype(BF16), vc_ref[0].astype(BF16))
        imps = []
        for g in range(NSA_KV):
            a = p[g * grows:g * grows + nq]
            for r in range(1, nrep):
                a = a + p[g * grows + r * nq:g * grows + (r + 1) * nq]
            imps.append(a)
        imp = jnp.concatenate(imps, axis=0)
        tq_s = past + lax.broadcasted_iota(I32, (NSA_KV * nq, 1), 0) % nq
        sel_s = _select_blocks(imp, ovl_ref[...], tq_s, nbs)
        sel = jnp.concatenate([sel_s[g * nq:(g + 1) * nq] for g in range(NSA_KV) for _ in range(nrep)], axis=0)
        sel_scr[...] = sel

        wb = kwp_ref.shape[1]
        npad = LANES - nq
        zpad = jnp.zeros((npad, pw), F32)
        kw = jnp.concatenate([kwp_ref[0], kwn_ref[0], zpad], axis=0).astype(BF16)
        vw = jnp.concatenate([vwp_ref[0], vwn_ref[0], zpad], axis=0).astype(BF16)
        idx = lax.broadcasted_iota(I32, (1, wb + LANES), 1)
        kpos = past - wb + idx
        valid = (idx < wb + nq) & (kpos <= tq) & (kpos > tq - WINDOW)
        o_win = _dot(_masked_softmax(scores(kw, kpos), valid).astype(BF16), vw)
        part_scr[...] = gate[:, 0:1] * o_cmp + gate[:, 2:3] * o_win

        kn = jnp.concatenate([ksn_ref[0], zpad], axis=0).astype(BF16)
        vn = jnp.concatenate([vsn_ref[0], zpad], axis=0).astype(BF16)
        idx = lax.broadcasted_iota(I32, (1, LANES), 1)
        kpos = past + idx
        blk = past // SEL_BLK
        valid = (idx < nq) & (kpos <= tq) & (sel[:, blk:blk + 1] > 0.5)
        s = jnp.where(valid, scores(kn, kpos), NEG)
        m = jnp.max(s, -1, keepdims=True)
        p = jnp.exp(s - m)
        m_scr[...] = m
        l_scr[...] = jnp.sum(p, -1, keepdims=True)
        acc_scr[...] = _dot(p.astype(BF16), vn)

    k = jnp.concatenate([ks0[0], ks1[0], ks2[0], ks3[0]], axis=0).astype(BF16)
    v = jnp.concatenate([vs0[0], vs1[0], vs2[0], vs3[0]], axis=0).astype(BF16)
    kpos = t * ATT_KT + lax.broadcasted_iota(I32, (1, ATT_KT), 1)
    selk = _expand_sel(sel_scr[...], t * (ATT_KT // SEL_BLK), ATT_KT)
    s = jnp.where(selk > 0.5, scores(k, kpos), NEG)
    m = m_scr[...]
    m_new = jnp.maximum(m, jnp.max(s, -1, keepdims=True))
    a = jnp.exp(m - m_new)
    p = jnp.exp(s - m_new)
    m_scr[...] = m_new
    l_scr[...] = l_scr[...] * a + jnp.sum(p, -1, keepdims=True)
    acc_scr[...] = acc_scr[...] * a + _dot(p.astype(BF16), v)

    @pl.when(t == pl.num_programs(1) - 1)
    def _():
        tot = part_scr[...] + gate[:, 1:2] * (acc_scr[...] / l_scr[...])
        o_ref[0] = jnp.concatenate([tot[g * grows:(g + 1) * grows, g * NSA_DH:(g + 1) * NSA_DH]
                                    for g in range(NSA_KV)], axis=0)


def _nsa_sample(qbd, gates, slopes, kc, vc, cache, page_table, kv_new, win_past, win_new, ovl, nbs, nq, nrep):
    nb, rows, pw = qbd.shape
    npg = page_table.shape[1]
    past = npg * PAGE
    ppt = ATT_KT // PAGE
    nt = npg // ppt
    nbc = kc.shape[1]
    wb = win_past.shape[1]
    const = lambda shape: pl.BlockSpec(shape, lambda b, t, pt: (0,) * len(shape))
    per_b = lambda n, w, col=0: pl.BlockSpec((1, n, w), lambda b, t, pt: (b, 0, col))
    page = lambda c, i: pl.BlockSpec((1, PAGE, pw), lambda b, t, pt: (pt[b, ppt * t + i], 0, c))
    grid_spec = pltpu.PrefetchScalarGridSpec(
        num_scalar_prefetch=1,
        grid=(nb, nt),
        in_specs=[per_b(rows, pw), per_b(rows, SUBLANES), const((rows, SUBLANES)), per_b(nbc, pw), per_b(nbc, pw)]
                 + [page(2, i) for i in range(ppt)] + [page(3, i) for i in range(ppt)]
                 + [per_b(nq, pw, 2), per_b(nq, pw, 3), per_b(wb, pw, 0), per_b(wb, pw, 1),
                    per_b(nq, pw, 0), per_b(nq, pw, 1), const(ovl.shape)],
        out_specs=pl.BlockSpec((1, rows, NSA_DH), lambda b, t, pt: (b, 0, 0)),
        scratch_shapes=[pltpu.VMEM((rows, ovl.shape[1]), F32), pltpu.VMEM((rows, 1), F32), pltpu.VMEM((rows, 1), F32),
                        pltpu.VMEM((rows, pw), F32), pltpu.VMEM((rows, pw), F32)])
    return pl.pallas_call(
        functools.partial(_nsa_sample_kernel, nbs=nbs, past=past, nq=nq, nrep=nrep),
        grid_spec=grid_spec,
        out_shape=jax.ShapeDtypeStruct((nb, rows, NSA_DH), F32),
        compiler_params=_cparams(("parallel", "arbitrary"), 48),
        name="nsa_sample",
    )(page_table, qbd, gates, slopes, kc, vc, *([cache] * (2 * ppt)), kv_new, kv_new, win_past, win_past,
      win_new, win_new, ovl)


def _overlap(nbc, nbs_pad):
    ci = jnp.arange(nbc)[:, None] * CMP_STRIDE
    sj = jnp.arange(nbs_pad)[None, :] * SEL_BLK
    return ((ci < sj + SEL_BLK) & (ci + CMP_BLK > sj)).astype(BF16)


def _alibi_slopes(nh):
    return jnp.exp2(-8.0 * (jnp.arange(nh, dtype=F32) + 1.0) / nh)


def _nsa_in(x, w_in, nh):
    qw = nh * NSA_DH
    kvw = 6 * NSA_KV * NSA_DH
    proj = _matmul(x, w_in[:, :qw + kvw].astype(BF16), tn=512)
    ng = 3 * nh
    gates = _matmul(x, jnp.pad(w_in[:, qw + kvw:], ((0, 0), (0, LANES - ng))).astype(BF16), act="sigmoid")
    kv4w = 4 * NSA_KV * NSA_DH
    return proj[:, :qw], proj[:, qw:qw + kv4w], proj[:, qw + kv4w:], gates[:, :ng]


def _nsa_prompt_mixer(x, nb, seq, prm):
    w_in, pos, w1, b1, w2, b2 = prm
    nh = (w_in.shape[1] - 6 * NSA_KV * NSA_DH) // (NSA_DH + 3)
    nrep = nh // NSA_KV
    pw = NSA_KV * NSA_DH
    qblk = 128
    nqb = seq // qblk
    rows = qblk * nrep
    q, kv4, win, gates = _nsa_in(x, w_in, nh)
    npg = seq // PAGE
    cmp_out = _compress(kv4.reshape(nb * npg, PAGE, 4 * pw), jnp.arange(nb * npg, dtype=I32).reshape(nb, npg),
                        pos, w1, b1, w2, b2)
    qs = q.reshape(nb, nqb, qblk, NSA_KV, nrep, NSA_DH).transpose(0, 3, 1, 4, 2, 5).reshape(nb, NSA_KV, nqb, rows, NSA_DH)
    gs = gates.reshape(nb, nqb, qblk, 3, NSA_KV, nrep).transpose(0, 4, 1, 5, 2, 3).reshape(nb, NSA_KV, nqb, rows, 3)
    gs = jnp.pad(gs, ((0, 0),) * 4 + ((0, SUBLANES - 3),))
    sl = jnp.repeat(_alibi_slopes(nh).reshape(NSA_KV, nrep), qblk, axis=1)
    sl = jnp.pad(sl[:, :, None], ((0, 0), (0, 0), (0, SUBLANES - 1)))
    per_group = lambda a: a.transpose(0, 2, 1, 3).astype(BF16)
    kv5 = kv4.reshape(nb, seq, 4, NSA_KV, NSA_DH)
    win5 = win.reshape(nb, seq, 2, NSA_KV, NSA_DH)
    nbs = -(-seq // SEL_BLK)
    o = _nsa_prompt(qs, gs, sl, cmp_out[:, 0], cmp_out[:, 1], per_group(kv5[:, :, 2]), per_group(kv5[:, :, 3]),
                    per_group(win5[:, :, 0]), per_group(win5[:, :, 1]), _overlap(seq // CMP_STRIDE, LANES), nbs)
    o = o.reshape(nb, NSA_KV, nqb, nrep, qblk, NSA_DH).transpose(0, 2, 4, 1, 3, 5).reshape(nb * seq, nh * NSA_DH)
    return o, kv5, win5


def _nsa_sample_mixer(x, nb, nq, cache, page_table, win_buf, prm):
    w_in, pos, w1, b1, w2, b2 = prm
    nh = (w_in.shape[1] - 6 * NSA_KV * NSA_DH) // (NSA_DH + 3)
    nrep = nh // NSA_KV
    pw = NSA_KV * NSA_DH
    npg = page_table.shape[1]
    past = npg * PAGE
    q, kv4, win, gates = _nsa_in(x, w_in, nh)
    pages = cache.reshape(cache.shape[0], PAGE, 4 * pw)
    cmp_out = _compress(pages, page_table, pos, w1, b1, w2, b2)
    nsub = cmp_out.shape[3]
    kc = cmp_out[:, 0].transpose(0, 2, 1, 3).reshape(nb, nsub, pw)
    vc = cmp_out[:, 1].transpose(0, 2, 1, 3).reshape(nb, nsub, pw)
    rows = NSA_KV * nrep * nq
    q5 = q.reshape(nb, nq, NSA_KV, nrep, NSA_DH).transpose(0, 2, 3, 1, 4)
    qbd = jnp.einsum("bgrqd,gh->bgrqhd", q5, jnp.eye(NSA_KV, dtype=F32)).reshape(nb, rows, pw)
    gs = gates.reshape(nb, nq, 3, NSA_KV, nrep).transpose(0, 3, 4, 1, 2).reshape(nb, rows, 3)
    gs = jnp.pad(gs, ((0, 0), (0, 0), (0, SUBLANES - 3)))
    sl = jnp.repeat(_alibi_slopes(nh), nq)[:, None]
    sl = jnp.pad(sl, ((0, 0), (0, SUBLANES - 1)))
    nbs = -(-(past + nq) // SEL_BLK)
    nbs_pad = -(-nbs // LANES) * LANES
    o = _nsa_sample(qbd, gs, sl, kc, vc, pages, page_table, kv4.reshape(nb, nq, 4 * pw),
                    win_buf.reshape(nb, win_buf.shape[1], 2 * pw), win.reshape(nb, nq, 2 * pw),
                    _overlap(nsub, nbs_pad), nbs, nq, nrep)
    o = o.reshape(nb, NSA_KV, nrep, nq, NSA_DH).transpose(0, 3, 1, 2, 4).reshape(nb * nq, nh * NSA_DH)
    return o, kv4.reshape(nb, nq, 4, NSA_KV, NSA_DH), win.reshape(nb, nq, 2, NSA_KV, NSA_DH)


def kernel(x_prompt, x_sample, state_ssm, state_conv, cache_kv, state_win, page_table, ln_mix_g, ln_mix_b, ln_ffn_g, ln_ffn_b, ssd_w_in, ssd_conv_w, ssd_conv_b, ssd_dt_bias, ssd_a_log, ssd_d, ssd_norm_w, ssd_w_out, gm_w_in, gm_b_in, gm_ln_g, gm_ln_b, gm_w_s, gm_b_s, gm_w_out, nsa_w_in, nsa_cmp_pos, nsa_cmp_w1, nsa_cmp_b1, nsa_cmp_w2, nsa_cmp_b2, nsa_w_out, ffn_w_gate, ffn_w_up, ffn_w_down, moe_w_router, moe_b_router, moe_w_gate, moe_w_up, moe_w_down):
    depth = ln_mix_g.shape[0]
    alpha = (2 * depth) ** 0.25
    bp, seq, d = x_prompt.shape
    db, dseq, _ = x_sample.shape
    xp = x_prompt.reshape(bp * seq, d)
    xs = x_sample.reshape(db * dseq, d)
    outs = {k: [] for k in ("ssm_p", "conv_p", "ssm_s", "conv_s", "gmv_s", "kv_p", "win_p", "kv_s", "win_s")}
    row = lambda a, i: a[i][None, :]
    for l in range(depth):
        kind, j = l % N_MIXERS, l // N_MIXERS
        mg, mb = row(ln_mix_g, l), row(ln_mix_b, l)
        if kind == 0:
            prm = (ssd_w_in[j], ssd_conv_w[j], ssd_conv_b[j], ssd_dt_bias[j], ssd_a_log[j], ssd_d[j], ssd_norm_w[j])
            cdim = ssd_conv_w.shape[2]
            w_out = ssd_w_out[j].astype(BF16)
            yp, cp, hp = _ssd_mixer(xp, bp, SSD_CHUNK, jnp.zeros((bp, 8, cdim), F32),
                                    jnp.zeros((bp,) + state_ssm.shape[2:], F32), prm)
            ys, cs, hs = _ssd_mixer(xs, db, dseq, jnp.pad(state_conv[j], ((0, 0), (8 - (SSD_CONV - 1), 0), (0, 0))),
                                    state_ssm[j], prm)
            outs["ssm_p"].append(hp), outs["conv_p"].append(cp), outs["ssm_s"].append(hs), outs["conv_s"].append(cs)
            xp = _mm_res_ln(yp, xp, w_out, mg, mb, alpha)
            xs = _mm_res_ln(ys, xs, w_out, mg, mb, alpha)
        elif kind == 1:
            prm = (gm_w_in[j], gm_b_in[j], gm_ln_g[j], gm_ln_b[j], gm_w_s[j], gm_b_s[j], gm_w_out[j])
            xp, _ = _gmlp_mixer(xp, bp, seq, prm, mg, mb, alpha)
            xs, vs = _gmlp_mixer(xs, db, dseq, prm, mg, mb, alpha)
            outs["gmv_s"].append(vs.reshape(db, dseq, -1))
        else:
            prm = (nsa_w_in[j], nsa_cmp_pos[j], nsa_cmp_w1[j], nsa_cmp_b1[j], nsa_cmp_w2[j], nsa_cmp_b2[j])
            w_out = nsa_w_out[j].astype(BF16)
            op, kvp, wp = _nsa_prompt_mixer(xp, bp, seq, prm)
            os_, kvs, wsn = _nsa_sample_mixer(xs, db, dseq, cache_kv[j], page_table, state_win[j], prm)
            outs["kv_p"].append(kvp), outs["win_p"].append(wp[:, seq - min(WINDOW, seq):])
            outs["kv_s"].append(kvs), outs["win_s"].append(wsn)
            xp = _mm_res_ln(op, xp, w_out, mg, mb, alpha)
            xs = _mm_res_ln(os_, xs, w_out, mg, mb, alpha)
        f = l // 2
        fg, fb = row(ln_ffn_g, l), row(ln_ffn_b, l)
        if l % 2 == 0:
            wg, wu, wd = ffn_w_gate[f].astype(BF16), ffn_w_up[f].astype(BF16), ffn_w_down[f].astype(BF16)
            xp = _swiglu_res_ln(xp, wg, wu, wd, fg, fb, alpha)
            xs = _swiglu_res_ln(xs, wg, wu, wd, fg, fb, alpha)
        else:
            wg, wu, wd = moe_w_gate[f].astype(BF16), moe_w_up[f].astype(BF16), moe_w_down[f].astype(BF16)
            xp = _moe_res_ln(xp, moe_w_router[f], moe_b_router[f], wg, wu, wd, fg, fb, alpha)
            xs = _moe_res_ln(xs, moe_w_router[f], moe_b_router[f], wg, wu, wd, fg, fb, alpha)
    st = lambda k: jnp.stack(outs[k])
    return (xp.reshape(bp, seq, d), xs.reshape(db, dseq, d), st("ssm_p"), st("conv_p"), st("ssm_s"), st("conv_s"),
            st("gmv_s"), st("kv_p"), st("win_p"), st("kv_s"), st("win_s"))
```

```python
import functools

import jax
import jax.numpy as jnp
from jax import lax
from jax.experimental import pallas as pl
from jax.experimental.pallas import tpu as pltpu

F32 = jnp.float32
BF16 = jnp.bfloat16
I32 = jnp.int32

LN_EPS = 1e-5
NEG = -1e30
N_MIXERS = 3
SSD_P = 64
SSD_N = 128
SSD_G = 4
SSD_CONV = 4
SSD_CHUNK = 128
GM_CHUNK = 128
GM_G = 8
NSA_KV = 4
NSA_DH = 64
CMP_BLK = 32
CMP_STRIDE = 16
SEL_BLK = 64
N_SEL = 16
WINDOW = 512
FORCE = 1e4
TOP_K = 2

LANES = 128
SUBLANES = 8
MIB = 1 << 20


def _cparams(sem, vmem_mib):
    return pltpu.CompilerParams(dimension_semantics=sem, vmem_limit_bytes=vmem_mib * MIB)


def _row_tile(m, cands=(1024, 512, 256, 128, 64, 32, 16, 8)):
    for c in cands:
        if m % c == 0:
            return c
    raise ValueError(f"no row tile for {m}")


def _ln(v, g, b):
    mu = jnp.mean(v, -1, keepdims=True)
    d = v - mu
    var = jnp.mean(d * d, -1, keepdims=True)
    return d * lax.rsqrt(var + LN_EPS) * g + b


def _split(x, n):
    out = []
    r = x
    for k in range(n):
        h = r.astype(BF16)
        out.append(h)
        if k + 1 < n:
            r = r - h.astype(F32)
    return out


def _dot(a, b):
    return jnp.dot(a, b, preferred_element_type=F32)


def _dot_nt(a, b):
    return lax.dot_general(a, b, (((1,), (1,)), ((), ())), preferred_element_type=F32)


def _dot_tn(a, b):
    return lax.dot_general(a, b, (((0,), (0,)), ((), ())), preferred_element_type=F32)


def _silu(x):
    return x * jax.nn.sigmoid(x)


def _gelu_tanh(x):
    return 0.5 * x * (1.0 + jnp.tanh(0.7978845608028654 * (x + 0.044715 * (x * x * x))))


def _mm_kernel(x_ref, w_ref, b_ref, o_ref, *, act):
    acc = _dot(x_ref[...].astype(BF16), w_ref[...]) + b_ref[...]
    if act == "sigmoid":
        acc = jax.nn.sigmoid(acc)
    o_ref[...] = acc.astype(o_ref.dtype)


def _matmul(x, w, b=None, act=None, tn=None):
    m, k = x.shape
    n = w.shape[1]
    tm = _row_tile(m)
    tn = n if tn is None else tn
    if b is None:
        b = jnp.zeros((1, n), F32)
    return pl.pallas_call(
        functools.partial(_mm_kernel, act=act),
        grid=(m // tm, n // tn),
        in_specs=[pl.BlockSpec((tm, k), lambda i, j: (i, 0)),
                  pl.BlockSpec((k, tn), lambda i, j: (0, j)),
                  pl.BlockSpec((1, tn), lambda i, j: (0, j))],
        out_specs=pl.BlockSpec((tm, tn), lambda i, j: (i, j)),
        out_shape=jax.ShapeDtypeStruct((m, n), F32),
        compiler_params=_cparams(("parallel", "arbitrary"), 48),
        name="matmul",
    )(x, w, b)


def _mm_res_ln_kernel(y_ref, x_ref, w_ref, g_ref, b_ref, o_ref, *, alpha):
    f = _dot(y_ref[...].astype(BF16), w_ref[...])
    o_ref[...] = _ln(alpha * x_ref[...] + f, g_ref[...], b_ref[...])


def _mm_res_ln(y, x, w, g, b, alpha):
    m, k = y.shape
    d = x.shape[1]
    tm = _row_tile(m, (512, 256, 128))
    return pl.pallas_call(
        functools.partial(_mm_res_ln_kernel, alpha=alpha),
        grid=(m // tm,),
        in_specs=[pl.BlockSpec((tm, k), lambda i: (i, 0)),
                  pl.BlockSpec((tm, d), lambda i: (i, 0)),
                  pl.BlockSpec((k, d), lambda i: (0, 0)),
                  pl.BlockSpec((1, d), lambda i: (0, 0)),
                  pl.BlockSpec((1, d), lambda i: (0, 0))],
        out_specs=pl.BlockSpec((tm, d), lambda i: (i, 0)),
        out_shape=jax.ShapeDtypeStruct((m, d), F32),
        compiler_params=_cparams(("parallel",), 48),
        name="mm_res_ln",
    )(y, x, w, g, b)


def _swiglu_kernel(x_ref, wg_ref, wu_ref, wd_ref, g_ref, b_ref, o_ref, acc_ref, xb_ref, *, alpha):
    f = pl.program_id(1)

    @pl.when(f == 0)
    def _():
        acc_ref[...] = jnp.zeros_like(acc_ref)
        xb_ref[...] = x_ref[...].astype(BF16)

    xb = xb_ref[...]
    h = _silu(_dot(xb, wg_ref[...])) * _dot(xb, wu_ref[...])
    acc_ref[...] += _dot(h.astype(BF16), wd_ref[...])

    @pl.when(f == pl.num_programs(1) - 1)
    def _():
        o_ref[...] = _ln(alpha * x_ref[...] + acc_ref[...], g_ref[...], b_ref[...])


def _ff_tile(dff):
    for c in (512, 256, 128):
        if dff % c == 0:
            return c
    return dff


def _swiglu_res_ln(x, wg, wu, wd, g, b, alpha):
    m, d = x.shape
    dff = wg.shape[1]
    tm = _row_tile(m)
    tf = _ff_tile(dff)
    return pl.pallas_call(
        functools.partial(_swiglu_kernel, alpha=alpha),
        grid=(m // tm, dff // tf),
        in_specs=[pl.BlockSpec((tm, d), lambda i, f: (i, 0)),
                  pl.BlockSpec((d, tf), lambda i, f: (0, f)),
                  pl.BlockSpec((d, tf), lambda i, f: (0, f)),
                  pl.BlockSpec((tf, d), lambda i, f: (f, 0)),
                  pl.BlockSpec((1, d), lambda i, f: (0, 0)),
                  pl.BlockSpec((1, d), lambda i, f: (0, 0))],
        out_specs=pl.BlockSpec((tm, d), lambda i, f: (i, 0)),
        out_shape=jax.ShapeDtypeStruct((m, d), F32),
        scratch_shapes=[pltpu.VMEM((tm, d), F32), pltpu.VMEM((tm, d), BF16)],
        compiler_params=_cparams(("parallel", "arbitrary"), 48),
        name="swiglu_res_ln",
    )(x, wg, wu, wd, g, b)


def _ssd_kernel(xbc_ref, z_ref, dt_ref, conv0_ref, h0_ref, cw_ref, cb_ref, dtb_ref, alog_ref,
                dexp_ref, nw_ref, e_ref, ltri_ref, y_ref, h_ref, xp_scr, dtp_scr, yacc_scr,
                *, lv, n_heads):
    q = SSD_CHUNK
    di = n_heads * SSD_P
    gn = SSD_G * SSD_N
    hpg = n_heads // SSD_G
    gw = hpg * SSD_P
    ci = pl.program_id(1)

    @pl.when(ci == 0)
    def _():
        xp_scr[0:8, :] = conv0_ref[0]
        h_ref[0] = h0_ref[0]

    @pl.when(ci > 0)
    def _():
        xp_scr[0:8, :] = xp_scr[q:q + 8, :]

    if lv < q:
        xp_scr[8 + lv:, :] = jnp.zeros((q - lv, xp_scr.shape[1]), F32)
        dtp_scr[...] = jnp.zeros_like(dtp_scr)
        dtp_scr[0:lv, :] = dt_ref[...]
        dt_raw = dtp_scr[...]
    else:
        dt_raw = dt_ref[...]
    xp_scr[8:8 + lv, :] = xbc_ref[...]

    conv = cb_ref[...]
    for k in range(SSD_CONV):
        conv = conv + cw_ref[k:k + 1, :] * xp_scr[5 + k:5 + k + q, :]
    xc = _silu(conv)
    row = lax.broadcasted_iota(I32, (q, LANES), 0)
    lane = lax.broadcasted_iota(I32, (q, LANES), 1)
    if lv < q:
        xc = jnp.where(lax.broadcasted_iota(I32, xc.shape, 0) < lv, xc, 0.0)

    v = dt_raw + dtb_ref[...]
    dt = jnp.maximum(v, 0.0) + jnp.log1p(jnp.exp(-jnp.abs(v)))
    dt = jnp.where((row < lv) & (lane < n_heads), dt, 0.0)
    adt = dt * (-jnp.exp(alog_ref[...]))
    acum = sum(_dot(ltri_ref[...], t) for t in _split(adt, 3))
    acum_t = acum.T
    dt_t = dt.T
    ea = jnp.exp(acum)
    w = dt * jnp.exp(acum[q - 1:q, :] - acum)
    w_exp = sum(_dot(t, e_ref[...]) for t in _split(w, 2))
    ea_exp = sum(_dot(t, e_ref[...]) for t in _split(ea[:lv], 2))
    cdm = jnp.exp(jnp.broadcast_to(acum_t[:, q - 1:q], (LANES, LANES)))

    xs = xc[:, :di]
    xd = (xs * w_exp).astype(BF16)
    causal = (lax.broadcasted_iota(I32, (lv, q), 0) >= lax.broadcasted_iota(I32, (lv, q), 1))
    lo = lane < SSD_P

    for g in range(SSD_G):
        bg = xc[:, di + g * SSD_N:di + (g + 1) * SSD_N].astype(BF16)
        cg = xc[:lv, di + gn + g * SSD_N:di + gn + (g + 1) * SSD_N].astype(BF16)
        cbm = _dot_nt(cg, bg)
        hg = h_ref[0, g * gw:(g + 1) * gw, :].astype(BF16)
        yoff = _dot_nt(cg, hg)
        st = _dot_tn(xd[:, g * gw:(g + 1) * gw], bg)
        for pr in range(hpg // 2):
            col = g * gw + pr * LANES
            ms = []
            for hh in (2 * pr, 2 * pr + 1):
                h = g * hpg + hh
                seg = acum[:lv, h:h + 1] - acum_t[h:h + 1, :]
                dec = jnp.exp(jnp.where(causal, seg, -jnp.inf))
                ms.append((cbm * dec * dt_t[h:h + 1, :]).astype(BF16))
            xpair = xs[:, col:col + LANES]
            rhs = jnp.concatenate([jnp.where(lo, xpair, 0.0), jnp.where(lo, 0.0, xpair)], axis=0).astype(BF16)
            yd = _dot(jnp.concatenate(ms, axis=1), rhs)
            yacc_scr[:, col:col + LANES] = (yd + yoff[:, pr * LANES:(pr + 1) * LANES] * ea_exp[:, col:col + LANES]
                                            + dexp_ref[:, col:col + LANES] * xs[:lv, col:col + LANES])
        for hh in range(hpg):
            h = g * hpg + hh
            r0 = h * SSD_P
            h_ref[0, r0:r0 + SSD_P, :] = (h_ref[0, r0:r0 + SSD_P, :] * cdm[h:h + 1, :]
                                          + st[hh * SSD_P:(hh + 1) * SSD_P, :])

    y = yacc_scr[...] * _silu(z_ref[...])
    ng = di // SSD_G
    for g in range(SSD_G):
        yg = y[:, g * ng:(g + 1) * ng]
        ms_ = jnp.mean(yg * yg, -1, keepdims=True)
        y_ref[:, g * ng:(g + 1) * ng] = yg * lax.rsqrt(ms_ + LN_EPS) * nw_ref[:, g * ng:(g + 1) * ng]


def _ssd_scan(xbc, z, dt, conv0, h0, cw, cb, dtb, alog, dexp, nw, e, ltri, nb, lv):
    m, cdim = xbc.shape
    di = z.shape[1]
    n_heads = di // SSD_P
    nc = m // (nb * lv)
    return pl.pallas_call(
        functools.partial(_ssd_kernel, lv=lv, n_heads=n_heads),
        grid=(nb, nc),
        in_specs=[pl.BlockSpec((lv, cdim), lambda b, c: (b * nc + c, 0)),
                  pl.BlockSpec((lv, di), lambda b, c: (b * nc + c, 0)),
                  pl.BlockSpec((lv, LANES), lambda b, c: (b * nc + c, 0)),
                  pl.BlockSpec((1, 8, cdim), lambda b, c: (b, 0, 0)),
                  pl.BlockSpec((1, di, SSD_N), lambda b, c: (b, 0, 0)),
                  pl.BlockSpec((SSD_CONV, cdim), lambda b, c: (0, 0)),
                  pl.BlockSpec((1, cdim), lambda b, c: (0, 0)),
                  pl.BlockSpec((1, LANES), lambda b, c: (0, 0)),
                  pl.BlockSpec((1, LANES), lambda b, c: (0, 0)),
                  pl.BlockSpec((1, di), lambda b, c: (0, 0)),
                  pl.BlockSpec((1, di), lambda b, c: (0, 0)),
                  pl.BlockSpec((LANES, di), lambda b, c: (0, 0)),
                  pl.BlockSpec((SSD_CHUNK, SSD_CHUNK), lambda b, c: (0, 0))],
        out_specs=[pl.BlockSpec((lv, di), lambda b, c: (b * nc + c, 0)),
                   pl.BlockSpec((1, di, SSD_N), lambda b, c: (b, 0, 0))],
        out_shape=[jax.ShapeDtypeStruct((m, di), F32),
                   jax.ShapeDtypeStruct((nb, di, SSD_N), F32)],
        scratch_shapes=[pltpu.VMEM((SSD_CHUNK + 8, cdim), F32),
                        pltpu.VMEM((SSD_CHUNK, LANES), F32),
                        pltpu.VMEM((lv, di), F32)],
        compiler_params=_cparams(("parallel", "arbitrary"), 56),
        name="ssd_scan",
    )(xbc, z, dt, conv0, h0, cw, cb, dtb, alog, dexp, nw, e, ltri)


def _ssd_mixer(x, nb, lv, conv0, h0, prm):
    w_in, conv_w, conv_b, dt_bias, a_log, d_skip, norm_w = prm
    di = norm_w.shape[0]
    cdim = conv_w.shape[1]
    n_heads = di // SSD_P
    seq = x.shape[0] // nb
    w_z = w_in[:, :di].astype(BF16)
    w_x = w_in[:, di:di + cdim].astype(BF16)
    w_dt = jnp.pad(w_in[:, di + cdim:], ((0, 0), (0, LANES - n_heads))).astype(BF16)
    z = _matmul(x, w_z, tn=1024)
    xbc = _matmul(x, w_x, tn=1024)
    dt = _matmul(x, w_dt)
    pad1 = lambda a: jnp.pad(a.astype(F32), (0, LANES - n_heads))[None, :]
    e = (jnp.arange(di)[None, :] // SSD_P == jnp.arange(LANES)[:, None]).astype(BF16)
    ltri = (jnp.arange(SSD_CHUNK)[:, None] >= jnp.arange(SSD_CHUNK)[None, :]).astype(BF16)
    y, h_new = _ssd_scan(xbc, z, dt, conv0, h0.reshape(nb, di, SSD_N), conv_w, conv_b[None, :],
                         pad1(dt_bias), pad1(a_log), jnp.repeat(d_skip, SSD_P)[None, :], norm_w[None, :],
                         e, ltri, nb, lv)
    conv_new = xbc.reshape(nb, seq, cdim)[:, seq - (SSD_CONV - 1):]
    return y, conv_new, h_new.reshape(nb, n_heads, SSD_P, SSD_N)


def _gm_in_kernel(x_ref, w_ref, b_ref, g_ref, bb_ref, u_ref, v_ref):
    h = _gelu_tanh(_dot(x_ref[...].astype(BF16), w_ref[...]) + b_ref[...])
    d = u_ref.shape[1]
    u_ref[...] = h[:, :d]
    v_ref[...] = _ln(h[:, d:], g_ref[...], bb_ref[...])


def _gm_in(x, w, b, g, bb):
    m, k = x.shape
    d = w.shape[1] // 2
    tm = _row_tile(m, (512, 256))
    return pl.pallas_call(
        _gm_in_kernel,
        grid=(m // tm,),
        in_specs=[pl.BlockSpec((tm, k), lambda i: (i, 0)),
                  pl.BlockSpec((k, 2 * d), lambda i: (0, 0)),
                  pl.BlockSpec((1, 2 * d), lambda i: (0, 0)),
                  pl.BlockSpec((1, d), lambda i: (0, 0)),
                  pl.BlockSpec((1, d), lambda i: (0, 0))],
        out_specs=[pl.BlockSpec((tm, d), lambda i: (i, 0)), pl.BlockSpec((tm, d), lambda i: (i, 0))],
        out_shape=[jax.ShapeDtypeStruct((m, d), F32), jax.ShapeDtypeStruct((m, d), F32)],
        compiler_params=_cparams(("parallel",), 48),
        name="gm_in",
    )(x, w, b, g, bb)


def _gm_out_kernel(u_ref, v_ref, x_ref, s_ref, sb_ref, w_ref, g_ref, b_ref, o_ref, gated_scr, *, alpha, r):
    tb = u_ref.shape[0]
    gd = u_ref.shape[1] // GM_G
    for s in range(tb // r):
        rows = slice(s * r, (s + 1) * r)
        for g in range(GM_G):
            cols = slice(g * gd, (g + 1) * gd)
            mixed = _dot(s_ref[g], v_ref[rows, cols].astype(BF16)) + sb_ref[rows, cols]
            gated_scr[rows, cols] = (u_ref[rows, cols] * mixed).astype(BF16)
    f = _dot(gated_scr[...], w_ref[...])
    o_ref[...] = _ln(alpha * x_ref[...] + f, g_ref[...], b_ref[...])


def _gm_out(u, v, x, smat, sbias, w, g, b, alpha, tb):
    m, d = u.shape
    r = smat.shape[1]
    return pl.pallas_call(
        functools.partial(_gm_out_kernel, alpha=alpha, r=r),
        grid=(m // tb,),
        in_specs=[pl.BlockSpec((tb, d), lambda i: (i, 0)),
                  pl.BlockSpec((tb, d), lambda i: (i, 0)),
                  pl.BlockSpec((tb, d), lambda i: (i, 0)),
                  pl.BlockSpec((GM_G, r, r), lambda i: (0, 0, 0)),
                  pl.BlockSpec((tb, d), lambda i: (0, 0)),
                  pl.BlockSpec((d, d), lambda i: (0, 0)),
                  pl.BlockSpec((1, d), lambda i: (0, 0)),
                  pl.BlockSpec((1, d), lambda i: (0, 0))],
        out_specs=pl.BlockSpec((tb, d), lambda i: (i, 0)),
        out_shape=jax.ShapeDtypeStruct((m, d), F32),
        scratch_shapes=[pltpu.VMEM((tb, d), BF16)],
        compiler_params=_cparams(("parallel",), 48),
        name="gm_out",
    )(u, v, x, smat, sbias, w, g, b)


def _gmlp_mixer(x, nb, seq, prm, ln_g, ln_b, alpha):
    w_in, b_in, g_in, bb_in, w_s, b_s, w_out = prm
    d = w_out.shape[0]
    gd = d // GM_G
    u, v = _gm_in(x, w_in.astype(BF16), b_in[None, :], g_in[None, :], bb_in[None, :])
    ws = jnp.tril(w_s)
    if seq % GM_CHUNK == 0:
        tb = 512
        smat = ws.astype(BF16)
        bias_rows = jnp.repeat(b_s.T, gd, axis=1)
        sbias = jnp.tile(bias_rows, (tb // GM_CHUNK, 1))
    else:
        tb = nb * seq
        smat = jnp.einsum("ab,gts->gatbs", jnp.eye(nb, dtype=F32), ws[:, :seq, :seq]).reshape(GM_G, tb, tb).astype(BF16)
        sbias = jnp.tile(jnp.repeat(b_s.T[:seq], gd, axis=1), (nb, 1))
    out = _gm_out(u, v, x, smat, sbias, w_out.astype(BF16), ln_g, ln_b, alpha, tb)
    return out, v


def _router_kernel(x_ref, wh_ref, wl_ref, b_ref, eid_ref, gate_ref):
    x = x_ref[...]
    xh, xl = _split(x, 2)
    logits = _dot(xh, wh_ref[...]) + _dot(xl, wh_ref[...]) + _dot(xh, wl_ref[...]) + b_ref[...]
    lane = lax.broadcasted_iota(I32, logits.shape, 1)
    m1 = jnp.max(logits, -1, keepdims=True)
    i1 = jnp.min(jnp.where(logits == m1, lane, LANES), -1, keepdims=True)
    rest = jnp.where(lane == i1, NEG * 2, logits)
    m2 = jnp.max(rest, -1, keepdims=True)
    i2 = jnp.min(jnp.where(rest == m2, lane, LANES), -1, keepdims=True)
    e = jnp.exp(m2 - m1)
    g1 = 1.0 / (1.0 + e)
    eid_ref[...] = jnp.where(lane == 0, i1, jnp.where(lane == 1, i2, 0))
    gate_ref[...] = jnp.where(lane == 0, g1, jnp.where(lane == 1, e * g1, 0.0))


def _router(x, w_router, b_router):
    m, d = x.shape
    ne = w_router.shape[1]
    w = jnp.pad(w_router, ((0, 0), (0, LANES - ne)))
    wh = w.astype(BF16)
    wl = (w - wh.astype(F32)).astype(BF16)
    b = jnp.pad(b_router.astype(F32), (0, LANES - ne), constant_values=NEG)[None, :]
    tm = _row_tile(m, (512, 256))
    return pl.pallas_call(
        _router_kernel,
        grid=(m // tm,),
        in_specs=[pl.BlockSpec((tm, d), lambda i: (i, 0)),
                  pl.BlockSpec((d, LANES), lambda i: (0, 0)),
                  pl.BlockSpec((d, LANES), lambda i: (0, 0)),
                  pl.BlockSpec((1, LANES), lambda i: (0, 0))],
        out_specs=[pl.BlockSpec((tm, LANES), lambda i: (i, 0)), pl.BlockSpec((tm, LANES), lambda i: (i, 0))],
        out_shape=[jax.ShapeDtypeStruct((m, LANES), I32), jax.ShapeDtypeStruct((m, LANES), F32)],
        compiler_params=_cparams(("parallel",), 32),
        name="router",
    )(x, wh, wl, b)


def _row_copy(src_hbm, dst, s_row, d_row, sem):
    return pltpu.make_async_copy(src_hbm.at[pl.ds(s_row, 1), :], dst.at[pl.ds(d_row, 1), :], sem)


def _row_gather(src_hbm, dst, idx_ref, n, per_row, sem, start):
    def body(j, c):
        for k in range(per_row):
            row = idx_ref[0, 0, per_row * j + k] if start else 0
            cp = _row_copy(src_hbm, dst.at[k], row, j, sem)
            cp.start() if start else cp.wait()
        return c

    lax.fori_loop(0, n, body, 0, unroll=8)


def _moe_ffn_kernel(be_ref, nu_ref, idx_ref, idxn_ref, x_hbm, wg_ref, wu_ref, wd_ref, o_ref,
                    acc_ref, xg_ref, xb_ref, sem, *, tb):
    i = pl.program_id(0)
    f = pl.program_id(1)
    last = pl.num_programs(1) - 1
    n_used = nu_ref[0]
    used = i < n_used
    slot = i % 2

    @pl.when((f == 0) & (i == 0))
    def _():
        _row_gather(x_hbm, xg_ref.at[0], idx_ref, tb, 1, sem.at[0], True)

    @pl.when((f == 0) & (i + 1 < n_used))
    def _():
        _row_gather(x_hbm, xg_ref.at[1 - slot], idxn_ref, tb, 1, sem.at[1 - slot], True)

    @pl.when(used & (f == 0))
    def _():
        _row_gather(x_hbm, xg_ref.at[slot], idx_ref, tb, 1, sem.at[slot], False)
        acc_ref[...] = jnp.zeros_like(acc_ref)
        xb_ref[...] = xg_ref[slot, 0].astype(BF16)

    @pl.when(used)
    def _():
        xb = xb_ref[...]
        h = _silu(_dot(xb, wg_ref[0])) * _dot(xb, wu_ref[0])
        acc_ref[...] += _dot(h.astype(BF16), wd_ref[0])

    @pl.when(used & (f == last))
    def _():
        o_ref[...] = acc_ref[...]

    @pl.when(jnp.logical_not(used) & (f == last))
    def _():
        o_ref[...] = jnp.zeros_like(o_ref)


def _moe_ffn(x, buf_tok, blk_e, n_used, wg, wu, wd, tb):
    d = x.shape[1]
    nblk = buf_tok.shape[0] // tb
    dff = wg.shape[2]
    tf = _ff_tile(dff)
    nf = dff // tf

    def fe(i, f, be, nu):
        return jnp.where(i < nu[0], f, nf - 1)

    grid_spec = pltpu.PrefetchScalarGridSpec(
        num_scalar_prefetch=2,
        grid=(nblk, nf),
        in_specs=[pl.BlockSpec((1, 1, tb), lambda i, f, be, nu: (i, 0, 0), memory_space=pltpu.SMEM),
                  pl.BlockSpec((1, 1, tb), lambda i, f, be, nu: (jnp.minimum(i + 1, nblk - 1), 0, 0),
                               memory_space=pltpu.SMEM),
                  pl.BlockSpec(memory_space=pl.ANY),
                  pl.BlockSpec((1, d, tf), lambda i, f, be, nu: (be[i], 0, fe(i, f, be, nu))),
                  pl.BlockSpec((1, d, tf), lambda i, f, be, nu: (be[i], 0, fe(i, f, be, nu))),
                  pl.BlockSpec((1, tf, d), lambda i, f, be, nu: (be[i], fe(i, f, be, nu), 0))],
        out_specs=pl.BlockSpec((tb, d), lambda i, f, be, nu: (i, 0)),
        scratch_shapes=[pltpu.VMEM((tb, d), F32), pltpu.VMEM((2, 1, tb, d), F32), pltpu.VMEM((tb, d), BF16),
                        pltpu.SemaphoreType.DMA((2,))])
    idx = buf_tok.reshape(nblk, 1, tb)
    return pl.pallas_call(
        functools.partial(_moe_ffn_kernel, tb=tb),
        grid_spec=grid_spec,
        out_shape=jax.ShapeDtypeStruct((nblk * tb, d), F32),
        compiler_params=_cparams(("arbitrary", "arbitrary"), 48),
        name="moe_ffn",
    )(blk_e, n_used, idx, idx, x, wg, wu, wd)


def _combine_kernel(idx_ref, idxn_ref, yb_hbm, x_ref, gate_ref, g_ref, b_ref, o_ref, ybuf, sem, *, alpha, tm):
    i = pl.program_id(0)
    slot = i % 2

    @pl.when(i == 0)
    def _():
        _row_gather(yb_hbm, ybuf.at[0], idx_ref, tm, TOP_K, sem.at[0], True)

    @pl.when(i + 1 < pl.num_programs(0))
    def _():
        _row_gather(yb_hbm, ybuf.at[1 - slot], idxn_ref, tm, TOP_K, sem.at[1 - slot], True)

    _row_gather(yb_hbm, ybuf.at[slot], idx_ref, tm, TOP_K, sem.at[slot], False)
    gate = gate_ref[...]
    y = gate[:, 0:1] * ybuf[slot, 0] + gate[:, 1:2] * ybuf[slot, 1]
    o_ref[...] = _ln(alpha * x_ref[...] + y, g_ref[...], b_ref[...])


def _moe_combine(yb, dest, x, gate, g, b, alpha):
    m, d = x.shape
    tm = _row_tile(m, (256, 128))
    nblk = m // tm
    idx = dest.reshape(nblk, 1, TOP_K * tm)
    return pl.pallas_call(
        functools.partial(_combine_kernel, alpha=alpha, tm=tm),
        grid=(nblk,),
        in_specs=[pl.BlockSpec((1, 1, TOP_K * tm), lambda i: (i, 0, 0), memory_space=pltpu.SMEM),
                  pl.BlockSpec((1, 1, TOP_K * tm), lambda i: (jnp.minimum(i + 1, nblk - 1), 0, 0),
                               memory_space=pltpu.SMEM),
                  pl.BlockSpec(memory_space=pl.ANY),
                  pl.BlockSpec((tm, d), lambda i: (i, 0)),
                  pl.BlockSpec((tm, LANES), lambda i: (i, 0)),
                  pl.BlockSpec((1, d), lambda i: (0, 0)),
                  pl.BlockSpec((1, d), lambda i: (0, 0))],
        out_specs=pl.BlockSpec((tm, d), lambda i: (i, 0)),
        out_shape=jax.ShapeDtypeStruct((m, d), F32),
        scratch_shapes=[pltpu.VMEM((2, TOP_K, tm, d), F32), pltpu.SemaphoreType.DMA((2,))],
        compiler_params=_cparams(("arbitrary",), 32),
        name="moe_combine",
    )(idx, idx, yb, x, gate, g, b)


def _moe_res_ln(x, w_router, b_router, wg, wu, wd, g, b, alpha):
    m, d = x.shape
    ne = w_router.shape[1]
    eid, gate = _router(x, w_router, b_router)
    tk = m * TOP_K
    tb = 512 if tk >= 8192 else 128
    ef = eid[:, :TOP_K].reshape(-1)
    onehot = (ef[:, None] == jnp.arange(ne, dtype=I32)[None, :]).astype(I32)
    csum = jnp.cumsum(onehot, axis=0)
    rank = jnp.sum(csum * onehot, axis=1) - 1
    counts = csum[-1]
    padded = (counts + tb - 1) // tb * tb
    pend = jnp.cumsum(padded)
    dest = (pend - padded)[ef] + rank
    nblk = -(-tk // tb) + ne
    buf_tok = jnp.zeros((nblk * tb,), I32).at[dest].set(jnp.arange(tk, dtype=I32) // TOP_K)
    blk_start = jnp.arange(nblk, dtype=I32) * tb
    blk_e = jnp.minimum(jnp.sum((pend[None, :] <= blk_start[:, None]).astype(I32), axis=1), ne - 1).astype(I32)
    n_used = (pend[-1:] // tb).astype(I32)
    yb = _moe_ffn(x, buf_tok, blk_e, n_used, wg, wu, wd, tb)
    return _moe_combine(yb, dest.astype(I32), x, gate, g, b, alpha)


PAGE = 128
ATT_KT = 512


def _compress_kernel(pt_ref, pages_hbm, w1_ref, pos_ref, b1_ref, w2_ref, b2_ref, o_ref, scr, acc_scr, cv_scr, sem,
                     *, npg, col0):
    b = pl.program_id(0)
    pw = NSA_KV * NSA_DH
    nsub = npg * PAGE // CMP_STRIDE
    hid = b1_ref.shape[2]

    nslab = 2 * pw // LANES
    gps = LANES // NSA_DH

    def copies(p, pg):
        row = pl.multiple_of(p * PAGE, PAGE)
        return [pltpu.make_async_copy(pages_hbm.at[pg, :, pl.ds(col0 + sl * LANES, LANES)],
                                      scr.at[sl, pl.ds(row, PAGE), :], sem) for sl in range(nslab)]

    def start(p, c_):
        for cp in copies(p, pt_ref[b, p]):
            cp.start()
        return c_

    lax.fori_loop(0, npg, start, 0)

    @pl.when(b == 0)
    def _():
        for c in range(2):
            cv = jnp.zeros((SUBLANES, hid), F32) + b1_ref[c]
            for o in range(CMP_STRIDE):
                cv = cv + _dot(pos_ref[c, o].astype(BF16), w1_ref[c, o])[:, :hid]
                cv = cv + _dot(pos_ref[c, CMP_STRIDE + o].astype(BF16), w1_ref[c, o])[:, hid:]
            cv_scr[c] = cv

    def wait(p, c_):
        for cp in copies(p, 0):
            cp.wait()
        return c_

    lax.fori_loop(0, npg, wait, 0)

    for c in range(2):
        acc_scr[...] = jnp.zeros_like(acc_scr)
        for o in range(CMP_STRIDE):
            for sl in range(nslab // 2):
                xo = scr[c * (nslab // 2) + sl, pl.ds(o, nsub, stride=CMP_STRIDE), :].astype(BF16)
                for gg in range(gps):
                    acc_scr[sl * gps + gg] += _dot(xo[:, gg * NSA_DH:(gg + 1) * NSA_DH], w1_ref[c, o])
        for g in range(NSA_KV):
            acc = acc_scr[g]
            pre = acc[:, :hid] + pltpu.roll(acc[:, hid:], nsub - 1, 0) + cv_scr[c, 0:1, :]
            o_ref[0, c, g] = _dot(_gelu_tanh(pre).astype(BF16), w2_ref[c]) + b2_ref[c]


def _compress(pages, page_table, pos, w1, b1, w2, b2, col0=0):
    nb, npg = page_table.shape
    nsub = npg * PAGE // CMP_STRIDE
    hid = w1.shape[-1]
    pw = NSA_KV * NSA_DH
    w1cat = jnp.concatenate([w1[:, :CMP_STRIDE], w1[:, CMP_STRIDE:]], axis=-1).astype(BF16)
    posb = jnp.broadcast_to(pos[:, :, None, :], (2, CMP_BLK, SUBLANES, NSA_DH))
    grid_spec = pltpu.PrefetchScalarGridSpec(
        num_scalar_prefetch=1,
        grid=(nb,),
        in_specs=[pl.BlockSpec(memory_space=pl.ANY),
                  pl.BlockSpec((2, CMP_STRIDE, NSA_DH, 2 * hid), lambda b, pt: (0, 0, 0, 0)),
                  pl.BlockSpec((2, CMP_BLK, SUBLANES, NSA_DH), lambda b, pt: (0, 0, 0, 0)),
                  pl.BlockSpec((2, 1, hid), lambda b, pt: (0, 0, 0)),
                  pl.BlockSpec((2, hid, NSA_DH), lambda b, pt: (0, 0, 0)),
                  pl.BlockSpec((2, 1, NSA_DH), lambda b, pt: (0, 0, 0))],
        out_specs=pl.BlockSpec((1, 2, NSA_KV, nsub, NSA_DH), lambda b, pt: (b, 0, 0, 0, 0)),
        scratch_shapes=[pltpu.VMEM((2 * pw // LANES, npg * PAGE, LANES), F32),
                        pltpu.VMEM((NSA_KV, nsub, 2 * hid), F32),
                        pltpu.VMEM((2, SUBLANES, hid), F32),
                        pltpu.SemaphoreType.DMA(())])
    return pl.pallas_call(
        functools.partial(_compress_kernel, npg=npg, col0=col0),
        grid_spec=grid_spec,
        out_shape=jax.ShapeDtypeStruct((nb, 2, NSA_KV, nsub, NSA_DH), F32),
        compiler_params=_cparams(("arbitrary",), 56),
        name="nsa_compress",
    )(page_table, pages, w1cat, posb, b1[:, None, :], w2.astype(BF16), b2[:, None, :])


def _masked_softmax(s, valid):
    s = jnp.where(valid, s, NEG)
    p = jnp.exp(s - jnp.max(s, -1, keepdims=True))
    return p / jnp.sum(p, -1, keepdims=True)


def _select_blocks(imp, ovl, tq, nbs):
    score = sum(_dot(t, ovl) for t in _split(imp, 3))
    jb = lax.broadcasted_iota(I32, score.shape, 1)
    ok = jb * SEL_BLK <= tq
    cur = tq // SEL_BLK
    forced = (jb == 0) | (jb == cur) | (jb == cur - 1)
    score = jnp.where(ok, score + jnp.where(forced, FORCE, 0.0), NEG)
    rank = jnp.zeros(score.shape, F32)
    for j2 in range(nbs):
        cj = score[:, j2:j2 + 1]
        beats = (cj > score) | ((cj == score) & (j2 < jb))
        rank = rank + jnp.where(beats, 1.0, 0.0)
    return jnp.where(ok & (rank < N_SEL), 1.0, 0.0)


def _expand_sel(sel, first_blk, nkeys):
    jj = lax.broadcasted_iota(I32, (sel.shape[1], nkeys), 0)
    kk = lax.broadcasted_iota(I32, (sel.shape[1], nkeys), 1)
    expand = jnp.where(kk // SEL_BLK + first_blk == jj, 1.0, 0.0).astype(BF16)
    return _dot(sel.astype(BF16), expand)


POS_SPLIT = 64
Q_EXTRA = 16


def _nsa_prompt_kernel(q_ref, gate_ref, slope_ref, kc_ref, vc_ref, ks_ref, vs_ref, kw_ref, vw_ref, ovl_ref, o_ref,
                       *, nbs, qblk, nrep):
    qb = pl.program_id(2)
    rows = qblk * nrep
    q = q_ref[0, 0, 0]
    q16 = q.astype(BF16)
    slope = slope_ref[0][:, 0:1]
    tq = qb * qblk + lax.broadcasted_iota(I32, (rows, 1), 0) % qblk
    tq1 = tq[0:qblk]

    kc = kc_ref[0, 0]
    qh, ql = _split(q[:, :NSA_DH], 2)
    kh, kl = _split(kc, 2)
    s = _dot_nt(qh, kh) + _dot_nt(ql, kh) + _dot_nt(qh, kl)
    e = lax.broadcasted_iota(I32, (1, kc.shape[0]), 1) * CMP_STRIDE + (CMP_BLK - 1)
    valid = e <= tq
    p = jnp.where(valid, _masked_softmax(s + slope * e.astype(F32), valid), 0.0)
    o_cmp = _dot(p.astype(BF16), vc_ref[0, 0].astype(BF16))
    imp = p[0:qblk]
    for r in range(1, nrep):
        imp = imp + p[r * qblk:(r + 1) * qblk]

    score = sum(_dot(ovl_ref[...], t) for t in _split(imp.T, 3))
    jb = lax.broadcasted_iota(I32, score.shape, 0)
    tq_row = qb * qblk + lax.broadcasted_iota(I32, (1, qblk), 1)
    ok = jb * SEL_BLK <= tq_row
    cur = tq_row // SEL_BLK
    forced = (jb == 0) | (jb == cur) | (jb == cur - 1)
    score = jnp.where(ok, score + jnp.where(forced, FORCE, 0.0), NEG)
    rank = jnp.zeros(score.shape, F32)
    for j2 in range(nbs):
        cj = score[j2:j2 + 1, :]
        beats = (cj > score) | ((cj == score) & (j2 < jb))
        rank = rank + jnp.where(beats, 1.0, 0.0)
    selneg = jnp.where(ok & (rank < N_SEL), 0.0, NEG).astype(BF16)

    def attend(k_ref, v_ref, kt_lo, kt_hi, bias_fn):
        def body(kt, carry):
            m, l, acc = carry
            off = pl.multiple_of(kt * ATT_KT, ATT_KT)
            kpos = off + lax.broadcasted_iota(I32, (1, ATT_KT), 1)
            r = lax.broadcasted_iota(I32, (Q_EXTRA, ATT_KT), 0)
            pos_rows = jnp.where((r == 0) | (r == 2), (kpos // POS_SPLIT).astype(F32),
                                 jnp.where((r == 1) | (r == 3), (kpos % POS_SPLIT).astype(F32), 0.0))
            k = jnp.concatenate([k_ref[0, 0, :, pl.ds(off, ATT_KT)].astype(BF16), pos_rows.astype(BF16)], axis=0)
            v = v_ref[0, 0, :, pl.ds(off, ATT_KT)].astype(BF16)
            s_ = _dot(q16, k) + jnp.concatenate([bias_fn(kt, kpos)] * nrep, axis=0)
            m_new = jnp.maximum(m, jnp.max(s_, -1, keepdims=True))
            a = jnp.exp(m - m_new)
            p_ = jnp.exp(s_ - m_new)
            return m_new, l * a + jnp.sum(p_, -1, keepdims=True), acc * a + _dot_nt(p_.astype(BF16), v)

        init = (jnp.full((rows, 1), -jnp.inf, F32), jnp.zeros((rows, 1), F32), jnp.zeros((rows, NSA_DH), F32))
        m, l, acc = lax.fori_loop(kt_lo, kt_hi, body, init)
        return acc / l

    def sel_bias(kt, kpos):
        jj = lax.broadcasted_iota(I32, (selneg.shape[0], ATT_KT), 0)
        expand = jnp.where(kpos // SEL_BLK == jj, 1.0, 0.0).astype(BF16)
        return jnp.where(kpos <= tq1, _dot_tn(selneg, expand), NEG)

    def win_bias(kt, kpos):
        return jnp.where((kpos <= tq1) & (kpos > tq1 - WINDOW), 0.0, NEG)

    kt_hi = (qb * qblk + qblk - 1) // ATT_KT + 1
    o_sel = attend(ks_ref, vs_ref, 0, kt_hi, sel_bias)
    o_win = attend(kw_ref, vw_ref, jnp.maximum(qb * qblk - (WINDOW - 1), 0) // ATT_KT, kt_hi, win_bias)
    gate = gate_ref[0, 0, 0]
    o_ref[0, 0, 0] = gate[:, 0:1] * o_cmp + gate[:, 1:2] * o_sel + gate[:, 2:3] * o_win


def _nsa_prompt(q, gates, slopes, kc, vc, kvt, ovl_t, nbs):
    nb, ng, nqb, rows, qw = q.shape
    dh = qw - Q_EXTRA
    seq = kvt.shape[3]
    nbc = kc.shape[2]
    qblk = seq // nqb
    nrep = rows // qblk
    qspec = lambda w: pl.BlockSpec((1, 1, 1, rows, w), lambda b, g, i: (b, g, i, 0, 0))
    cspec = pl.BlockSpec((1, 1, nbc, dh), lambda b, g, i: (b, g, 0, 0))
    tspec = lambda c: pl.BlockSpec((1, 1, dh, seq), lambda b, g, i: (b, c * ng + g, 0, 0))
    return pl.pallas_call(
        functools.partial(_nsa_prompt_kernel, nbs=nbs, qblk=qblk, nrep=nrep),
        grid=(nb, ng, nqb),
        in_specs=[qspec(qw), qspec(SUBLANES),
                  pl.BlockSpec((1, rows, SUBLANES), lambda b, g, i: (g, 0, 0)),
                  cspec, cspec, tspec(2), tspec(3), tspec(4), tspec(5),
                  pl.BlockSpec(ovl_t.shape, lambda b, g, i: (0, 0))],
        out_specs=qspec(dh),
        out_shape=jax.ShapeDtypeStruct((nb, ng, nqb, rows, dh), F32),
        compiler_params=_cparams(("parallel", "parallel", "arbitrary"), 48),
        name="nsa_prompt",
    )(q, gates, slopes, kc, vc, kvt, kvt, kvt, kvt, ovl_t)


def _nsa_sample_kernel(pt_ref, q_ref, gate_ref, slope_ref, kc_ref, vc_ref,
                       ks0, ks1, ks2, ks3, vs0, vs1, vs2, vs3, ksn_ref, vsn_ref, kwp_ref, vwp_ref, kwn_ref, vwn_ref,
                       ovl_ref, o_ref, sel_scr, m_scr, l_scr, acc_scr, part_scr, *, nbs, past, nq, nrep):
    t = pl.program_id(1)
    rows = q_ref.shape[1]
    pw = q_ref.shape[2]
    grows = rows // NSA_KV
    scale = NSA_DH ** -0.5
    q = q_ref[0]
    q16 = q.astype(BF16)
    slope = slope_ref[:, 0:1]
    tq = past + lax.broadcasted_iota(I32, (rows, 1), 0) % nq
    tqf = tq.astype(F32)
    gate = gate_ref[0]

    def logits(qk, kpos):
        return qk * scale - slope * (tqf - kpos.astype(F32))

    @pl.when(t == 0)
    def _():
        kc = kc_ref[0]
        qh, ql = _split(q, 2)
        kh, kl = _split(kc, 2)
        s = (_dot_nt(qh, kh) + _dot_nt(ql, kh) + _dot_nt(qh, kl)) * scale
        e = lax.broadcasted_iota(I32, (1, kc.shape[0]), 1) * CMP_STRIDE + (CMP_BLK - 1)
        valid = e <= tq
        p = jnp.where(valid, _masked_softmax(s - slope * (tqf - e.astype(F32)), valid), 0.0)
        o_cmp = _dot(p.astype(BF16), vc_ref[0].astype(BF16))
        imps = []
        for g in range(NSA_KV):
            a = p[g * grows:g * grows + nq]
            for r in range(1, nrep):
                a = a + p[g * grows + r * nq:g * grows + (r + 1) * nq]
            imps.append(a)
        imp = jnp.concatenate(imps, axis=0)
        tq_s = past + lax.broadcasted_iota(I32, (NSA_KV * nq, 1), 0) % nq
        sel_s = _select_blocks(imp, ovl_ref[...], tq_s, nbs)
        sel = jnp.concatenate([sel_s[g * nq:(g + 1) * nq] for g in range(NSA_KV) for _ in range(nrep)], axis=0)
        sel_scr[...] = sel

        wb = kwp_ref.shape[3]
        npad = LANES - nq
        zpad = jnp.zeros((npad, pw), F32)
        kwn = jnp.concatenate([kwn_ref[0], zpad], axis=0).astype(BF16)
        vwn = jnp.concatenate([vwn_ref[0], zpad], axis=0).astype(BF16)
        qk = jnp.concatenate([_dot(q16, kwp_ref[0, 0].astype(BF16)), _dot_nt(q16, kwn)], axis=1)
        idx = lax.broadcasted_iota(I32, (1, wb + LANES), 1)
        kpos = past - wb + idx
        valid = (idx < wb + nq) & (kpos <= tq) & (kpos > tq - WINDOW)
        pw_ = _masked_softmax(logits(qk, kpos), valid).astype(BF16)
        o_win = _dot_nt(pw_[:, :wb], vwp_ref[0, 0].astype(BF16)) + _dot(pw_[:, wb:], vwn)
        part_scr[...] = gate[:, 0:1] * o_cmp + gate[:, 2:3] * o_win

        kn = jnp.concatenate([ksn_ref[0], zpad], axis=0).astype(BF16)
        vn = jnp.concatenate([vsn_ref[0], zpad], axis=0).astype(BF16)
        idx = lax.broadcasted_iota(I32, (1, LANES), 1)
        kpos = past + idx
        blk = past // SEL_BLK
        valid = (idx < nq) & (kpos <= tq) & (sel[:, blk:blk + 1] > 0.5)
        s = jnp.where(valid, logits(_dot_nt(q16, kn), kpos), NEG)
        m = jnp.max(s, -1, keepdims=True)
        p = jnp.exp(s - m)
        m_scr[...] = m
        l_scr[...] = jnp.sum(p, -1, keepdims=True)
        acc_scr[...] = _dot(p.astype(BF16), vn)

    k = jnp.concatenate([ks0[0, 0], ks1[0, 0], ks2[0, 0], ks3[0, 0]], axis=1).astype(BF16)
    v = jnp.concatenate([vs0[0, 0], vs1[0, 0], vs2[0, 0], vs3[0, 0]], axis=1).astype(BF16)
    kpos = t * ATT_KT + lax.broadcasted_iota(I32, (1, ATT_KT), 1)
    selk = _expand_sel(sel_scr[...], t * (ATT_KT // SEL_BLK), ATT_KT)
    s = jnp.where(selk > 0.5, logits(_dot(q16, k), kpos), NEG)
    m = m_scr[...]
    m_new = jnp.maximum(m, jnp.max(s, -1, keepdims=True))
    a = jnp.exp(m - m_new)
    p = jnp.exp(s - m_new)
    m_scr[...] = m_new
    l_scr[...] = l_scr[...] * a + jnp.sum(p, -1, keepdims=True)
    acc_scr[...] = acc_scr[...] * a + _dot_nt(p.astype(BF16), v)

    @pl.when(t == pl.num_programs(1) - 1)
    def _():
        tot = part_scr[...] + gate[:, 1:2] * (acc_scr[...] / l_scr[...])
        o_ref[0] = jnp.concatenate([tot[g * grows:(g + 1) * grows, g * NSA_DH:(g + 1) * NSA_DH]
                                    for g in range(NSA_KV)], axis=0)


def _nsa_sample(qbd, gates, slopes, kc, vc, cache_t, page_table, kv_new, win_t, win_new, ovl, nbs, nq, nrep):
    nb, rows, pw = qbd.shape
    npg = page_table.shape[1]
    past = npg * PAGE
    ppt = ATT_KT // PAGE
    nt = npg // ppt
    nbc = kc.shape[1]
    wb = win_t.shape[3]
    const = lambda shape: pl.BlockSpec(shape, lambda b, t, pt: (0,) * len(shape))
    per_b = lambda n, w, col=0: pl.BlockSpec((1, n, w), lambda b, t, pt: (b, 0, col))
    page = lambda c, i: pl.BlockSpec((1, 1, pw, PAGE), lambda b, t, pt: (pt[b, ppt * t + i], c, 0, 0))
    wspec = lambda c: pl.BlockSpec((1, 1, pw, wb), lambda b, t, pt: (b, c, 0, 0))
    grid_spec = pltpu.PrefetchScalarGridSpec(
        num_scalar_prefetch=1,
        grid=(nb, nt),
        in_specs=[per_b(rows, pw), per_b(rows, SUBLANES), const((rows, SUBLANES)), per_b(nbc, pw), per_b(nbc, pw)]
                 + [page(2, i) for i in range(ppt)] + [page(3, i) for i in range(ppt)]
                 + [per_b(nq, pw, 2), per_b(nq, pw, 3), wspec(0), wspec(1),
                    per_b(nq, pw, 0), per_b(nq, pw, 1), const(ovl.shape)],
        out_specs=pl.BlockSpec((1, rows, NSA_DH), lambda b, t, pt: (b, 0, 0)),
        scratch_shapes=[pltpu.VMEM((rows, ovl.shape[1]), F32), pltpu.VMEM((rows, 1), F32), pltpu.VMEM((rows, 1), F32),
                        pltpu.VMEM((rows, pw), F32), pltpu.VMEM((rows, pw), F32)])
    return pl.pallas_call(
        functools.partial(_nsa_sample_kernel, nbs=nbs, past=past, nq=nq, nrep=nrep),
        grid_spec=grid_spec,
        out_shape=jax.ShapeDtypeStruct((nb, rows, NSA_DH), F32),
        compiler_params=_cparams(("parallel", "arbitrary"), 48),
        name="nsa_sample",
    )(page_table, qbd, gates, slopes, kc, vc, *([cache_t] * (2 * ppt)), kv_new, kv_new, win_t, win_t,
      win_new, win_new, ovl)


def _overlap(nbc, nbs_pad):
    ci = jnp.arange(nbc)[:, None] * CMP_STRIDE
    sj = jnp.arange(nbs_pad)[None, :] * SEL_BLK
    return ((ci < sj + SEL_BLK) & (ci + CMP_BLK > sj)).astype(BF16)


def _alibi_slopes(nh):
    return jnp.exp2(-8.0 * (jnp.arange(nh, dtype=F32) + 1.0) / nh)


def _nsa_in(x, w_in, nh):
    qw = nh * NSA_DH
    kvw = 6 * NSA_KV * NSA_DH
    proj = _matmul(x, w_in[:, :qw + kvw].astype(BF16), tn=512)
    ng = 3 * nh
    gates = _matmul(x, jnp.pad(w_in[:, qw + kvw:], ((0, 0), (0, LANES - ng))).astype(BF16), act="sigmoid")
    kv4w = 4 * NSA_KV * NSA_DH
    return proj[:, :qw], proj[:, qw:qw + kv4w], proj[:, qw + kv4w:], gates[:, :ng]


def _mm_t_kernel(w_ref, x_ref, o_ref):
    o_ref[0] = _dot_nt(w_ref[...], x_ref[...].astype(BF16))


def _matmul_t(x, wt, nb):
    m, k = x.shape
    n = wt.shape[0]
    seq = m // nb
    tl = _row_tile(seq, (1024, 512, 256, 128))
    tn = _row_tile(n, (512, 256, 128))
    return pl.pallas_call(
        _mm_t_kernel,
        grid=(nb, seq // tl, n // tn),
        in_specs=[pl.BlockSpec((tn, k), lambda b, l, j: (j, 0)),
                  pl.BlockSpec((tl, k), lambda b, l, j: (b * (seq // tl) + l, 0))],
        out_specs=pl.BlockSpec((1, tn, tl), lambda b, l, j: (b, j, l)),
        out_shape=jax.ShapeDtypeStruct((nb, n, seq), F32),
        compiler_params=_cparams(("parallel", "parallel", "arbitrary"), 48),
        name="matmul_t",
    )(wt, x)


def _nsa_prompt_mixer(x, nb, seq, prm):
    w_in, pos, w1, b1, w2, b2 = prm
    nh = (w_in.shape[1] - 6 * NSA_KV * NSA_DH) // (NSA_DH + 3)
    nrep = nh // NSA_KV
    pw = NSA_KV * NSA_DH
    qw = nh * NSA_DH
    qblk = 128
    nqb = seq // qblk
    rows = qblk * nrep
    proj = _matmul(x, w_in[:, :qw + 2 * pw].astype(BF16), tn=512)
    kvt = _matmul_t(x, w_in[:, qw:qw + 6 * pw].T.astype(BF16), nb)
    ng = 3 * nh
    gates = _matmul(x, jnp.pad(w_in[:, qw + 6 * pw:], ((0, 0), (0, LANES - ng))).astype(BF16), act="sigmoid")[:, :ng]
    npg = seq // PAGE
    cmp_out = _compress(proj.reshape(nb * npg, PAGE, qw + 2 * pw), jnp.arange(nb * npg, dtype=I32).reshape(nb, npg),
                        pos, w1, b1, w2, b2, col0=qw)
    q = proj[:, :qw] * (NSA_DH ** -0.5)
    qs = q.reshape(nb, nqb, qblk, NSA_KV, nrep, NSA_DH).transpose(0, 3, 1, 4, 2, 5).reshape(nb, NSA_KV, nqb, rows, NSA_DH)
    slf = jnp.repeat(_alibi_slopes(nh).reshape(NSA_KV, nrep), qblk, axis=1)
    s_hi = slf.astype(BF16).astype(F32)
    s_lo = (slf - s_hi).astype(BF16).astype(F32)
    extra = jnp.stack([POS_SPLIT * s_hi, s_hi, POS_SPLIT * s_lo, s_lo] + [jnp.zeros_like(slf)] * (Q_EXTRA - 4), axis=-1)
    qs = jnp.concatenate([qs, jnp.broadcast_to(extra[None, :, None], (nb, NSA_KV, nqb, rows, Q_EXTRA))], axis=-1)
    gs = gates.reshape(nb, nqb, qblk, 3, NSA_KV, nrep).transpose(0, 4, 1, 5, 2, 3).reshape(nb, NSA_KV, nqb, rows, 3)
    gs = jnp.pad(gs, ((0, 0),) * 4 + ((0, SUBLANES - 3),))
    sl = jnp.pad(slf[:, :, None], ((0, 0), (0, 0), (0, SUBLANES - 1)))
    nbs = -(-seq // SEL_BLK)
    nbs_r = -(-nbs // SUBLANES) * SUBLANES
    o = _nsa_prompt(qs, gs, sl, cmp_out[:, 0], cmp_out[:, 1], kvt.reshape(nb, 6 * NSA_KV, NSA_DH, seq),
                    _overlap(seq // CMP_STRIDE, nbs_r).T, nbs)
    o = o.reshape(nb, NSA_KV, nqb, nrep, qblk, NSA_DH).transpose(0, 2, 4, 1, 3, 5).reshape(nb * seq, nh * NSA_DH)
    kv6 = kvt.reshape(nb, 6, NSA_KV, NSA_DH, seq).transpose(0, 4, 1, 2, 3)
    return o, kv6[:, :, :4], kv6[:, :, 4:]


def _nsa_sample_mixer(x, nb, nq, cache, page_table, win_buf, prm):
    w_in, pos, w1, b1, w2, b2 = prm
    nh = (w_in.shape[1] - 6 * NSA_KV * NSA_DH) // (NSA_DH + 3)
    nrep = nh // NSA_KV
    pw = NSA_KV * NSA_DH
    npg = page_table.shape[1]
    past = npg * PAGE
    q, kv4, win, gates = _nsa_in(x, w_in, nh)
    n_pool = cache.shape[0]
    cache_t = cache.transpose(0, 2, 3, 4, 1).reshape(n_pool, 4, pw, PAGE)
    win_t = win_buf.transpose(0, 2, 3, 4, 1).reshape(nb, 2, pw, win_buf.shape[1])
    pages = cache[:, :, :2].reshape(n_pool, PAGE, 2 * pw)
    cmp_out = _compress(pages, page_table, pos, w1, b1, w2, b2)
    nsub = cmp_out.shape[3]
    kc = cmp_out[:, 0].transpose(0, 2, 1, 3).reshape(nb, nsub, pw)
    vc = cmp_out[:, 1].transpose(0, 2, 1, 3).reshape(nb, nsub, pw)
    rows = NSA_KV * nrep * nq
    q5 = q.reshape(nb, nq, NSA_KV, nrep, NSA_DH).transpose(0, 2, 3, 1, 4)
    qbd = jnp.einsum("bgrqd,gh->bgrqhd", q5, jnp.eye(NSA_KV, dtype=F32)).reshape(nb, rows, pw)
    gs = gates.reshape(nb, nq, 3, NSA_KV, nrep).transpose(0, 3, 4, 1, 2).reshape(nb, rows, 3)
    gs = jnp.pad(gs, ((0, 0), (0, 0), (0, SUBLANES - 3)))
    sl = jnp.repeat(_alibi_slopes(nh), nq)[:, None]
    sl = jnp.pad(sl, ((0, 0), (0, SUBLANES - 1)))
    nbs = -(-(past + nq) // SEL_BLK)
    nbs_pad = -(-nbs // LANES) * LANES
    o = _nsa_sample(qbd, gs, sl, kc, vc, cache_t, page_table, kv4.reshape(nb, nq, 4 * pw),
                    win_t, win.reshape(nb, nq, 2 * pw), _overlap(nsub, nbs_pad), nbs, nq, nrep)
    o = o.reshape(nb, NSA_KV, nrep, nq, NSA_DH).transpose(0, 3, 1, 2, 4).reshape(nb * nq, nh * NSA_DH)
    return o, kv4.reshape(nb, nq, 4, NSA_KV, NSA_DH), win.reshape(nb, nq, 2, NSA_KV, NSA_DH)


def kernel(x_prompt, x_sample, state_ssm, state_conv, cache_kv, state_win, page_table, ln_mix_g, ln_mix_b, ln_ffn_g, ln_ffn_b, ssd_w_in, ssd_conv_w, ssd_conv_b, ssd_dt_bias, ssd_a_log, ssd_d, ssd_norm_w, ssd_w_out, gm_w_in, gm_b_in, gm_ln_g, gm_ln_b, gm_w_s, gm_b_s, gm_w_out, nsa_w_in, nsa_cmp_pos, nsa_cmp_w1, nsa_cmp_b1, nsa_cmp_w2, nsa_cmp_b2, nsa_w_out, ffn_w_gate, ffn_w_up, ffn_w_down, moe_w_router, moe_b_router, moe_w_gate, moe_w_up, moe_w_down):
    depth = ln_mix_g.shape[0]
    alpha = (2 * depth) ** 0.25
    bp, seq, d = x_prompt.shape
    db, dseq, _ = x_sample.shape
    xp = x_prompt.reshape(bp * seq, d)
    xs = x_sample.reshape(db * dseq, d)
    outs = {k: [] for k in ("ssm_p", "conv_p", "ssm_s", "conv_s", "gmv_s", "kv_p", "win_p", "kv_s", "win_s")}
    row = lambda a, i: a[i][None, :]
    for l in range(depth):
        kind, j = l % N_MIXERS, l // N_MIXERS
        mg, mb = row(ln_mix_g, l), row(ln_mix_b, l)
        if kind == 0:
            prm = (ssd_w_in[j], ssd_conv_w[j], ssd_conv_b[j], ssd_dt_bias[j], ssd_a_log[j], ssd_d[j], ssd_norm_w[j])
            cdim = ssd_conv_w.shape[2]
            w_out = ssd_w_out[j].astype(BF16)
            yp, cp, hp = _ssd_mixer(xp, bp, SSD_CHUNK, jnp.zeros((bp, 8, cdim), F32),
                                    jnp.zeros((bp,) + state_ssm.shape[2:], F32), prm)
            ys, cs, hs = _ssd_mixer(xs, db, dseq, jnp.pad(state_conv[j], ((0, 0), (8 - (SSD_CONV - 1), 0), (0, 0))),
                                    state_ssm[j], prm)
            outs["ssm_p"].append(hp), outs["conv_p"].append(cp), outs["ssm_s"].append(hs), outs["conv_s"].append(cs)
            xp = _mm_res_ln(yp, xp, w_out, mg, mb, alpha)
            xs = _mm_res_ln(ys, xs, w_out, mg, mb, alpha)
        elif kind == 1:
            prm = (gm_w_in[j], gm_b_in[j], gm_ln_g[j], gm_ln_b[j], gm_w_s[j], gm_b_s[j], gm_w_out[j])
            xp, _ = _gmlp_mixer(xp, bp, seq, prm, mg, mb, alpha)
            xs, vs = _gmlp_mixer(xs, db, dseq, prm, mg, mb, alpha)
            outs["gmv_s"].append(vs.reshape(db, dseq, -1))
        else:
            prm = (nsa_w_in[j], nsa_cmp_pos[j], nsa_cmp_w1[j], nsa_cmp_b1[j], nsa_cmp_w2[j], nsa_cmp_b2[j])
            w_out = nsa_w_out[j].astype(BF16)
            op, kvp, wp = _nsa_prompt_mixer(xp, bp, seq, prm)
            os_, kvs, wsn = _nsa_sample_mixer(xs, db, dseq, cache_kv[j], page_table, state_win[j], prm)
            outs["kv_p"].append(kvp), outs["win_p"].append(wp[:, seq - min(WINDOW, seq):])
            outs["kv_s"].append(kvs), outs["win_s"].append(wsn)
            xp = _mm_res_ln(op, xp, w_out, mg, mb, alpha)
            xs = _mm_res_ln(os_, xs, w_out, mg, mb, alpha)
        f = l // 2
        fg, fb = row(ln_ffn_g, l), row(ln_ffn_b, l)
        if l % 2 == 0:
            wg, wu, wd = ffn_w_gate[f].astype(BF16), ffn_w_up[f].astype(BF16), ffn_w_down[f].astype(BF16)
            xp = _swiglu_res_ln(xp, wg, wu, wd, fg, fb, alpha)
            xs = _swiglu_res_ln(xs, wg, wu, wd, fg, fb, alpha)
        else:
            wg, wu, wd = moe_w_gate[f].astype(BF16), moe_w_up[f].astype(BF16), moe_w_down[f].astype(BF16)
            xp = _moe_res_ln(xp, moe_w_router[f], moe_b_router[f], wg, wu, wd, fg, fb, alpha)
            xs = _moe_res_ln(xs, moe_w_router[f], moe_b_router[f], wg, wu, wd, fg, fb, alpha)
    st = lambda k: jnp.stack(outs[k])
    return (xp.reshape(bp, seq, d), xs.reshape(db, dseq, d), st("ssm_p"), st("conv_p"), st("ssm_s"), st("conv_s"),
            st("gmv_s"), st("kv_p"), st("win_p"), st("kv_s"), st("win_s"))
```

```python
import functools

import jax
import jax.numpy as jnp
from jax import lax
from jax.experimental import pallas as pl
from jax.experimental.pallas import tpu as pltpu

F32 = jnp.float32
BF16 = jnp.bfloat16
I32 = jnp.int32

LN_EPS = 1e-5
NEG = -1e30
N_MIXERS = 3
SSD_P = 64
SSD_N = 128
SSD_G = 4
SSD_CONV = 4
SSD_CHUNK = 128
GM_CHUNK = 128
GM_G = 8
NSA_KV = 4
NSA_DH = 64
CMP_BLK = 32
CMP_STRIDE = 16
SEL_BLK = 64
N_SEL = 16
WINDOW = 512
FORCE = 1e4
TOP_K = 2

LANES = 128
SUBLANES = 8
MIB = 1 << 20


def _cparams(sem, vmem_mib):
    return pltpu.CompilerParams(dimension_semantics=sem, vmem_limit_bytes=vmem_mib * MIB)


def _row_tile(m, cands=(1024, 512, 256, 128, 64, 32, 16, 8)):
    for c in cands:
        if m % c == 0:
            return c
    raise ValueError(f"no row tile for {m}")


def _ln(v, g, b):
    mu = jnp.mean(v, -1, keepdims=True)
    d = v - mu
    var = jnp.mean(d * d, -1, keepdims=True)
    return d * lax.rsqrt(var + LN_EPS) * g + b


def _split(x, n):
    out = []
    r = x
    for k in range(n):
        h = r.astype(BF16)
        out.append(h)
        if k + 1 < n:
            r = r - h.astype(F32)
    return out


def _dot(a, b):
    return jnp.dot(a, b, preferred_element_type=F32)


def _dot_nt(a, b):
    return lax.dot_general(a, b, (((1,), (1,)), ((), ())), preferred_element_type=F32)


def _dot_tn(a, b):
    return lax.dot_general(a, b, (((0,), (0,)), ((), ())), preferred_element_type=F32)


def _silu(x):
    return x * jax.nn.sigmoid(x)


def _gelu_tanh(x):
    return 0.5 * x * (1.0 + jnp.tanh(0.7978845608028654 * (x + 0.044715 * (x * x * x))))


def _mm_kernel(x_ref, w_ref, b_ref, o_ref, *, act):
    acc = _dot(x_ref[...].astype(BF16), w_ref[...]) + b_ref[...]
    if act == "sigmoid":
        acc = jax.nn.sigmoid(acc)
    o_ref[...] = acc.astype(o_ref.dtype)


def _matmul(x, w, b=None, act=None, tn=None):
    m, k = x.shape
    n = w.shape[1]
    tm = _row_tile(m)
    tn = n if tn is None else tn
    if b is None:
        b = jnp.zeros((1, n), F32)
    return pl.pallas_call(
        functools.partial(_mm_kernel, act=act),
        grid=(m // tm, n // tn),
        in_specs=[pl.BlockSpec((tm, k), lambda i, j: (i, 0)),
                  pl.BlockSpec((k, tn), lambda i, j: (0, j)),
                  pl.BlockSpec((1, tn), lambda i, j: (0, j))],
        out_specs=pl.BlockSpec((tm, tn), lambda i, j: (i, j)),
        out_shape=jax.ShapeDtypeStruct((m, n), F32),
        compiler_params=_cparams(("parallel", "arbitrary"), 48),
        name="matmul",
    )(x, w, b)


def _mm_res_ln_kernel(y_ref, x_ref, w_ref, g_ref, b_ref, o_ref, *, alpha):
    f = _dot(y_ref[...].astype(BF16), w_ref[...])
    o_ref[...] = _ln(alpha * x_ref[...] + f, g_ref[...], b_ref[...])


def _mm_res_ln(y, x, w, g, b, alpha):
    m, k = y.shape
    d = x.shape[1]
    tm = _row_tile(m, (512, 256, 128))
    return pl.pallas_call(
        functools.partial(_mm_res_ln_kernel, alpha=alpha),
        grid=(m // tm,),
        in_specs=[pl.BlockSpec((tm, k), lambda i: (i, 0)),
                  pl.BlockSpec((tm, d), lambda i: (i, 0)),
                  pl.BlockSpec((k, d), lambda i: (0, 0)),
                  pl.BlockSpec((1, d), lambda i: (0, 0)),
                  pl.BlockSpec((1, d), lambda i: (0, 0))],
        out_specs=pl.BlockSpec((tm, d), lambda i: (i, 0)),
        out_shape=jax.ShapeDtypeStruct((m, d), F32),
        compiler_params=_cparams(("parallel",), 48),
        name="mm_res_ln",
    )(y, x, w, g, b)


def _swiglu_kernel(x_ref, wg_ref, wu_ref, wd_ref, g_ref, b_ref, o_ref, acc_ref, xb_ref, *, alpha):
    f = pl.program_id(1)

    @pl.when(f == 0)
    def _():
        acc_ref[...] = jnp.zeros_like(acc_ref)
        xb_ref[...] = x_ref[...].astype(BF16)

    xb = xb_ref[...]
    h = _silu(_dot(xb, wg_ref[...])) * _dot(xb, wu_ref[...])
    acc_ref[...] += _dot(h.astype(BF16), wd_ref[...])

    @pl.when(f == pl.num_programs(1) - 1)
    def _():
        o_ref[...] = _ln(alpha * x_ref[...] + acc_ref[...], g_ref[...], b_ref[...])


def _ff_tile(dff):
    for c in (512, 256, 128):
        if dff % c == 0:
            return c
    return dff


def _swiglu_res_ln(x, wg, wu, wd, g, b, alpha):
    m, d = x.shape
    dff = wg.shape[1]
    tm = _row_tile(m)
    tf = _ff_tile(dff)
    return pl.pallas_call(
        functools.partial(_swiglu_kernel, alpha=alpha),
        grid=(m // tm, dff // tf),
        in_specs=[pl.BlockSpec((tm, d), lambda i, f: (i, 0)),
                  pl.BlockSpec((d, tf), lambda i, f: (0, f)),
                  pl.BlockSpec((d, tf), lambda i, f: (0, f)),
                  pl.BlockSpec((tf, d), lambda i, f: (f, 0)),
                  pl.BlockSpec((1, d), lambda i, f: (0, 0)),
                  pl.BlockSpec((1, d), lambda i, f: (0, 0))],
        out_specs=pl.BlockSpec((tm, d), lambda i, f: (i, 0)),
        out_shape=jax.ShapeDtypeStruct((m, d), F32),
        scratch_shapes=[pltpu.VMEM((tm, d), F32), pltpu.VMEM((tm, d), BF16)],
        compiler_params=_cparams(("parallel", "arbitrary"), 48),
        name="swiglu_res_ln",
    )(x, wg, wu, wd, g, b)


def _ssd_kernel(xbc_ref, z_ref, dt_ref, conv0_ref, h0_ref, cw_ref, cb_ref, dtb_ref, alog_ref,
                dexp_ref, nw_ref, e_ref, ltri_ref, y_ref, h_ref, xp_scr, dtp_scr, yacc_scr,
                *, lv, n_heads):
    q = SSD_CHUNK
    di = n_heads * SSD_P
    gn = SSD_G * SSD_N
    hpg = n_heads // SSD_G
    gw = hpg * SSD_P
    ci = pl.program_id(1)

    @pl.when(ci == 0)
    def _():
        xp_scr[0:8, :] = conv0_ref[0]
        h_ref[0] = h0_ref[0]

    @pl.when(ci > 0)
    def _():
        xp_scr[0:8, :] = xp_scr[q:q + 8, :]

    if lv < q:
        xp_scr[8 + lv:, :] = jnp.zeros((q - lv, xp_scr.shape[1]), F32)
        dtp_scr[...] = jnp.zeros_like(dtp_scr)
        dtp_scr[0:lv, :] = dt_ref[...]
        dt_raw = dtp_scr[...]
    else:
        dt_raw = dt_ref[...]
    xp_scr[8:8 + lv, :] = xbc_ref[...]

    conv = cb_ref[...]
    for k in range(SSD_CONV):
        conv = conv + cw_ref[k:k + 1, :] * xp_scr[5 + k:5 + k + q, :]
    xc = _silu(conv)
    row = lax.broadcasted_iota(I32, (q, LANES), 0)
    lane = lax.broadcasted_iota(I32, (q, LANES), 1)
    if lv < q:
        xc = jnp.where(lax.broadcasted_iota(I32, xc.shape, 0) < lv, xc, 0.0)

    v = dt_raw + dtb_ref[...]
    dt = jnp.maximum(v, 0.0) + jnp.log1p(jnp.exp(-jnp.abs(v)))
    dt = jnp.where((row < lv) & (lane < n_heads), dt, 0.0)
    adt = dt * (-jnp.exp(alog_ref[...]))
    acum = sum(_dot(ltri_ref[...], t) for t in _split(adt, 3))
    acum_t = acum.T
    dt_t = dt.T
    ea = jnp.exp(acum)
    w = dt * jnp.exp(acum[q - 1:q, :] - acum)
    w_exp = sum(_dot(t, e_ref[...]) for t in _split(w, 2))
    ea_exp = sum(_dot(t, e_ref[...]) for t in _split(ea[:lv], 2))
    cdm = jnp.exp(jnp.broadcast_to(acum_t[:, q - 1:q], (LANES, LANES)))

    xs = xc[:, :di]
    xd = (xs * w_exp).astype(BF16)
    causal = (lax.broadcasted_iota(I32, (lv, q), 0) >= lax.broadcasted_iota(I32, (lv, q), 1))
    lo = lane < SSD_P

    for g in range(SSD_G):
        bg = xc[:, di + g * SSD_N:di + (g + 1) * SSD_N].astype(BF16)
        cg = xc[:lv, di + gn + g * SSD_N:di + gn + (g + 1) * SSD_N].astype(BF16)
        cbm = _dot_nt(cg, bg)
        hg = h_ref[0, g * gw:(g + 1) * gw, :].astype(BF16)
        yoff = _dot_nt(cg, hg)
        st = _dot_tn(xd[:, g * gw:(g + 1) * gw], bg)
        for pr in range(hpg // 2):
            col = g * gw + pr * LANES
            ms = []
            for hh in (2 * pr, 2 * pr + 1):
                h = g * hpg + hh
                seg = acum[:lv, h:h + 1] - acum_t[h:h + 1, :]
                dec = jnp.exp(jnp.where(causal, seg, -jnp.inf))
                ms.append((cbm * dec * dt_t[h:h + 1, :]).astype(BF16))
            xpair = xs[:, col:col + LANES]
            rhs = jnp.concatenate([jnp.where(lo, xpair, 0.0), jnp.where(lo, 0.0, xpair)], axis=0).astype(BF16)
            yd = _dot(jnp.concatenate(ms, axis=1), rhs)
            yacc_scr[:, col:col + LANES] = (yd + yoff[:, pr * LANES:(pr + 1) * LANES] * ea_exp[:, col:col + LANES]
                                            + dexp_ref[:, col:col + LANES] * xs[:lv, col:col + LANES])
        for hh in range(hpg):
            h = g * hpg + hh
            r0 = h * SSD_P
            h_ref[0, r0:r0 + SSD_P, :] = (h_ref[0, r0:r0 + SSD_P, :] * cdm[h:h + 1, :]
                                          + st[hh * SSD_P:(hh + 1) * SSD_P, :])

    y = yacc_scr[...] * _silu(z_ref[...])
    ng = di // SSD_G
    for g in range(SSD_G):
        yg = y[:, g * ng:(g + 1) * ng]
        ms_ = jnp.mean(yg * yg, -1, keepdims=True)
        y_ref[:, g * ng:(g + 1) * ng] = yg * lax.rsqrt(ms_ + LN_EPS) * nw_ref[:, g * ng:(g + 1) * ng]


def _ssd_scan(xbc, z, dt, conv0, h0, cw, cb, dtb, alog, dexp, nw, e, ltri, nb, lv):
    m, cdim = xbc.shape
    di = z.shape[1]
    n_heads = di // SSD_P
    nc = m // (nb * lv)
    return pl.pallas_call(
        functools.partial(_ssd_kernel, lv=lv, n_heads=n_heads),
        grid=(nb, nc),
        in_specs=[pl.BlockSpec((lv, cdim), lambda b, c: (b * nc + c, 0)),
                  pl.BlockSpec((lv, di), lambda b, c: (b * nc + c, 0)),
                  pl.BlockSpec((lv, LANES), lambda b, c: (b * nc + c, 0)),
                  pl.BlockSpec((1, 8, cdim), lambda b, c: (b, 0, 0)),
                  pl.BlockSpec((1, di, SSD_N), lambda b, c: (b, 0, 0)),
                  pl.BlockSpec((SSD_CONV, cdim), lambda b, c: (0, 0)),
                  pl.BlockSpec((1, cdim), lambda b, c: (0, 0)),
                  pl.BlockSpec((1, LANES), lambda b, c: (0, 0)),
                  pl.BlockSpec((1, LANES), lambda b, c: (0, 0)),
                  pl.BlockSpec((1, di), lambda b, c: (0, 0)),
                  pl.BlockSpec((1, di), lambda b, c: (0, 0)),
                  pl.BlockSpec((LANES, di), lambda b, c: (0, 0)),
                  pl.BlockSpec((SSD_CHUNK, SSD_CHUNK), lambda b, c: (0, 0))],
        out_specs=[pl.BlockSpec((lv, di), lambda b, c: (b * nc + c, 0)),
                   pl.BlockSpec((1, di, SSD_N), lambda b, c: (b, 0, 0))],
        out_shape=[jax.ShapeDtypeStruct((m, di), F32),
                   jax.ShapeDtypeStruct((nb, di, SSD_N), F32)],
        scratch_shapes=[pltpu.VMEM((SSD_CHUNK + 8, cdim), F32),
                        pltpu.VMEM((SSD_CHUNK, LANES), F32),
                        pltpu.VMEM((lv, di), F32)],
        compiler_params=_cparams(("parallel", "arbitrary"), 56),
        name="ssd_scan",
    )(xbc, z, dt, conv0, h0, cw, cb, dtb, alog, dexp, nw, e, ltri)


def _ssd_mixer(x, nb, lv, conv0, h0, prm):
    w_in, conv_w, conv_b, dt_bias, a_log, d_skip, norm_w = prm
    di = norm_w.shape[0]
    cdim = conv_w.shape[1]
    n_heads = di // SSD_P
    seq = x.shape[0] // nb
    w_z = w_in[:, :di].astype(BF16)
    w_x = w_in[:, di:di + cdim].astype(BF16)
    w_dt = jnp.pad(w_in[:, di + cdim:], ((0, 0), (0, LANES - n_heads))).astype(BF16)
    z = _matmul(x, w_z, tn=1024)
    xbc = _matmul(x, w_x, tn=1024)
    dt = _matmul(x, w_dt)
    pad1 = lambda a: jnp.pad(a.astype(F32), (0, LANES - n_heads))[None, :]
    e = (jnp.arange(di)[None, :] // SSD_P == jnp.arange(LANES)[:, None]).astype(BF16)
    ltri = (jnp.arange(SSD_CHUNK)[:, None] >= jnp.arange(SSD_CHUNK)[None, :]).astype(BF16)
    y, h_new = _ssd_scan(xbc, z, dt, conv0, h0.reshape(nb, di, SSD_N), conv_w, conv_b[None, :],
                         pad1(dt_bias), pad1(a_log), jnp.repeat(d_skip, SSD_P)[None, :], norm_w[None, :],
                         e, ltri, nb, lv)
    conv_new = xbc.reshape(nb, seq, cdim)[:, seq - (SSD_CONV - 1):]
    return y, conv_new, h_new.reshape(nb, n_heads, SSD_P, SSD_N)


def _gm_in_kernel(x_ref, w_ref, b_ref, g_ref, bb_ref, u_ref, v_ref):
    h = _gelu_tanh(_dot(x_ref[...].astype(BF16), w_ref[...]) + b_ref[...])
    d = u_ref.shape[1]
    u_ref[...] = h[:, :d]
    v_ref[...] = _ln(h[:, d:], g_ref[...], bb_ref[...])


def _gm_in(x, w, b, g, bb):
    m, k = x.shape
    d = w.shape[1] // 2
    tm = _row_tile(m, (512, 256))
    return pl.pallas_call(
        _gm_in_kernel,
        grid=(m // tm,),
        in_specs=[pl.BlockSpec((tm, k), lambda i: (i, 0)),
                  pl.BlockSpec((k, 2 * d), lambda i: (0, 0)),
                  pl.BlockSpec((1, 2 * d), lambda i: (0, 0)),
                  pl.BlockSpec((1, d), lambda i: (0, 0)),
                  pl.BlockSpec((1, d), lambda i: (0, 0))],
        out_specs=[pl.BlockSpec((tm, d), lambda i: (i, 0)), pl.BlockSpec((tm, d), lambda i: (i, 0))],
        out_shape=[jax.ShapeDtypeStruct((m, d), F32), jax.ShapeDtypeStruct((m, d), F32)],
        compiler_params=_cparams(("parallel",), 48),
        name="gm_in",
    )(x, w, b, g, bb)


def _gm_out_kernel(u_ref, v_ref, x_ref, s_ref, sb_ref, w_ref, g_ref, b_ref, o_ref, gated_scr, *, alpha, r):
    tb = u_ref.shape[0]
    gd = u_ref.shape[1] // GM_G
    for s in range(tb // r):
        rows = slice(s * r, (s + 1) * r)
        for g in range(GM_G):
            cols = slice(g * gd, (g + 1) * gd)
            mixed = _dot(s_ref[g], v_ref[rows, cols].astype(BF16)) + sb_ref[rows, cols]
            gated_scr[rows, cols] = (u_ref[rows, cols] * mixed).astype(BF16)
    f = _dot(gated_scr[...], w_ref[...])
    o_ref[...] = _ln(alpha * x_ref[...] + f, g_ref[...], b_ref[...])


def _gm_out(u, v, x, smat, sbias, w, g, b, alpha, tb):
    m, d = u.shape
    r = smat.shape[1]
    return pl.pallas_call(
        functools.partial(_gm_out_kernel, alpha=alpha, r=r),
        grid=(m // tb,),
        in_specs=[pl.BlockSpec((tb, d), lambda i: (i, 0)),
                  pl.BlockSpec((tb, d), lambda i: (i, 0)),
                  pl.BlockSpec((tb, d), lambda i: (i, 0)),
                  pl.BlockSpec((GM_G, r, r), lambda i: (0, 0, 0)),
                  pl.BlockSpec((tb, d), lambda i: (0, 0)),
                  pl.BlockSpec((d, d), lambda i: (0, 0)),
                  pl.BlockSpec((1, d), lambda i: (0, 0)),
                  pl.BlockSpec((1, d), lambda i: (0, 0))],
        out_specs=pl.BlockSpec((tb, d), lambda i: (i, 0)),
        out_shape=jax.ShapeDtypeStruct((m, d), F32),
        scratch_shapes=[pltpu.VMEM((tb, d), BF16)],
        compiler_params=_cparams(("parallel",), 48),
        name="gm_out",
    )(u, v, x, smat, sbias, w, g, b)


def _gmlp_mixer(x, nb, seq, prm, ln_g, ln_b, alpha):
    w_in, b_in, g_in, bb_in, w_s, b_s, w_out = prm
    d = w_out.shape[0]
    gd = d // GM_G
    u, v = _gm_in(x, w_in.astype(BF16), b_in[None, :], g_in[None, :], bb_in[None, :])
    ws = jnp.tril(w_s)
    if seq % GM_CHUNK == 0:
        tb = 512
        smat = ws.astype(BF16)
        bias_rows = jnp.repeat(b_s.T, gd, axis=1)
        sbias = jnp.tile(bias_rows, (tb // GM_CHUNK, 1))
    else:
        tb = nb * seq
        smat = jnp.einsum("ab,gts->gatbs", jnp.eye(nb, dtype=F32), ws[:, :seq, :seq]).reshape(GM_G, tb, tb).astype(BF16)
        sbias = jnp.tile(jnp.repeat(b_s.T[:seq], gd, axis=1), (nb, 1))
    out = _gm_out(u, v, x, smat, sbias, w_out.astype(BF16), ln_g, ln_b, alpha, tb)
    return out, v


def _router_kernel(x_ref, wh_ref, wl_ref, b_ref, eid_ref, gate_ref):
    x = x_ref[...]
    xh, xl = _split(x, 2)
    logits = _dot(xh, wh_ref[...]) + _dot(xl, wh_ref[...]) + _dot(xh, wl_ref[...]) + b_ref[...]
    lane = lax.broadcasted_iota(I32, logits.shape, 1)
    m1 = jnp.max(logits, -1, keepdims=True)
    i1 = jnp.min(jnp.where(logits == m1, lane, LANES), -1, keepdims=True)
    rest = jnp.where(lane == i1, NEG * 2, logits)
    m2 = jnp.max(rest, -1, keepdims=True)
    i2 = jnp.min(jnp.where(rest == m2, lane, LANES), -1, keepdims=True)
    e = jnp.exp(m2 - m1)
    g1 = 1.0 / (1.0 + e)
    eid_ref[...] = jnp.where(lane == 0, i1, jnp.where(lane == 1, i2, 0))
    gate_ref[...] = jnp.where(lane == 0, g1, jnp.where(lane == 1, e * g1, 0.0))


def _router(x, w_router, b_router):
    m, d = x.shape
    ne = w_router.shape[1]
    w = jnp.pad(w_router, ((0, 0), (0, LANES - ne)))
    wh = w.astype(BF16)
    wl = (w - wh.astype(F32)).astype(BF16)
    b = jnp.pad(b_router.astype(F32), (0, LANES - ne), constant_values=NEG)[None, :]
    tm = _row_tile(m, (512, 256))
    return pl.pallas_call(
        _router_kernel,
        grid=(m // tm,),
        in_specs=[pl.BlockSpec((tm, d), lambda i: (i, 0)),
                  pl.BlockSpec((d, LANES), lambda i: (0, 0)),
                  pl.BlockSpec((d, LANES), lambda i: (0, 0)),
                  pl.BlockSpec((1, LANES), lambda i: (0, 0))],
        out_specs=[pl.BlockSpec((tm, LANES), lambda i: (i, 0)), pl.BlockSpec((tm, LANES), lambda i: (i, 0))],
        out_shape=[jax.ShapeDtypeStruct((m, LANES), I32), jax.ShapeDtypeStruct((m, LANES), F32)],
        compiler_params=_cparams(("parallel",), 32),
        name="router",
    )(x, wh, wl, b)


def _row_copy(src_hbm, dst, s_row, d_row, sem):
    return pltpu.make_async_copy(src_hbm.at[pl.ds(s_row, 1), :], dst.at[pl.ds(d_row, 1), :], sem)


def _row_gather(src_hbm, dst, idx_ref, n, per_row, sem, start):
    if not start:
        for k in range(per_row):
            pltpu.make_async_copy(src_hbm.at[pl.ds(0, n), :], dst.at[k], sem).wait()
        return

    def body(j, c):
        for k in range(per_row):
            _row_copy(src_hbm, dst.at[k], idx_ref[0, 0, per_row * j + k], j, sem).start()
        return c

    lax.fori_loop(0, n, body, 0, unroll=8)


def _moe_ffn_kernel(be_ref, nu_ref, idx_ref, idxn_ref, x_hbm, wg_ref, wu_ref, wd_ref, o_ref,
                    acc_ref, xg_ref, xb_ref, sem, *, tb):
    i = pl.program_id(0)
    f = pl.program_id(1)
    last = pl.num_programs(1) - 1
    n_used = nu_ref[0]
    used = i < n_used
    slot = i % 2

    @pl.when((f == 0) & (i == 0))
    def _():
        _row_gather(x_hbm, xg_ref.at[0], idx_ref, tb, 1, sem.at[0], True)

    @pl.when((f == 0) & (i + 1 < n_used))
    def _():
        _row_gather(x_hbm, xg_ref.at[1 - slot], idxn_ref, tb, 1, sem.at[1 - slot], True)

    @pl.when(used & (f == 0))
    def _():
        _row_gather(x_hbm, xg_ref.at[slot], idx_ref, tb, 1, sem.at[slot], False)
        acc_ref[...] = jnp.zeros_like(acc_ref)
        xb_ref[...] = xg_ref[slot, 0].astype(BF16)

    @pl.when(used)
    def _():
        xb = xb_ref[...]
        h = _silu(_dot(xb, wg_ref[0])) * _dot(xb, wu_ref[0])
        acc_ref[...] += _dot(h.astype(BF16), wd_ref[0])

    @pl.when(used & (f == last))
    def _():
        o_ref[...] = acc_ref[...]

    @pl.when(jnp.logical_not(used) & (f == last))
    def _():
        o_ref[...] = jnp.zeros_like(o_ref)


def _moe_ffn(x, buf_tok, blk_e, n_used, wg, wu, wd, tb):
    d = x.shape[1]
    nblk = buf_tok.shape[0] // tb
    dff = wg.shape[2]
    tf = _ff_tile(dff)
    nf = dff // tf

    def fe(i, f, be, nu):
        return jnp.where(i < nu[0], f, nf - 1)

    grid_spec = pltpu.PrefetchScalarGridSpec(
        num_scalar_prefetch=2,
        grid=(nblk, nf),
        in_specs=[pl.BlockSpec((1, 1, tb), lambda i, f, be, nu: (i, 0, 0), memory_space=pltpu.SMEM),
                  pl.BlockSpec((1, 1, tb), lambda i, f, be, nu: (jnp.minimum(i + 1, nblk - 1), 0, 0),
                               memory_space=pltpu.SMEM),
                  pl.BlockSpec(memory_space=pl.ANY),
                  pl.BlockSpec((1, d, tf), lambda i, f, be, nu: (be[i], 0, fe(i, f, be, nu))),
                  pl.BlockSpec((1, d, tf), lambda i, f, be, nu: (be[i], 0, fe(i, f, be, nu))),
                  pl.BlockSpec((1, tf, d), lambda i, f, be, nu: (be[i], fe(i, f, be, nu), 0))],
        out_specs=pl.BlockSpec((tb, d), lambda i, f, be, nu: (i, 0)),
        scratch_shapes=[pltpu.VMEM((tb, d), F32), pltpu.VMEM((2, 1, tb, d), F32), pltpu.VMEM((tb, d), BF16),
                        pltpu.SemaphoreType.DMA((2,))])
    idx = buf_tok.reshape(nblk, 1, tb)
    return pl.pallas_call(
        functools.partial(_moe_ffn_kernel, tb=tb),
        grid_spec=grid_spec,
        out_shape=jax.ShapeDtypeStruct((nblk * tb, d), F32),
        compiler_params=_cparams(("arbitrary", "arbitrary"), 48),
        name="moe_ffn",
    )(blk_e, n_used, idx, idx, x, wg, wu, wd)


def _combine_kernel(idx_ref, idxn_ref, yb_hbm, x_ref, gate_ref, g_ref, b_ref, o_ref, ybuf, sem, *, alpha, tm):
    i = pl.program_id(0)
    slot = i % 2

    @pl.when(i == 0)
    def _():
        _row_gather(yb_hbm, ybuf.at[0], idx_ref, tm, TOP_K, sem.at[0], True)

    @pl.when(i + 1 < pl.num_programs(0))
    def _():
        _row_gather(yb_hbm, ybuf.at[1 - slot], idxn_ref, tm, TOP_K, sem.at[1 - slot], True)

    _row_gather(yb_hbm, ybuf.at[slot], idx_ref, tm, TOP_K, sem.at[slot], False)
    gate = gate_ref[...]
    y = gate[:, 0:1] * ybuf[slot, 0] + gate[:, 1:2] * ybuf[slot, 1]
    o_ref[...] = _ln(alpha * x_ref[...] + y, g_ref[...], b_ref[...])


def _moe_combine(yb, dest, x, gate, g, b, alpha):
    m, d = x.shape
    tm = _row_tile(m, (256, 128))
    nblk = m // tm
    idx = dest.reshape(nblk, 1, TOP_K * tm)
    return pl.pallas_call(
        functools.partial(_combine_kernel, alpha=alpha, tm=tm),
        grid=(nblk,),
        in_specs=[pl.BlockSpec((1, 1, TOP_K * tm), lambda i: (i, 0, 0), memory_space=pltpu.SMEM),
                  pl.BlockSpec((1, 1, TOP_K * tm), lambda i: (jnp.minimum(i + 1, nblk - 1), 0, 0),
                               memory_space=pltpu.SMEM),
                  pl.BlockSpec(memory_space=pl.ANY),
                  pl.BlockSpec((tm, d), lambda i: (i, 0)),
                  pl.BlockSpec((tm, LANES), lambda i: (i, 0)),
                  pl.BlockSpec((1, d), lambda i: (0, 0)),
                  pl.BlockSpec((1, d), lambda i: (0, 0))],
        out_specs=pl.BlockSpec((tm, d), lambda i: (i, 0)),
        out_shape=jax.ShapeDtypeStruct((m, d), F32),
        scratch_shapes=[pltpu.VMEM((2, TOP_K, tm, d), F32), pltpu.SemaphoreType.DMA((2,))],
        compiler_params=_cparams(("arbitrary",), 32),
        name="moe_combine",
    )(idx, idx, yb, x, gate, g, b)


def _moe_res_ln(x, w_router, b_router, wg, wu, wd, g, b, alpha):
    m, d = x.shape
    ne = w_router.shape[1]
    eid, gate = _router(x, w_router, b_router)
    tk = m * TOP_K
    tb = 1024 if tk >= 16384 else 256
    ef = eid[:, :TOP_K].reshape(-1)
    onehot = (ef[:, None] == jnp.arange(ne, dtype=I32)[None, :]).astype(I32)
    csum = jnp.cumsum(onehot, axis=0)
    rank = jnp.sum(csum * onehot, axis=1) - 1
    counts = csum[-1]
    padded = (counts + tb - 1) // tb * tb
    pend = jnp.cumsum(padded)
    dest = (pend - padded)[ef] + rank
    nblk = -(-tk // tb) + ne
    buf_tok = jnp.zeros((nblk * tb,), I32).at[dest].set(jnp.arange(tk, dtype=I32) // TOP_K)
    blk_start = jnp.arange(nblk, dtype=I32) * tb
    blk_e = jnp.minimum(jnp.sum((pend[None, :] <= blk_start[:, None]).astype(I32), axis=1), ne - 1).astype(I32)
    n_used = (pend[-1:] // tb).astype(I32)
    yb = _moe_ffn(x, buf_tok, blk_e, n_used, wg, wu, wd, tb)
    return _moe_combine(yb, dest.astype(I32), x, gate, g, b, alpha)


PAGE = 128
ATT_KT = 512


def _compress_kernel(pt_ref, pages_hbm, w1_ref, pos_ref, b1_ref, w2_ref, b2_ref, o_ref, scr, acc_scr, cv_scr, sem,
                     *stage, npg, col0):
    b = pl.program_id(0)
    pw = NSA_KV * NSA_DH
    nsub = npg * PAGE // CMP_STRIDE
    hid = b1_ref.shape[2]

    nslab = 2 * pw // LANES
    gps = LANES // NSA_DH

    def copies(p, pg):
        if stage:
            return [pltpu.make_async_copy(pages_hbm.at[pg, c], stage[0].at[c, p], sem) for c in range(2)]
        row = pl.multiple_of(p * PAGE, PAGE)
        return [pltpu.make_async_copy(pages_hbm.at[pg, :, pl.ds(col0 + sl * LANES, LANES)],
                                      scr.at[sl, pl.ds(row, PAGE), :], sem) for sl in range(nslab)]

    def start(p, c_):
        for cp in copies(p, pt_ref[b, p]):
            cp.start()
        return c_

    lax.fori_loop(0, npg, start, 0)

    @pl.when(b == 0)
    def _():
        for c in range(2):
            cv = jnp.zeros((SUBLANES, hid), F32) + b1_ref[c]
            for o in range(CMP_STRIDE):
                cv = cv + _dot(pos_ref[c, o].astype(BF16), w1_ref[c, o])[:, :hid]
                cv = cv + _dot(pos_ref[c, CMP_STRIDE + o].astype(BF16), w1_ref[c, o])[:, hid:]
            cv_scr[c] = cv

    def wait(p, c_):
        for cp in copies(p, 0):
            cp.wait()
        return c_

    lax.fori_loop(0, npg, wait, 0)

    if stage:
        def to_token_major(p, c_):
            row = pl.multiple_of(p * PAGE, PAGE)
            for sl in range(nslab):
                c, half = divmod(sl, nslab // 2)
                scr[sl, pl.ds(row, PAGE), :] = stage[0][c, p, half * LANES:(half + 1) * LANES, :].T
            return c_

        lax.fori_loop(0, npg, to_token_major, 0)

    for c in range(2):
        acc_scr[...] = jnp.zeros_like(acc_scr)
        for o in range(CMP_STRIDE):
            for sl in range(nslab // 2):
                xo = scr[c * (nslab // 2) + sl, pl.ds(o, nsub, stride=CMP_STRIDE), :].astype(BF16)
                for gg in range(gps):
                    acc_scr[sl * gps + gg] += _dot(xo[:, gg * NSA_DH:(gg + 1) * NSA_DH], w1_ref[c, o])
        for g in range(NSA_KV):
            acc = acc_scr[g]
            pre = acc[:, :hid] + pltpu.roll(acc[:, hid:], nsub - 1, 0) + cv_scr[c, 0:1, :]
            o_ref[0, c, g] = _dot(_gelu_tanh(pre).astype(BF16), w2_ref[c]) + b2_ref[c]


def _compress(pages, page_table, pos, w1, b1, w2, b2, col0=0, token_minor=False):
    nb, npg = page_table.shape
    nsub = npg * PAGE // CMP_STRIDE
    hid = w1.shape[-1]
    pw = NSA_KV * NSA_DH
    w1cat = jnp.concatenate([w1[:, :CMP_STRIDE], w1[:, CMP_STRIDE:]], axis=-1).astype(BF16)
    posb = jnp.broadcast_to(pos[:, :, None, :], (2, CMP_BLK, SUBLANES, NSA_DH))
    grid_spec = pltpu.PrefetchScalarGridSpec(
        num_scalar_prefetch=1,
        grid=(nb,),
        in_specs=[pl.BlockSpec(memory_space=pl.ANY),
                  pl.BlockSpec((2, CMP_STRIDE, NSA_DH, 2 * hid), lambda b, pt: (0, 0, 0, 0)),
                  pl.BlockSpec((2, CMP_BLK, SUBLANES, NSA_DH), lambda b, pt: (0, 0, 0, 0)),
                  pl.BlockSpec((2, 1, hid), lambda b, pt: (0, 0, 0)),
                  pl.BlockSpec((2, hid, NSA_DH), lambda b, pt: (0, 0, 0)),
                  pl.BlockSpec((2, 1, NSA_DH), lambda b, pt: (0, 0, 0))],
        out_specs=pl.BlockSpec((1, 2, NSA_KV, nsub, NSA_DH), lambda b, pt: (b, 0, 0, 0, 0)),
        scratch_shapes=[pltpu.VMEM((2 * pw // LANES, npg * PAGE, LANES), F32),
                        pltpu.VMEM((NSA_KV, nsub, 2 * hid), F32),
                        pltpu.VMEM((2, SUBLANES, hid), F32),
                        pltpu.SemaphoreType.DMA(())]
                       + ([pltpu.VMEM((2, npg, pw, PAGE), F32)] if token_minor else []))
    return pl.pallas_call(
        functools.partial(_compress_kernel, npg=npg, col0=col0),
        grid_spec=grid_spec,
        out_shape=jax.ShapeDtypeStruct((nb, 2, NSA_KV, nsub, NSA_DH), F32),
        compiler_params=_cparams(("arbitrary",), 56),
        name="nsa_compress",
    )(page_table, pages, w1cat, posb, b1[:, None, :], w2.astype(BF16), b2[:, None, :])


def _masked_softmax(s, valid):
    s = jnp.where(valid, s, NEG)
    p = jnp.exp(s - jnp.max(s, -1, keepdims=True))
    return p / jnp.sum(p, -1, keepdims=True)


def _select_blocks(imp, ovl, tq, nbs):
    score = sum(_dot(t, ovl) for t in _split(imp, 3))
    jb = lax.broadcasted_iota(I32, score.shape, 1)
    ok = jb * SEL_BLK <= tq
    cur = tq // SEL_BLK
    forced = (jb == 0) | (jb == cur) | (jb == cur - 1)
    score = jnp.where(ok, score + jnp.where(forced, FORCE, 0.0), NEG)
    rank = jnp.zeros(score.shape, F32)
    for j2 in range(nbs):
        cj = score[:, j2:j2 + 1]
        beats = (cj > score) | ((cj == score) & (j2 < jb))
        rank = rank + jnp.where(beats, 1.0, 0.0)
    return jnp.where(ok & (rank < N_SEL), 1.0, 0.0)


def _expand_sel(sel, first_blk, nkeys):
    jj = lax.broadcasted_iota(I32, (sel.shape[1], nkeys), 0)
    kk = lax.broadcasted_iota(I32, (sel.shape[1], nkeys), 1)
    expand = jnp.where(kk // SEL_BLK + first_blk == jj, 1.0, 0.0).astype(BF16)
    return _dot(sel.astype(BF16), expand)


POS_SPLIT = 64
Q_EXTRA = 16


PROMPT_KT = 512


def _nsa_prompt_kernel(q_ref, gate_ref, slope_ref, kc_ref, vc_ref, ks_ref, vs_ref, kw_ref, vw_ref, ovl_ref, exp_ref,
                       o_ref, *, nbs, qblk, nrep):
    qb = pl.program_id(2)
    cols = qblk * nrep
    q = q_ref[0, 0, 0]
    q16 = q.astype(BF16)
    slope = slope_ref[0][0:1, :]
    tq = qb * qblk + lax.broadcasted_iota(I32, (1, cols), 1) % qblk
    tq_row = tq[:, 0:qblk]

    kc = kc_ref[0, 0]
    qh, ql = _split(q[:NSA_DH, :], 2)
    kh, kl = _split(kc, 2)
    s = _dot(kh, qh) + _dot(kh, ql) + _dot(kl, qh)
    e = lax.broadcasted_iota(I32, (kc.shape[0], 1), 0) * CMP_STRIDE + (CMP_BLK - 1)
    valid = e <= tq
    s = jnp.where(valid, s + slope * e.astype(F32), NEG)
    p = jnp.exp(s - jnp.max(s, 0, keepdims=True))
    p = jnp.where(valid, p / jnp.sum(p, 0, keepdims=True), 0.0)
    o_cmp = _dot(vc_ref[0, 0].astype(BF16), p.astype(BF16))
    imp = p[:, 0:qblk]
    for r in range(1, nrep):
        imp = imp + p[:, r * qblk:(r + 1) * qblk]

    score = sum(_dot(ovl_ref[...], t) for t in _split(imp, 3))
    jb = lax.broadcasted_iota(I32, score.shape, 0)
    ok = jb * SEL_BLK <= tq_row
    cur = tq_row // SEL_BLK
    forced = (jb == 0) | (jb == cur) | (jb == cur - 1)
    score = jnp.where(ok, score + jnp.where(forced, FORCE, 0.0), NEG)
    rank = jnp.zeros(score.shape, F32)
    for j2 in range(nbs):
        cj = score[j2:j2 + 1, :]
        beats = (cj > score) | ((cj == score) & (j2 < jb))
        rank = rank + jnp.where(beats, 1.0, 0.0)
    selneg = jnp.where(ok & (rank < N_SEL), 0.0, NEG).astype(BF16)

    def attend(k_ref, v_ref, kt_lo, kt_hi, bias_fn):
        def body(kt, carry):
            m, l, acc = carry
            off = pl.multiple_of(kt * PROMPT_KT, PROMPT_KT)
            kpos = off + lax.broadcasted_iota(I32, (PROMPT_KT, 1), 0)
            s_ = _dot(k_ref[0, 0, pl.ds(off, PROMPT_KT), :], q16)
            s_ = s_ + jnp.concatenate([bias_fn(off, kpos)] * nrep, axis=1)
            m_new = jnp.maximum(m, jnp.max(s_, 0, keepdims=True))
            a = jnp.exp(m - m_new)
            p_ = jnp.exp(s_ - m_new)
            v = v_ref[0, 0, :, pl.ds(off, PROMPT_KT)].astype(BF16)
            return m_new, l * a + jnp.sum(p_, 0, keepdims=True), acc * a + _dot(v, p_.astype(BF16))

        init = (jnp.full((1, cols), -jnp.inf, F32), jnp.zeros((1, cols), F32), jnp.zeros((NSA_DH, cols), F32))
        m, l, acc = lax.fori_loop(kt_lo, kt_hi, body, init)
        return acc / l

    def sel_bias(off, kpos):
        return jnp.where(kpos <= tq_row, _dot(exp_ref[pl.ds(off, PROMPT_KT), :], selneg), NEG)

    def win_bias(off, kpos):
        return jnp.where((kpos <= tq_row) & (kpos > tq_row - WINDOW), 0.0, NEG)

    kt_hi = (qb * qblk + qblk - 1) // PROMPT_KT + 1
    gate = gate_ref[0, 0, 0]
    o = gate[0:1, :] * o_cmp
    o = o + gate[1:2, :] * attend(ks_ref, vs_ref, 0, kt_hi, sel_bias)
    o = o + gate[2:3, :] * attend(kw_ref, vw_ref, jnp.maximum(qb * qblk - (WINDOW - 1), 0) // PROMPT_KT, kt_hi, win_bias)
    o_ref[0, 0, 0] = o


def _nsa_prompt(qt, gates, slopes, kc, vct, ksa, kwa, kvt, ovl_t, nbs):
    nb, ng, nqb, qw, cols = qt.shape
    dh = qw - Q_EXTRA
    seq = kvt.shape[3]
    nbc = kc.shape[2]
    qblk = seq // nqb
    nrep = cols // qblk
    qspec = lambda r: pl.BlockSpec((1, 1, 1, r, cols), lambda b, g, i: (b, g, i, 0, 0))
    per_g = lambda r, c: pl.BlockSpec((1, 1, r, c), lambda b, g, i: (b, g, 0, 0))
    tspec = lambda c: pl.BlockSpec((1, 1, dh, seq), lambda b, g, i: (b, c * ng + g, 0, 0))
    expand = (jnp.arange(seq)[:, None] // SEL_BLK == jnp.arange(ovl_t.shape[0])[None, :]).astype(BF16)
    return pl.pallas_call(
        functools.partial(_nsa_prompt_kernel, nbs=nbs, qblk=qblk, nrep=nrep),
        grid=(nb, ng, nqb),
        in_specs=[qspec(qw), qspec(SUBLANES),
                  pl.BlockSpec((1, SUBLANES, cols), lambda b, g, i: (g, 0, 0)),
                  per_g(nbc, dh), per_g(dh, nbc), per_g(seq, qw), tspec(3), per_g(seq, qw), tspec(5),
                  pl.BlockSpec(ovl_t.shape, lambda b, g, i: (0, 0)),
                  pl.BlockSpec(expand.shape, lambda b, g, i: (0, 0))],
        out_specs=qspec(dh),
        out_shape=jax.ShapeDtypeStruct((nb, ng, nqb, dh, cols), F32),
        compiler_params=_cparams(("parallel", "parallel", "arbitrary"), 48),
        name="nsa_prompt",
    )(qt, gates, slopes, kc, vct, ksa, kvt, kwa, kvt, ovl_t, expand)


def _nsa_sample_kernel(pt_ref, q_ref, gate_ref, slope_ref, kc_ref, vc_ref,
                       ks0, ks1, ks2, ks3, vs0, vs1, vs2, vs3, ksn_ref, vsn_ref, kwp_ref, vwp_ref, kwn_ref, vwn_ref,
                       ovl_ref, o_ref, sel_scr, m_scr, l_scr, acc_scr, part_scr, *, nbs, past, nq, nrep):
    t = pl.program_id(1)
    rows = q_ref.shape[1]
    pw = q_ref.shape[2]
    grows = rows // NSA_KV
    scale = NSA_DH ** -0.5
    q = q_ref[0]
    q16 = q.astype(BF16)
    slope = slope_ref[:, 0:1]
    tq = past + lax.broadcasted_iota(I32, (rows, 1), 0) % nq
    tqf = tq.astype(F32)
    gate = gate_ref[0]

    def logits(qk, kpos):
        return qk * scale - slope * (tqf - kpos.astype(F32))

    @pl.when(t == 0)
    def _():
        kc = kc_ref[0]
        qh, ql = _split(q, 2)
        kh, kl = _split(kc, 2)
        s = (_dot_nt(qh, kh) + _dot_nt(ql, kh) + _dot_nt(qh, kl)) * scale
        e = lax.broadcasted_iota(I32, (1, kc.shape[0]), 1) * CMP_STRIDE + (CMP_BLK - 1)
        valid = e <= tq
        p = jnp.where(valid, _masked_softmax(s - slope * (tqf - e.astype(F32)), valid), 0.0)
        o_cmp = _dot(p.astype(BF16), vc_ref[0].astype(BF16))
        imps = []
        for g in range(NSA_KV):
            a = p[g * grows:g * grows + nq]
            for r in range(1, nrep):
                a = a + p[g * grows + r * nq:g * grows + (r + 1) * nq]
            imps.append(a)
        imp = jnp.concatenate(imps, axis=0)
        tq_s = past + lax.broadcasted_iota(I32, (NSA_KV * nq, 1), 0) % nq
        sel_s = _select_blocks(imp, ovl_ref[...], tq_s, nbs)
        sel = jnp.concatenate([sel_s[g * nq:(g + 1) * nq] for g in range(NSA_KV) for _ in range(nrep)], axis=0)
        sel_scr[...] = sel

        wb = kwp_ref.shape[3]
        npad = LANES - nq
        zpad = jnp.zeros((npad, pw), F32)
        kwn = jnp.concatenate([kwn_ref[0], zpad], axis=0).astype(BF16)
        vwn = jnp.concatenate([vwn_ref[0], zpad], axis=0).astype(BF16)
        qk = jnp.concatenate([_dot(q16, kwp_ref[0, 0].astype(BF16)), _dot_nt(q16, kwn)], axis=1)
        idx = lax.broadcasted_iota(I32, (1, wb + LANES), 1)
        kpos = past - wb + idx
        valid = (idx < wb + nq) & (kpos <= tq) & (kpos > tq - WINDOW)
        pw_ = _masked_softmax(logits(qk, kpos), valid).astype(BF16)
        o_win = _dot_nt(pw_[:, :wb], vwp_ref[0, 0].astype(BF16)) + _dot(pw_[:, wb:], vwn)
        part_scr[...] = gate[:, 0:1] * o_cmp + gate[:, 2:3] * o_win

        kn = jnp.concatenate([ksn_ref[0], zpad], axis=0).astype(BF16)
        vn = jnp.concatenate([vsn_ref[0], zpad], axis=0).astype(BF16)
        idx = lax.broadcasted_iota(I32, (1, LANES), 1)
        kpos = past + idx
        blk = past // SEL_BLK
        valid = (idx < nq) & (kpos <= tq) & (sel[:, blk:blk + 1] > 0.5)
        s = jnp.where(valid, logits(_dot_nt(q16, kn), kpos), NEG)
        m = jnp.max(s, -1, keepdims=True)
        p = jnp.exp(s - m)
        m_scr[...] = m
        l_scr[...] = jnp.sum(p, -1, keepdims=True)
        acc_scr[...] = _dot(p.astype(BF16), vn)

    k = jnp.concatenate([ks0[0, 0], ks1[0, 0], ks2[0, 0], ks3[0, 0]], axis=1).astype(BF16)
    v = jnp.concatenate([vs0[0, 0], vs1[0, 0], vs2[0, 0], vs3[0, 0]], axis=1).astype(BF16)
    kpos = t * ATT_KT + lax.broadcasted_iota(I32, (1, ATT_KT), 1)
    selk = _expand_sel(sel_scr[...], t * (ATT_KT // SEL_BLK), ATT_KT)
    s = jnp.where(selk > 0.5, logits(_dot(q16, k), kpos), NEG)
    m = m_scr[...]
    m_new = jnp.maximum(m, jnp.max(s, -1, keepdims=True))
    a = jnp.exp(m - m_new)
    p = jnp.exp(s - m_new)
    m_scr[...] = m_new
    l_scr[...] = l_scr[...] * a + jnp.sum(p, -1, keepdims=True)
    acc_scr[...] = acc_scr[...] * a + _dot_nt(p.astype(BF16), v)

    @pl.when(t == pl.num_programs(1) - 1)
    def _():
        tot = part_scr[...] + gate[:, 1:2] * (acc_scr[...] / l_scr[...])
        o_ref[0] = jnp.concatenate([tot[g * grows:(g + 1) * grows, g * NSA_DH:(g + 1) * NSA_DH]
                                    for g in range(NSA_KV)], axis=0)


def _nsa_sample(qbd, gates, slopes, kc, vc, cache_t, page_table, kv_new, win_t, win_new, ovl, nbs, nq, nrep):
    nb, rows, pw = qbd.shape
    npg = page_table.shape[1]
    past = npg * PAGE
    ppt = ATT_KT // PAGE
    nt = npg // ppt
    nbc = kc.shape[1]
    wb = win_t.shape[3]
    const = lambda shape: pl.BlockSpec(shape, lambda b, t, pt: (0,) * len(shape))
    per_b = lambda n, w, col=0: pl.BlockSpec((1, n, w), lambda b, t, pt: (b, 0, col))
    page = lambda c, i: pl.BlockSpec((1, 1, pw, PAGE), lambda b, t, pt: (pt[b, ppt * t + i], c, 0, 0))
    wspec = lambda c: pl.BlockSpec((1, 1, pw, wb), lambda b, t, pt: (b, c, 0, 0))
    grid_spec = pltpu.PrefetchScalarGridSpec(
        num_scalar_prefetch=1,
        grid=(nb, nt),
        in_specs=[per_b(rows, pw), per_b(rows, SUBLANES), const((rows, SUBLANES)), per_b(nbc, pw), per_b(nbc, pw)]
                 + [page(2, i) for i in range(ppt)] + [page(3, i) for i in range(ppt)]
                 + [per_b(nq, pw, 2), per_b(nq, pw, 3), wspec(0), wspec(1),
                    per_b(nq, pw, 0), per_b(nq, pw, 1), const(ovl.shape)],
        out_specs=pl.BlockSpec((1, rows, NSA_DH), lambda b, t, pt: (b, 0, 0)),
        scratch_shapes=[pltpu.VMEM((rows, ovl.shape[1]), F32), pltpu.VMEM((rows, 1), F32), pltpu.VMEM((rows, 1), F32),
                        pltpu.VMEM((rows, pw), F32), pltpu.VMEM((rows, pw), F32)])
    return pl.pallas_call(
        functools.partial(_nsa_sample_kernel, nbs=nbs, past=past, nq=nq, nrep=nrep),
        grid_spec=grid_spec,
        out_shape=jax.ShapeDtypeStruct((nb, rows, NSA_DH), F32),
        compiler_params=_cparams(("parallel", "arbitrary"), 48),
        name="nsa_sample",
    )(page_table, qbd, gates, slopes, kc, vc, *([cache_t] * (2 * ppt)), kv_new, kv_new, win_t, win_t,
      win_new, win_new, ovl)


def _overlap(nbc, nbs_pad):
    ci = jnp.arange(nbc)[:, None] * CMP_STRIDE
    sj = jnp.arange(nbs_pad)[None, :] * SEL_BLK
    return ((ci < sj + SEL_BLK) & (ci + CMP_BLK > sj)).astype(BF16)


def _alibi_slopes(nh):
    return jnp.exp2(-8.0 * (jnp.arange(nh, dtype=F32) + 1.0) / nh)


def _nsa_in(x, w_in, nh):
    qw = nh * NSA_DH
    kvw = 6 * NSA_KV * NSA_DH
    proj = _matmul(x, w_in[:, :qw + kvw].astype(BF16), tn=512)
    ng = 3 * nh
    gates = _matmul(x, jnp.pad(w_in[:, qw + kvw:], ((0, 0), (0, LANES - ng))).astype(BF16), act="sigmoid")
    kv4w = 4 * NSA_KV * NSA_DH
    return proj[:, :qw], proj[:, qw:qw + kv4w], proj[:, qw + kv4w:], gates[:, :ng]


def _mm_t_kernel(w_ref, x_ref, o_ref):
    o_ref[0] = _dot_nt(w_ref[...], x_ref[...].astype(BF16))


def _matmul_t(x, wt, nb):
    m, k = x.shape
    n = wt.shape[0]
    seq = m // nb
    tl = _row_tile(seq, (1024, 512, 256, 128))
    tn = _row_tile(n, (512, 256, 128))
    return pl.pallas_call(
        _mm_t_kernel,
        grid=(nb, seq // tl, n // tn),
        in_specs=[pl.BlockSpec((tn, k), lambda b, l, j: (j, 0)),
                  pl.BlockSpec((tl, k), lambda b, l, j: (b * (seq // tl) + l, 0))],
        out_specs=pl.BlockSpec((1, tn, tl), lambda b, l, j: (b, j, l)),
        out_shape=jax.ShapeDtypeStruct((nb, n, seq), F32),
        compiler_params=_cparams(("parallel", "parallel", "arbitrary"), 48),
        name="matmul_t",
    )(wt, x)


def _nsa_prompt_mixer(x, nb, seq, prm):
    w_in, pos, w1, b1, w2, b2 = prm
    nh = (w_in.shape[1] - 6 * NSA_KV * NSA_DH) // (NSA_DH + 3)
    nrep = nh // NSA_KV
    pw = NSA_KV * NSA_DH
    qw = nh * NSA_DH
    qblk = 128
    nqb = seq // qblk
    rows = qblk * nrep
    w_tok = jnp.concatenate([w_in[:, :qw + 3 * pw], w_in[:, qw + 4 * pw:qw + 5 * pw]], axis=1)
    proj = _matmul(x, w_tok.astype(BF16), tn=512)
    kvt = _matmul_t(x, w_in[:, qw:qw + 6 * pw].T.astype(BF16), nb)
    ng = 3 * nh
    gates = _matmul(x, jnp.pad(w_in[:, qw + 6 * pw:], ((0, 0), (0, LANES - ng))).astype(BF16), act="sigmoid")[:, :ng]
    npg = seq // PAGE
    cmp_out = _compress(proj.reshape(nb * npg, PAGE, proj.shape[1]), jnp.arange(nb * npg, dtype=I32).reshape(nb, npg),
                        pos, w1, b1, w2, b2, col0=qw)
    q = proj[:, :qw] * (NSA_DH ** -0.5)
    qt = q.reshape(nb, nqb, qblk, NSA_KV, nrep, NSA_DH).transpose(0, 3, 1, 5, 4, 2).reshape(nb, NSA_KV, nqb, NSA_DH, rows)
    slf = jnp.repeat(_alibi_slopes(nh).reshape(NSA_KV, nrep), qblk, axis=1)
    s_hi = slf.astype(BF16).astype(F32)
    s_lo = (slf - s_hi).astype(BF16).astype(F32)
    extra = jnp.stack([POS_SPLIT * s_hi, s_hi, POS_SPLIT * s_lo, s_lo] + [jnp.zeros_like(slf)] * (Q_EXTRA - 4), axis=1)
    qt = jnp.concatenate([qt, jnp.broadcast_to(extra[None, :, None], (nb, NSA_KV, nqb, Q_EXTRA, rows))], axis=3)
    gs = gates.reshape(nb, nqb, qblk, 3, NSA_KV, nrep).transpose(0, 4, 1, 3, 5, 2).reshape(nb, NSA_KV, nqb, 3, rows)
    gs = jnp.pad(gs, ((0, 0),) * 3 + ((0, SUBLANES - 3), (0, 0)))
    sl = jnp.pad(slf[:, None, :], ((0, 0), (0, SUBLANES - 1), (0, 0)))
    t = jnp.arange(seq, dtype=I32)
    pos_cols = jnp.stack([t // POS_SPLIT, t % POS_SPLIT] * 2 + [jnp.zeros_like(t)] * (Q_EXTRA - 4), axis=1).astype(F32)
    pos_cols = jnp.broadcast_to(pos_cols[None, None], (nb, NSA_KV, seq, Q_EXTRA))
    with_pos = lambda cols: jnp.concatenate(
        [cols.reshape(nb, seq, NSA_KV, NSA_DH).transpose(0, 2, 1, 3), pos_cols], axis=-1).astype(BF16)
    ksa = with_pos(proj[:, qw + 2 * pw:qw + 3 * pw])
    kwa = with_pos(proj[:, qw + 3 * pw:qw + 4 * pw])
    nbs = -(-seq // SEL_BLK)
    nbs_r = -(-nbs // SUBLANES) * SUBLANES
    o = _nsa_prompt(qt, gs, sl, cmp_out[:, 0], cmp_out[:, 1].transpose(0, 1, 3, 2), ksa, kwa,
                    kvt.reshape(nb, 6 * NSA_KV, NSA_DH, seq), _overlap(seq // CMP_STRIDE, nbs_r).T, nbs)
    o = o.reshape(nb, NSA_KV, nqb, NSA_DH, nrep, qblk).transpose(0, 2, 5, 1, 4, 3).reshape(nb * seq, nh * NSA_DH)
    kv6 = kvt.reshape(nb, 6, NSA_KV, NSA_DH, seq).transpose(0, 4, 1, 2, 3)
    return o, kv6[:, :, :4], kv6[:, :, 4:]


def _nsa_sample_mixer(x, nb, nq, cache, page_table, win_buf, prm):
    w_in, pos, w1, b1, w2, b2 = prm
    nh = (w_in.shape[1] - 6 * NSA_KV * NSA_DH) // (NSA_DH + 3)
    nrep = nh // NSA_KV
    pw = NSA_KV * NSA_DH
    npg = page_table.shape[1]
    past = npg * PAGE
    q, kv4, win, gates = _nsa_in(x, w_in, nh)
    n_pool = cache.shape[0]
    cache_t = cache.transpose(0, 2, 3, 4, 1).reshape(n_pool, 4, pw, PAGE)
    win_t = win_buf.transpose(0, 2, 3, 4, 1).reshape(nb, 2, pw, win_buf.shape[1])
    cmp_out = _compress(cache_t, page_table, pos, w1, b1, w2, b2, token_minor=True)
    nsub = cmp_out.shape[3]
    kc = cmp_out[:, 0].transpose(0, 2, 1, 3).reshape(nb, nsub, pw)
    vc = cmp_out[:, 1].transpose(0, 2, 1, 3).reshape(nb, nsub, pw)
    rows = NSA_KV * nrep * nq
    q5 = q.reshape(nb, nq, NSA_KV, nrep, NSA_DH).transpose(0, 2, 3, 1, 4)
    qbd = jnp.einsum("bgrqd,gh->bgrqhd", q5, jnp.eye(NSA_KV, dtype=F32)).reshape(nb, rows, pw)
    gs = gates.reshape(nb, nq, 3, NSA_KV, nrep).transpose(0, 3, 4, 1, 2).reshape(nb, rows, 3)
    gs = jnp.pad(gs, ((0, 0), (0, 0), (0, SUBLANES - 3)))
    sl = jnp.repeat(_alibi_slopes(nh), nq)[:, None]
    sl = jnp.pad(sl, ((0, 0), (0, SUBLANES - 1)))
    nbs = -(-(past + nq) // SEL_BLK)
    nbs_pad = -(-nbs // LANES) * LANES
    o = _nsa_sample(qbd, gs, sl, kc, vc, cache_t, page_table, kv4.reshape(nb, nq, 4 * pw),
                    win_t, win.reshape(nb, nq, 2 * pw), _overlap(nsub, nbs_pad), nbs, nq, nrep)
    o = o.reshape(nb, NSA_KV, nrep, nq, NSA_DH).transpose(0, 3, 1, 2, 4).reshape(nb * nq, nh * NSA_DH)
    return o, kv4.reshape(nb, nq, 4, NSA_KV, NSA_DH), win.reshape(nb, nq, 2, NSA_KV, NSA_DH)


def kernel(x_prompt, x_sample, state_ssm, state_conv, cache_kv, state_win, page_table, ln_mix_g, ln_mix_b, ln_ffn_g, ln_ffn_b, ssd_w_in, ssd_conv_w, ssd_conv_b, ssd_dt_bias, ssd_a_log, ssd_d, ssd_norm_w, ssd_w_out, gm_w_in, gm_b_in, gm_ln_g, gm_ln_b, gm_w_s, gm_b_s, gm_w_out, nsa_w_in, nsa_cmp_pos, nsa_cmp_w1, nsa_cmp_b1, nsa_cmp_w2, nsa_cmp_b2, nsa_w_out, ffn_w_gate, ffn_w_up, ffn_w_down, moe_w_router, moe_b_router, moe_w_gate, moe_w_up, moe_w_down):
    depth = ln_mix_g.shape[0]
    alpha = (2 * depth) ** 0.25
    bp, seq, d = x_prompt.shape
    db, dseq, _ = x_sample.shape
    xp = x_prompt.reshape(bp * seq, d)
    xs = x_sample.reshape(db * dseq, d)
    outs = {k: [] for k in ("ssm_p", "conv_p", "ssm_s", "conv_s", "gmv_s", "kv_p", "win_p", "kv_s", "win_s")}
    row = lambda a, i: a[i][None, :]
    for l in range(depth):
        kind, j = l % N_MIXERS, l // N_MIXERS
        mg, mb = row(ln_mix_g, l), row(ln_mix_b, l)
        if kind == 0:
            prm = (ssd_w_in[j], ssd_conv_w[j], ssd_conv_b[j], ssd_dt_bias[j], ssd_a_log[j], ssd_d[j], ssd_norm_w[j])
            cdim = ssd_conv_w.shape[2]
            w_out = ssd_w_out[j].astype(BF16)
            yp, cp, hp = _ssd_mixer(xp, bp, SSD_CHUNK, jnp.zeros((bp, 8, cdim), F32),
                                    jnp.zeros((bp,) + state_ssm.shape[2:], F32), prm)
            ys, cs, hs = _ssd_mixer(xs, db, dseq, jnp.pad(state_conv[j], ((0, 0), (8 - (SSD_CONV - 1), 0), (0, 0))),
                                    state_ssm[j], prm)
            outs["ssm_p"].append(hp), outs["conv_p"].append(cp), outs["ssm_s"].append(hs), outs["conv_s"].append(cs)
            xp = _mm_res_ln(yp, xp, w_out, mg, mb, alpha)
            xs = _mm_res_ln(ys, xs, w_out, mg, mb, alpha)
        elif kind == 1:
            prm = (gm_w_in[j], gm_b_in[j], gm_ln_g[j], gm_ln_b[j], gm_w_s[j], gm_b_s[j], gm_w_out[j])
            xp, _ = _gmlp_mixer(xp, bp, seq, prm, mg, mb, alpha)
            xs, vs = _gmlp_mixer(xs, db, dseq, prm, mg, mb, alpha)
            outs["gmv_s"].append(vs.reshape(db, dseq, -1))
        else:
            prm = (nsa_w_in[j], nsa_cmp_pos[j], nsa_cmp_w1[j], nsa_cmp_b1[j], nsa_cmp_w2[j], nsa_cmp_b2[j])
            w_out = nsa_w_out[j].astype(BF16)
            op, kvp, wp = _nsa_prompt_mixer(xp, bp, seq, prm)
            os_, kvs, wsn = _nsa_sample_mixer(xs, db, dseq, cache_kv[j], page_table, state_win[j], prm)
            outs["kv_p"].append(kvp), outs["win_p"].append(wp[:, seq - min(WINDOW, seq):])
            outs["kv_s"].append(kvs), outs["win_s"].append(wsn)
            xp = _mm_res_ln(op, xp, w_out, mg, mb, alpha)
            xs = _mm_res_ln(os_, xs, w_out, mg, mb, alpha)
        f = l // 2
        fg, fb = row(ln_ffn_g, l), row(ln_ffn_b, l)
        if l % 2 == 0:
            wg, wu, wd = ffn_w_gate[f].astype(BF16), ffn_w_up[f].astype(BF16), ffn_w_down[f].astype(BF16)
            xp = _swiglu_res_ln(xp, wg, wu, wd, fg, fb, alpha)
            xs = _swiglu_res_ln(xs, wg, wu, wd, fg, fb, alpha)
        else:
            wg, wu, wd = moe_w_gate[f].astype(BF16), moe_w_up[f].astype(BF16), moe_w_down[f].astype(BF16)
            xp = _moe_res_ln(xp, moe_w_router[f], moe_b_router[f], wg, wu, wd, fg, fb, alpha)
            xs = _moe_res_ln(xs, moe_w_router[f], moe_b_router[f], wg, wu, wd, fg, fb, alpha)
    st = lambda k: jnp.stack(outs[k])
    return (xp.reshape(bp, seq, d), xs.reshape(db, dseq, d), st("ssm_p"), st("conv_p"), st("ssm_s"), st("conv_s"),
            st("gmv_s"), st("kv_p"), st("win_p"), st("kv_s"), st("win_s"))
```

```python
import functools

import jax
import jax.numpy as jnp
from jax import lax
from jax.experimental import pallas as pl
from jax.experimental.pallas import tpu as pltpu

F32 = jnp.float32
BF16 = jnp.bfloat16
I32 = jnp.int32

LN_EPS = 1e-5
NEG = -1e30
N_MIXERS = 3
SSD_P = 64
SSD_N = 128
SSD_G = 4
SSD_CONV = 4
SSD_CHUNK = 128
GM_CHUNK = 128
GM_G = 8
NSA_KV = 4
NSA_DH = 64
CMP_BLK = 32
CMP_STRIDE = 16
SEL_BLK = 64
N_SEL = 16
WINDOW = 512
FORCE = 1e4
TOP_K = 2

LANES = 128
SUBLANES = 8
MIB = 1 << 20


def _cparams(sem, vmem_mib):
    return pltpu.CompilerParams(dimension_semantics=sem, vmem_limit_bytes=vmem_mib * MIB)


def _row_tile(m, cands=(1024, 512, 256, 128, 64, 32, 16, 8)):
    for c in cands:
        if m % c == 0:
            return c
    raise ValueError(f"no row tile for {m}")


def _ln(v, g, b):
    mu = jnp.mean(v, -1, keepdims=True)
    d = v - mu
    var = jnp.mean(d * d, -1, keepdims=True)
    return d * lax.rsqrt(var + LN_EPS) * g + b


def _split(x, n):
    out = []
    r = x
    for k in range(n):
        h = r.astype(BF16)
        out.append(h)
        if k + 1 < n:
            r = r - h.astype(F32)
    return out


def _dot(a, b):
    return jnp.dot(a, b, preferred_element_type=F32)


def _dot_nt(a, b):
    return lax.dot_general(a, b, (((1,), (1,)), ((), ())), preferred_element_type=F32)


def _dot_tn(a, b):
    return lax.dot_general(a, b, (((0,), (0,)), ((), ())), preferred_element_type=F32)


def _silu(x):
    return x * jax.nn.sigmoid(x)


def _gelu_tanh(x):
    return 0.5 * x * (1.0 + jnp.tanh(0.7978845608028654 * (x + 0.044715 * (x * x * x))))


def _mm_kernel(x_ref, w_ref, b_ref, o_ref, *, act):
    acc = _dot(x_ref[...].astype(BF16), w_ref[...]) + b_ref[...]
    if act == "sigmoid":
        acc = jax.nn.sigmoid(acc)
    o_ref[...] = acc.astype(o_ref.dtype)


def _matmul(x, w, b=None, act=None, tn=None):
    m, k = x.shape
    n = w.shape[1]
    tm = _row_tile(m)
    tn = n if tn is None else tn
    if b is None:
        b = jnp.zeros((1, n), F32)
    return pl.pallas_call(
        functools.partial(_mm_kernel, act=act),
        grid=(m // tm, n // tn),
        in_specs=[pl.BlockSpec((tm, k), lambda i, j: (i, 0)),
                  pl.BlockSpec((k, tn), lambda i, j: (0, j)),
                  pl.BlockSpec((1, tn), lambda i, j: (0, j))],
        out_specs=pl.BlockSpec((tm, tn), lambda i, j: (i, j)),
        out_shape=jax.ShapeDtypeStruct((m, n), F32),
        compiler_params=_cparams(("parallel", "arbitrary"), 48),
        name="matmul",
    )(x, w, b)


def _mm_res_ln_kernel(y_ref, x_ref, w_ref, g_ref, b_ref, o_ref, *, alpha):
    f = _dot(y_ref[...].astype(BF16), w_ref[...])
    o_ref[...] = _ln(alpha * x_ref[...] + f, g_ref[...], b_ref[...])


def _mm_res_ln(y, x, w, g, b, alpha):
    m, k = y.shape
    d = x.shape[1]
    tm = _row_tile(m, (512, 256, 128))
    return pl.pallas_call(
        functools.partial(_mm_res_ln_kernel, alpha=alpha),
        grid=(m // tm,),
        in_specs=[pl.BlockSpec((tm, k), lambda i: (i, 0)),
                  pl.BlockSpec((tm, d), lambda i: (i, 0)),
                  pl.BlockSpec((k, d), lambda i: (0, 0)),
                  pl.BlockSpec((1, d), lambda i: (0, 0)),
                  pl.BlockSpec((1, d), lambda i: (0, 0))],
        out_specs=pl.BlockSpec((tm, d), lambda i: (i, 0)),
        out_shape=jax.ShapeDtypeStruct((m, d), F32),
        compiler_params=_cparams(("parallel",), 48),
        name="mm_res_ln",
    )(y, x, w, g, b)


def _swiglu_kernel(x_ref, wg_ref, wu_ref, wd_ref, g_ref, b_ref, o_ref, acc_ref, xb_ref, *, alpha):
    f = pl.program_id(1)

    @pl.when(f == 0)
    def _():
        acc_ref[...] = jnp.zeros_like(acc_ref)
        xb_ref[...] = x_ref[...].astype(BF16)

    xb = xb_ref[...]
    h = _silu(_dot(xb, wg_ref[...])) * _dot(xb, wu_ref[...])
    acc_ref[...] += _dot(h.astype(BF16), wd_ref[...])

    @pl.when(f == pl.num_programs(1) - 1)
    def _():
        o_ref[...] = _ln(alpha * x_ref[...] + acc_ref[...], g_ref[...], b_ref[...])


def _ff_tile(dff):
    for c in (512, 256, 128):
        if dff % c == 0:
            return c
    return dff


def _swiglu_res_ln(x, wg, wu, wd, g, b, alpha):
    m, d = x.shape
    dff = wg.shape[1]
    tm = _row_tile(m)
    tf = _ff_tile(dff)
    return pl.pallas_call(
        functools.partial(_swiglu_kernel, alpha=alpha),
        grid=(m // tm, dff // tf),
        in_specs=[pl.BlockSpec((tm, d), lambda i, f: (i, 0)),
                  pl.BlockSpec((d, tf), lambda i, f: (0, f)),
                  pl.BlockSpec((d, tf), lambda i, f: (0, f)),
                  pl.BlockSpec((tf, d), lambda i, f: (f, 0)),
                  pl.BlockSpec((1, d), lambda i, f: (0, 0)),
                  pl.BlockSpec((1, d), lambda i, f: (0, 0))],
        out_specs=pl.BlockSpec((tm, d), lambda i, f: (i, 0)),
        out_shape=jax.ShapeDtypeStruct((m, d), F32),
        scratch_shapes=[pltpu.VMEM((tm, d), F32), pltpu.VMEM((tm, d), BF16)],
        compiler_params=_cparams(("parallel", "arbitrary"), 48),
        name="swiglu_res_ln",
    )(x, wg, wu, wd, g, b)


def _ssd_kernel(xbc_ref, z_ref, dt_ref, conv0_ref, h0_ref, cw_ref, cb_ref, dtb_ref, alog_ref,
                dexp_ref, nw_ref, e_ref, ltri_ref, y_ref, h_ref, xp_scr, dtp_scr, yacc_scr,
                *, lv, n_heads):
    q = SSD_CHUNK
    di = n_heads * SSD_P
    gn = SSD_G * SSD_N
    hpg = n_heads // SSD_G
    gw = hpg * SSD_P
    ci = pl.program_id(1)

    @pl.when(ci == 0)
    def _():
        xp_scr[0:8, :] = conv0_ref[0]
        h_ref[0] = h0_ref[0]

    @pl.when(ci > 0)
    def _():
        xp_scr[0:8, :] = xp_scr[q:q + 8, :]

    if lv < q:
        xp_scr[8 + lv:, :] = jnp.zeros((q - lv, xp_scr.shape[1]), F32)
        dtp_scr[...] = jnp.zeros_like(dtp_scr)
        dtp_scr[0:lv, :] = dt_ref[...]
        dt_raw = dtp_scr[...]
    else:
        dt_raw = dt_ref[...]
    xp_scr[8:8 + lv, :] = xbc_ref[...]

    conv = cb_ref[...]
    for k in range(SSD_CONV):
        conv = conv + cw_ref[k:k + 1, :] * xp_scr[5 + k:5 + k + q, :]
    xc = _silu(conv)
    row = lax.broadcasted_iota(I32, (q, LANES), 0)
    lane = lax.broadcasted_iota(I32, (q, LANES), 1)
    if lv < q:
        xc = jnp.where(lax.broadcasted_iota(I32, xc.shape, 0) < lv, xc, 0.0)

    v = dt_raw + dtb_ref[...]
    dt = jnp.maximum(v, 0.0) + jnp.log1p(jnp.exp(-jnp.abs(v)))
    dt = jnp.where((row < lv) & (lane < n_heads), dt, 0.0)
    adt = dt * (-jnp.exp(alog_ref[...]))
    acum = sum(_dot(ltri_ref[...], t) for t in _split(adt, 3))
    acum_t = acum.T
    dt_t = dt.T
    ea = jnp.exp(acum)
    w = dt * jnp.exp(acum[q - 1:q, :] - acum)
    w_exp = sum(_dot(t, e_ref[...]) for t in _split(w, 2))
    ea_exp = sum(_dot(t, e_ref[...]) for t in _split(ea[:lv], 2))
    cdm = jnp.exp(jnp.broadcast_to(acum_t[:, q - 1:q], (LANES, LANES)))

    xs = xc[:, :di]
    xd = (xs * w_exp).astype(BF16)
    causal = (lax.broadcasted_iota(I32, (lv, q), 0) >= lax.broadcasted_iota(I32, (lv, q), 1))
    lo = lane < SSD_P

    for g in range(SSD_G):
        bg = xc[:, di + g * SSD_N:di + (g + 1) * SSD_N].astype(BF16)
        cg = xc[:lv, di + gn + g * SSD_N:di + gn + (g + 1) * SSD_N].astype(BF16)
        cbm = _dot_nt(cg, bg)
        hg = h_ref[0, g * gw:(g + 1) * gw, :].astype(BF16)
        yoff = _dot_nt(cg, hg)
        st = _dot_tn(xd[:, g * gw:(g + 1) * gw], bg)
        for pr in range(hpg // 2):
            col = g * gw + pr * LANES
            ms = []
            for hh in (2 * pr, 2 * pr + 1):
                h = g * hpg + hh
                seg = acum[:lv, h:h + 1] - acum_t[h:h + 1, :]
                dec = jnp.exp(jnp.where(causal, seg, -jnp.inf))
                ms.append((cbm * dec * dt_t[h:h + 1, :]).astype(BF16))
            xpair = xs[:, col:col + LANES]
            rhs = jnp.concatenate([jnp.where(lo, xpair, 0.0), jnp.where(lo, 0.0, xpair)], axis=0).astype(BF16)
            yd = _dot(jnp.concatenate(ms, axis=1), rhs)
            yacc_scr[:, col:col + LANES] = (yd + yoff[:, pr * LANES:(pr + 1) * LANES] * ea_exp[:, col:col + LANES]
                                            + dexp_ref[:, col:col + LANES] * xs[:lv, col:col + LANES])
        for hh in range(hpg):
            h = g * hpg + hh
            r0 = h * SSD_P
            h_ref[0, r0:r0 + SSD_P, :] = (h_ref[0, r0:r0 + SSD_P, :] * cdm[h:h + 1, :]
                                          + st[hh * SSD_P:(hh + 1) * SSD_P, :])

    y = yacc_scr[...] * _silu(z_ref[...])
    ng = di // SSD_G
    for g in range(SSD_G):
        yg = y[:, g * ng:(g + 1) * ng]
        ms_ = jnp.mean(yg * yg, -1, keepdims=True)
        y_ref[:, g * ng:(g + 1) * ng] = yg * lax.rsqrt(ms_ + LN_EPS) * nw_ref[:, g * ng:(g + 1) * ng]


def _ssd_scan(xbc, z, dt, conv0, h0, cw, cb, dtb, alog, dexp, nw, e, ltri, nb, lv):
    m, cdim = xbc.shape
    di = z.shape[1]
    n_heads = di // SSD_P
    nc = m // (nb * lv)
    return pl.pallas_call(
        functools.partial(_ssd_kernel, lv=lv, n_heads=n_heads),
        grid=(nb, nc),
        in_specs=[pl.BlockSpec((lv, cdim), lambda b, c: (b * nc + c, 0)),
                  pl.BlockSpec((lv, di), lambda b, c: (b * nc + c, 0)),
                  pl.BlockSpec((lv, LANES), lambda b, c: (b * nc + c, 0)),
                  pl.BlockSpec((1, 8, cdim), lambda b, c: (b, 0, 0)),
                  pl.BlockSpec((1, di, SSD_N), lambda b, c: (b, 0, 0)),
                  pl.BlockSpec((SSD_CONV, cdim), lambda b, c: (0, 0)),
                  pl.BlockSpec((1, cdim), lambda b, c: (0, 0)),
                  pl.BlockSpec((1, LANES), lambda b, c: (0, 0)),
                  pl.BlockSpec((1, LANES), lambda b, c: (0, 0)),
                  pl.BlockSpec((1, di), lambda b, c: (0, 0)),
                  pl.BlockSpec((1, di), lambda b, c: (0, 0)),
                  pl.BlockSpec((LANES, di), lambda b, c: (0, 0)),
                  pl.BlockSpec((SSD_CHUNK, SSD_CHUNK), lambda b, c: (0, 0))],
        out_specs=[pl.BlockSpec((lv, di), lambda b, c: (b * nc + c, 0)),
                   pl.BlockSpec((1, di, SSD_N), lambda b, c: (b, 0, 0))],
        out_shape=[jax.ShapeDtypeStruct((m, di), F32),
                   jax.ShapeDtypeStruct((nb, di, SSD_N), F32)],
        scratch_shapes=[pltpu.VMEM((SSD_CHUNK + 8, cdim), F32),
                        pltpu.VMEM((SSD_CHUNK, LANES), F32),
                        pltpu.VMEM((lv, di), F32)],
        compiler_params=_cparams(("parallel", "arbitrary"), 56),
        name="ssd_scan",
    )(xbc, z, dt, conv0, h0, cw, cb, dtb, alog, dexp, nw, e, ltri)


def _ssd_mixer(x, nb, lv, conv0, h0, prm):
    w_in, conv_w, conv_b, dt_bias, a_log, d_skip, norm_w = prm
    di = norm_w.shape[0]
    cdim = conv_w.shape[1]
    n_heads = di // SSD_P
    seq = x.shape[0] // nb
    w_z = w_in[:, :di].astype(BF16)
    w_x = w_in[:, di:di + cdim].astype(BF16)
    w_dt = jnp.pad(w_in[:, di + cdim:], ((0, 0), (0, LANES - n_heads))).astype(BF16)
    z = _matmul(x, w_z, tn=1024)
    xbc = _matmul(x, w_x, tn=1024)
    dt = _matmul(x, w_dt)
    pad1 = lambda a: jnp.pad(a.astype(F32), (0, LANES - n_heads))[None, :]
    e = (jnp.arange(di)[None, :] // SSD_P == jnp.arange(LANES)[:, None]).astype(BF16)
    ltri = (jnp.arange(SSD_CHUNK)[:, None] >= jnp.arange(SSD_CHUNK)[None, :]).astype(BF16)
    y, h_new = _ssd_scan(xbc, z, dt, conv0, h0.reshape(nb, di, SSD_N), conv_w, conv_b[None, :],
                         pad1(dt_bias), pad1(a_log), jnp.repeat(d_skip, SSD_P)[None, :], norm_w[None, :],
                         e, ltri, nb, lv)
    conv_new = xbc.reshape(nb, seq, cdim)[:, seq - (SSD_CONV - 1):]
    return y, conv_new, h_new.reshape(nb, n_heads, SSD_P, SSD_N)


def _gm_in_kernel(x_ref, w_ref, b_ref, g_ref, bb_ref, u_ref, v_ref):
    h = _gelu_tanh(_dot(x_ref[...].astype(BF16), w_ref[...]) + b_ref[...])
    d = u_ref.shape[1]
    u_ref[...] = h[:, :d]
    v_ref[...] = _ln(h[:, d:], g_ref[...], bb_ref[...])


def _gm_in(x, w, b, g, bb):
    m, k = x.shape
    d = w.shape[1] // 2
    tm = _row_tile(m, (512, 256))
    return pl.pallas_call(
        _gm_in_kernel,
        grid=(m // tm,),
        in_specs=[pl.BlockSpec((tm, k), lambda i: (i, 0)),
                  pl.BlockSpec((k, 2 * d), lambda i: (0, 0)),
                  pl.BlockSpec((1, 2 * d), lambda i: (0, 0)),
                  pl.BlockSpec((1, d), lambda i: (0, 0)),
                  pl.BlockSpec((1, d), lambda i: (0, 0))],
        out_specs=[pl.BlockSpec((tm, d), lambda i: (i, 0)), pl.BlockSpec((tm, d), lambda i: (i, 0))],
        out_shape=[jax.ShapeDtypeStruct((m, d), F32), jax.ShapeDtypeStruct((m, d), F32)],
        compiler_params=_cparams(("parallel",), 48),
        name="gm_in",
    )(x, w, b, g, bb)


def _gm_out_kernel(u_ref, v_ref, x_ref, s_ref, sb_ref, w_ref, g_ref, b_ref, o_ref, gated_scr, *, alpha, r):
    tb = u_ref.shape[0]
    gd = u_ref.shape[1] // GM_G
    for s in range(tb // r):
        rows = slice(s * r, (s + 1) * r)
        for g in range(GM_G):
            cols = slice(g * gd, (g + 1) * gd)
            mixed = _dot(s_ref[g], v_ref[rows, cols].astype(BF16)) + sb_ref[rows, cols]
            gated_scr[rows, cols] = (u_ref[rows, cols] * mixed).astype(BF16)
    f = _dot(gated_scr[...], w_ref[...])
    o_ref[...] = _ln(alpha * x_ref[...] + f, g_ref[...], b_ref[...])


def _gm_out(u, v, x, smat, sbias, w, g, b, alpha, tb):
    m, d = u.shape
    r = smat.shape[1]
    return pl.pallas_call(
        functools.partial(_gm_out_kernel, alpha=alpha, r=r),
        grid=(m // tb,),
        in_specs=[pl.BlockSpec((tb, d), lambda i: (i, 0)),
                  pl.BlockSpec((tb, d), lambda i: (i, 0)),
                  pl.BlockSpec((tb, d), lambda i: (i, 0)),
                  pl.BlockSpec((GM_G, r, r), lambda i: (0, 0, 0)),
                  pl.BlockSpec((tb, d), lambda i: (0, 0)),
                  pl.BlockSpec((d, d), lambda i: (0, 0)),
                  pl.BlockSpec((1, d), lambda i: (0, 0)),
                  pl.BlockSpec((1, d), lambda i: (0, 0))],
        out_specs=pl.BlockSpec((tb, d), lambda i: (i, 0)),
        out_shape=jax.ShapeDtypeStruct((m, d), F32),
        scratch_shapes=[pltpu.VMEM((tb, d), BF16)],
        compiler_params=_cparams(("parallel",), 48),
        name="gm_out",
    )(u, v, x, smat, sbias, w, g, b)


def _gmlp_mixer(x, nb, seq, prm, ln_g, ln_b, alpha):
    w_in, b_in, g_in, bb_in, w_s, b_s, w_out = prm
    d = w_out.shape[0]
    gd = d // GM_G
    u, v = _gm_in(x, w_in.astype(BF16), b_in[None, :], g_in[None, :], bb_in[None, :])
    ws = jnp.tril(w_s)
    if seq % GM_CHUNK == 0:
        tb = 512
        smat = ws.astype(BF16)
        bias_rows = jnp.repeat(b_s.T, gd, axis=1)
        sbias = jnp.tile(bias_rows, (tb // GM_CHUNK, 1))
    else:
        tb = nb * seq
        smat = jnp.einsum("ab,gts->gatbs", jnp.eye(nb, dtype=F32), ws[:, :seq, :seq]).reshape(GM_G, tb, tb).astype(BF16)
        sbias = jnp.tile(jnp.repeat(b_s.T[:seq], gd, axis=1), (nb, 1))
    out = _gm_out(u, v, x, smat, sbias, w_out.astype(BF16), ln_g, ln_b, alpha, tb)
    return out, v


def _router_kernel(x_ref, wh_ref, wl_ref, b_ref, eid_ref, gate_ref):
    x = x_ref[...]
    xh, xl = _split(x, 2)
    logits = _dot(xh, wh_ref[...]) + _dot(xl, wh_ref[...]) + _dot(xh, wl_ref[...]) + b_ref[...]
    lane = lax.broadcasted_iota(I32, logits.shape, 1)
    m1 = jnp.max(logits, -1, keepdims=True)
    i1 = jnp.min(jnp.where(logits == m1, lane, LANES), -1, keepdims=True)
    rest = jnp.where(lane == i1, NEG * 2, logits)
    m2 = jnp.max(rest, -1, keepdims=True)
    i2 = jnp.min(jnp.where(rest == m2, lane, LANES), -1, keepdims=True)
    e = jnp.exp(m2 - m1)
    g1 = 1.0 / (1.0 + e)
    eid_ref[...] = jnp.where(lane == 0, i1, jnp.where(lane == 1, i2, 0))
    gate_ref[...] = jnp.where(lane == 0, g1, jnp.where(lane == 1, e * g1, 0.0))


def _router(x, w_router, b_router):
    m, d = x.shape
    ne = w_router.shape[1]
    w = jnp.pad(w_router, ((0, 0), (0, LANES - ne)))
    wh = w.astype(BF16)
    wl = (w - wh.astype(F32)).astype(BF16)
    b = jnp.pad(b_router.astype(F32), (0, LANES - ne), constant_values=NEG)[None, :]
    tm = _row_tile(m, (512, 256))
    return pl.pallas_call(
        _router_kernel,
        grid=(m // tm,),
        in_specs=[pl.BlockSpec((tm, d), lambda i: (i, 0)),
                  pl.BlockSpec((d, LANES), lambda i: (0, 0)),
                  pl.BlockSpec((d, LANES), lambda i: (0, 0)),
                  pl.BlockSpec((1, LANES), lambda i: (0, 0))],
        out_specs=[pl.BlockSpec((tm, LANES), lambda i: (i, 0)), pl.BlockSpec((tm, LANES), lambda i: (i, 0))],
        out_shape=[jax.ShapeDtypeStruct((m, LANES), I32), jax.ShapeDtypeStruct((m, LANES), F32)],
        compiler_params=_cparams(("parallel",), 32),
        name="router",
    )(x, wh, wl, b)


def _row_copy(src_hbm, dst, s_row, d_row, sem):
    return pltpu.make_async_copy(src_hbm.at[pl.ds(s_row, 1), :], dst.at[pl.ds(d_row, 1), :], sem)


def _row_gather(src_hbm, dst, idx_ref, n, per_row, sem, start):
    if not start:
        for k in range(per_row):
            pltpu.make_async_copy(src_hbm.at[pl.ds(0, n), :], dst.at[k], sem).wait()
        return

    def body(j, c):
        for k in range(per_row):
            _row_copy(src_hbm, dst.at[k], idx_ref[0, 0, per_row * j + k], j, sem).start()
        return c

    lax.fori_loop(0, n, body, 0, unroll=8)


def _moe_ffn_kernel(be_ref, nu_ref, idx_ref, idxn_ref, x_hbm, wg_ref, wu_ref, wd_ref, o_ref,
                    acc_ref, xg_ref, xb_ref, sem, *, tb, rps):
    i = pl.program_id(0)
    f = pl.program_id(1)
    last = pl.num_programs(1) - 1
    used = i < nu_ref[0]
    slot = i % 2
    tbp = xg_ref.shape[2]

    def start_next_rows():
        base = f * rps
        for j in range(rps):
            _row_copy(x_hbm, xg_ref.at[1 - slot, 0], idxn_ref[0, 0, base + j], base + j, sem.at[1 - slot]).start()

    @pl.when((f == 0) & (i == 0))
    def _():
        _row_gather(x_hbm, xg_ref.at[0], idx_ref, tbp, 1, sem.at[0], True)

    @pl.when(f == 0)
    def _():
        _row_gather(x_hbm, xg_ref.at[slot], idx_ref, tbp, 1, sem.at[slot], False)

    @pl.when(used & (f == 0))
    def _():
        acc_ref[...] = jnp.zeros_like(acc_ref)
        xb_ref[...] = xg_ref[slot, 0, 0:tb, :].astype(BF16)

    @pl.when(used)
    def _():
        start_next_rows()
        xb = xb_ref[...]
        h = _silu(_dot(xb, wg_ref[0].astype(BF16))) * _dot(xb, wu_ref[0].astype(BF16))
        acc_ref[...] += _dot(h.astype(BF16), wd_ref[0].astype(BF16))

    @pl.when(jnp.logical_not(used))
    def _():
        start_next_rows()

    @pl.when(used & (f == last))
    def _():
        o_ref[...] = acc_ref[...]

    @pl.when(jnp.logical_not(used) & (f == last))
    def _():
        o_ref[...] = jnp.zeros_like(o_ref)

    @pl.when((i == pl.num_programs(0) - 1) & (f == last))
    def _():
        _row_gather(x_hbm, xg_ref.at[1 - slot], idx_ref, tbp, 1, sem.at[1 - slot], False)


def _moe_ffn(x, buf_tok, blk_e, n_used, wg, wu, wd, tb):
    d = x.shape[1]
    nblk = buf_tok.shape[0] // tb
    dff = wg.shape[2]
    tf = _ff_tile(dff)
    nf = dff // tf
    rps = -(-tb // (nf * SUBLANES)) * SUBLANES
    tbp = rps * nf

    def fe(i, f, be, nu):
        return jnp.where(i < nu[0], f, nf - 1)

    grid_spec = pltpu.PrefetchScalarGridSpec(
        num_scalar_prefetch=2,
        grid=(nblk, nf),
        in_specs=[pl.BlockSpec((1, 1, tbp), lambda i, f, be, nu: (i, 0, 0), memory_space=pltpu.SMEM),
                  pl.BlockSpec((1, 1, tbp), lambda i, f, be, nu: (jnp.minimum(i + 1, nblk - 1), 0, 0),
                               memory_space=pltpu.SMEM),
                  pl.BlockSpec(memory_space=pl.ANY),
                  pl.BlockSpec((1, d, tf), lambda i, f, be, nu: (be[i], 0, fe(i, f, be, nu))),
                  pl.BlockSpec((1, d, tf), lambda i, f, be, nu: (be[i], 0, fe(i, f, be, nu))),
                  pl.BlockSpec((1, tf, d), lambda i, f, be, nu: (be[i], fe(i, f, be, nu), 0))],
        out_specs=pl.BlockSpec((tb, d), lambda i, f, be, nu: (i, 0)),
        scratch_shapes=[pltpu.VMEM((tb, d), F32), pltpu.VMEM((2, 1, tbp, d), F32), pltpu.VMEM((tb, d), BF16),
                        pltpu.SemaphoreType.DMA((2,))])
    idx = jnp.pad(buf_tok.reshape(nblk, 1, tb), ((0, 0), (0, 0), (0, tbp - tb)))
    return pl.pallas_call(
        functools.partial(_moe_ffn_kernel, tb=tb, rps=rps),
        grid_spec=grid_spec,
        out_shape=jax.ShapeDtypeStruct((nblk * tb, d), F32),
        compiler_params=_cparams(("arbitrary", "arbitrary"), 56),
        name="moe_ffn",
    )(blk_e, n_used, idx, idx, x, wg, wu, wd)


def _combine_kernel(idx_ref, idxn_ref, yb_hbm, x_ref, gate_ref, g_ref, b_ref, o_ref, ybuf, sem, *, alpha, tm):
    i = pl.program_id(0)
    slot = i % 2

    @pl.when(i == 0)
    def _():
        _row_gather(yb_hbm, ybuf.at[0], idx_ref, tm, TOP_K, sem.at[0], True)

    @pl.when(i + 1 < pl.num_programs(0))
    def _():
        _row_gather(yb_hbm, ybuf.at[1 - slot], idxn_ref, tm, TOP_K, sem.at[1 - slot], True)

    _row_gather(yb_hbm, ybuf.at[slot], idx_ref, tm, TOP_K, sem.at[slot], False)
    gate = gate_ref[...]
    y = gate[:, 0:1] * ybuf[slot, 0] + gate[:, 1:2] * ybuf[slot, 1]
    o_ref[...] = _ln(alpha * x_ref[...] + y, g_ref[...], b_ref[...])


def _moe_combine(yb, dest, x, gate, g, b, alpha):
    m, d = x.shape
    tm = _row_tile(m, (256, 128))
    nblk = m // tm
    idx = dest.reshape(nblk, 1, TOP_K * tm)
    return pl.pallas_call(
        functools.partial(_combine_kernel, alpha=alpha, tm=tm),
        grid=(nblk,),
        in_specs=[pl.BlockSpec((1, 1, TOP_K * tm), lambda i: (i, 0, 0), memory_space=pltpu.SMEM),
                  pl.BlockSpec((1, 1, TOP_K * tm), lambda i: (jnp.minimum(i + 1, nblk - 1), 0, 0),
                               memory_space=pltpu.SMEM),
                  pl.BlockSpec(memory_space=pl.ANY),
                  pl.BlockSpec((tm, d), lambda i: (i, 0)),
                  pl.BlockSpec((tm, LANES), lambda i: (i, 0)),
                  pl.BlockSpec((1, d), lambda i: (0, 0)),
                  pl.BlockSpec((1, d), lambda i: (0, 0))],
        out_specs=pl.BlockSpec((tm, d), lambda i: (i, 0)),
        out_shape=jax.ShapeDtypeStruct((m, d), F32),
        scratch_shapes=[pltpu.VMEM((2, TOP_K, tm, d), F32), pltpu.SemaphoreType.DMA((2,))],
        compiler_params=_cparams(("arbitrary",), 32),
        name="moe_combine",
    )(idx, idx, yb, x, gate, g, b)


def _moe_res_ln(x, w_router, b_router, wg, wu, wd, g, b, alpha):
    m, d = x.shape
    ne = w_router.shape[1]
    eid, gate = _router(x, w_router, b_router)
    tk = m * TOP_K
    tb = 1024 if tk >= 16384 else 256
    ef = eid[:, :TOP_K].reshape(-1)
    onehot = (ef[:, None] == jnp.arange(ne, dtype=I32)[None, :]).astype(I32)
    csum = jnp.cumsum(onehot, axis=0)
    rank = jnp.sum(csum * onehot, axis=1) - 1
    counts = csum[-1]
    padded = (counts + tb - 1) // tb * tb
    pend = jnp.cumsum(padded)
    dest = (pend - padded)[ef] + rank
    nblk = -(-tk // tb) + ne
    buf_tok = jnp.zeros((nblk * tb,), I32).at[dest].set(jnp.arange(tk, dtype=I32) // TOP_K)
    blk_start = jnp.arange(nblk, dtype=I32) * tb
    blk_e = jnp.minimum(jnp.sum((pend[None, :] <= blk_start[:, None]).astype(I32), axis=1), ne - 1).astype(I32)
    n_used = (pend[-1:] // tb).astype(I32)
    yb = _moe_ffn(x, buf_tok, blk_e, n_used, wg, wu, wd, tb)
    return _moe_combine(yb, dest.astype(I32), x, gate, g, b, alpha)


PAGE = 128


def _compress_kernel(pt_ref, pages_hbm, w1_ref, pos_ref, b1_ref, w2_ref, b2_ref, o_ref, scr, acc_scr, cv_scr, sem,
                     *stage, npg, col0):
    b = pl.program_id(0)
    pw = NSA_KV * NSA_DH
    nsub = npg * PAGE // CMP_STRIDE
    hid = b1_ref.shape[2]

    nslab = 2 * pw // LANES

    def position_term():
        for c in range(2):
            cv = jnp.zeros((SUBLANES, hid), F32) + b1_ref[c]
            for o in range(CMP_STRIDE):
                cv = cv + _dot(pos_ref[c, o].astype(BF16), w1_ref[c, o])[:, :hid]
                cv = cv + _dot(pos_ref[c, CMP_STRIDE + o].astype(BF16), w1_ref[c, o])[:, hid:]
            cv_scr[c] = cv

    if not stage:
        def start(p, c_):
            row = pl.multiple_of(p * PAGE, PAGE)
            for sl in range(nslab):
                pltpu.make_async_copy(pages_hbm.at[pt_ref[b, p], :, pl.ds(col0 + sl * LANES, LANES)],
                                      scr.at[sl, pl.ds(row, PAGE), :], sem.at[0]).start()
            return c_

        lax.fori_loop(0, npg, start, 0)
        pl.when(b == 0)(position_term)
        for sl in range(nslab):
            pltpu.make_async_copy(scr.at[sl], scr.at[sl], sem.at[0]).wait()
    else:
        stg = stage[0]
        cpg = stg.shape[2]
        nch = npg // cpg

        def start_chunk(bb, k, slot):
            def body(p, c_):
                for c in range(2):
                    pltpu.make_async_copy(pages_hbm.at[pt_ref[bb, k * cpg + p], c], stg.at[slot, c, p],
                                          sem.at[slot]).start()
                return c_

            lax.fori_loop(0, cpg, body, 0)

        @pl.when(b == 0)
        def _():
            start_chunk(b, 0, 0)
            position_term()

        for k in range(nch):
            slot = k % 2
            if k + 1 < nch:
                start_chunk(b, k + 1, 1 - slot)
            else:
                pl.when(b + 1 < pl.num_programs(0))(functools.partial(start_chunk, b + 1, 0, (k + 1) % 2))
            for c in range(2):
                pltpu.make_async_copy(stg.at[slot, c], stg.at[slot, c], sem.at[slot]).wait()

            def to_token_major(p, c_):
                row = pl.multiple_of((k * cpg + p) * PAGE, PAGE)
                for sl in range(nslab):
                    c, half = divmod(sl, nslab // 2)
                    scr[sl, pl.ds(row, PAGE), :] = stg[slot, c, p, half * LANES:(half + 1) * LANES, :].T
                return c_

            lax.fori_loop(0, cpg, to_token_major, 0)

    gps = LANES // NSA_DH
    for c in range(2):
        acc_scr[...] = jnp.zeros_like(acc_scr)
        for o in range(CMP_STRIDE):
            for sl in range(nslab // 2):
                xo = scr[c * (nslab // 2) + sl, pl.ds(o, nsub, stride=CMP_STRIDE), :].astype(BF16)
                for gg in range(gps):
                    acc_scr[sl * gps + gg] += _dot(xo[:, gg * NSA_DH:(gg + 1) * NSA_DH], w1_ref[c, o])
        for g in range(NSA_KV):
            acc = acc_scr[g]
            pre = acc[:, :hid] + pltpu.roll(acc[:, hid:], nsub - 1, 0) + cv_scr[c, 0:1, :]
            o_ref[0, c, g] = _dot(_gelu_tanh(pre).astype(BF16), w2_ref[c]) + b2_ref[c]


def _compress(pages, page_table, pos, w1, b1, w2, b2, col0=0, token_minor=False):
    nb, npg = page_table.shape
    nsub = npg * PAGE // CMP_STRIDE
    hid = w1.shape[-1]
    pw = NSA_KV * NSA_DH
    w1cat = jnp.concatenate([w1[:, :CMP_STRIDE], w1[:, CMP_STRIDE:]], axis=-1).astype(BF16)
    posb = jnp.broadcast_to(pos[:, :, None, :], (2, CMP_BLK, SUBLANES, NSA_DH))
    chunk = 16
    assert not token_minor or (npg % chunk == 0 and (npg // chunk) % 2 == 0)
    grid_spec = pltpu.PrefetchScalarGridSpec(
        num_scalar_prefetch=1,
        grid=(nb,),
        in_specs=[pl.BlockSpec(memory_space=pl.ANY),
                  pl.BlockSpec((2, CMP_STRIDE, NSA_DH, 2 * hid), lambda b, pt: (0, 0, 0, 0)),
                  pl.BlockSpec((2, CMP_BLK, SUBLANES, NSA_DH), lambda b, pt: (0, 0, 0, 0)),
                  pl.BlockSpec((2, 1, hid), lambda b, pt: (0, 0, 0)),
                  pl.BlockSpec((2, hid, NSA_DH), lambda b, pt: (0, 0, 0)),
                  pl.BlockSpec((2, 1, NSA_DH), lambda b, pt: (0, 0, 0))],
        out_specs=pl.BlockSpec((1, 2, NSA_KV, nsub, NSA_DH), lambda b, pt: (b, 0, 0, 0, 0)),
        scratch_shapes=[pltpu.VMEM((2 * pw // LANES, npg * PAGE, LANES), F32),
                        pltpu.VMEM((NSA_KV, nsub, 2 * hid), F32),
                        pltpu.VMEM((2, SUBLANES, hid), F32),
                        pltpu.SemaphoreType.DMA((2,))]
                       + ([pltpu.VMEM((2, 2, chunk, pw, PAGE), F32)] if token_minor else []))
    return pl.pallas_call(
        functools.partial(_compress_kernel, npg=npg, col0=col0),
        grid_spec=grid_spec,
        out_shape=jax.ShapeDtypeStruct((nb, 2, NSA_KV, nsub, NSA_DH), F32),
        compiler_params=_cparams(("arbitrary",), 56),
        name="nsa_compress",
    )(page_table, pages, w1cat, posb, b1[:, None, :], w2.astype(BF16), b2[:, None, :])


def _masked_softmax(s, valid):
    s = jnp.where(valid, s, NEG)
    p = jnp.exp(s - jnp.max(s, -1, keepdims=True))
    return p / jnp.sum(p, -1, keepdims=True)


def _select_blocks(imp, ovl, tq, nbs):
    score = sum(_dot(t, ovl) for t in _split(imp, 3))
    jb = lax.broadcasted_iota(I32, score.shape, 1)
    ok = jb * SEL_BLK <= tq
    cur = tq // SEL_BLK
    forced = (jb == 0) | (jb == cur) | (jb == cur - 1)
    score = jnp.where(ok, score + jnp.where(forced, FORCE, 0.0), NEG)
    rank = jnp.zeros(score.shape, F32)
    for j2 in range(nbs):
        cj = score[:, j2:j2 + 1]
        beats = (cj > score) | ((cj == score) & (j2 < jb))
        rank = rank + jnp.where(beats, 1.0, 0.0)
    return jnp.where(ok & (rank < N_SEL), 1.0, 0.0)


def _expand_sel(sel, first_blk, nkeys):
    jj = lax.broadcasted_iota(I32, (sel.shape[1], nkeys), 0)
    kk = lax.broadcasted_iota(I32, (sel.shape[1], nkeys), 1)
    expand = jnp.where(kk // SEL_BLK + first_blk == jj, 1.0, 0.0).astype(BF16)
    return _dot(sel.astype(BF16), expand)


POS_SPLIT = 64
Q_EXTRA = 16


PROMPT_KT = 512


def _nsa_prompt_kernel(q_ref, gate_ref, slope_ref, kc_ref, vc_ref, ks_ref, vs_ref, kw_ref, vw_ref, ovl_ref, exp_ref,
                       o_ref, *, nbs, qblk, nrep):
    qb = pl.program_id(2)
    cols = qblk * nrep
    q = q_ref[0, 0, 0]
    q16 = q.astype(BF16)
    slope = slope_ref[0][0:1, :]
    tq = qb * qblk + lax.broadcasted_iota(I32, (1, cols), 1) % qblk
    tq_row = tq[:, 0:qblk]

    kc = kc_ref[0, 0]
    qh, ql = _split(q[:NSA_DH, :], 2)
    kh, kl = _split(kc, 2)
    s = _dot(kh, qh) + _dot(kh, ql) + _dot(kl, qh)
    e = lax.broadcasted_iota(I32, (kc.shape[0], 1), 0) * CMP_STRIDE + (CMP_BLK - 1)
    valid = e <= tq
    s = jnp.where(valid, s + slope * e.astype(F32), NEG)
    p = jnp.exp(s - jnp.max(s, 0, keepdims=True))
    p = jnp.where(valid, p / jnp.sum(p, 0, keepdims=True), 0.0)
    o_cmp = _dot(vc_ref[0, 0].astype(BF16), p.astype(BF16))
    imp = p[:, 0:qblk]
    for r in range(1, nrep):
        imp = imp + p[:, r * qblk:(r + 1) * qblk]

    score = sum(_dot(ovl_ref[...], t) for t in _split(imp, 3))
    jb = lax.broadcasted_iota(I32, score.shape, 0)
    ok = jb * SEL_BLK <= tq_row
    cur = tq_row // SEL_BLK
    forced = (jb == 0) | (jb == cur) | (jb == cur - 1)
    score = jnp.where(ok, score + jnp.where(forced, FORCE, 0.0), NEG)
    rank = jnp.zeros(score.shape, F32)
    for j2 in range(nbs):
        cj = score[j2:j2 + 1, :]
        beats = (cj > score) | ((cj == score) & (j2 < jb))
        rank = rank + jnp.where(beats, 1.0, 0.0)
    selneg = jnp.where(ok & (rank < N_SEL), 0.0, NEG).astype(BF16)

    def tile(k_ref, v_ref, bias_fn, kt, carry):
        m, l, acc = carry
        off = pl.multiple_of(kt * PROMPT_KT, PROMPT_KT)
        kpos = off + lax.broadcasted_iota(I32, (PROMPT_KT, 1), 0)
        s_ = _dot(k_ref[0, 0, pl.ds(off, PROMPT_KT), :], q16)
        s_ = s_ + jnp.concatenate([bias_fn(off, kpos)] * nrep, axis=1)
        m_new = jnp.maximum(m, jnp.max(s_, 0, keepdims=True))
        a = jnp.exp(m - m_new)
        p_ = jnp.exp(s_ - m_new)
        v = v_ref[0, 0, :, pl.ds(off, PROMPT_KT)].astype(BF16)
        return m_new, l * a + jnp.sum(p_, 0, keepdims=True), acc * a + _dot(v, p_.astype(BF16))

    def sel_bias(off, kpos):
        return jnp.where(kpos <= tq_row, _dot(exp_ref[pl.ds(off, PROMPT_KT), :], selneg), NEG)

    def win_bias(off, kpos):
        return jnp.where((kpos <= tq_row) & (kpos > tq_row - WINDOW), 0.0, NEG)

    kt_hi = (qb * qblk + qblk - 1) // PROMPT_KT + 1
    kt_win = jnp.maximum(qb * qblk - (WINDOW - 1), 0) // PROMPT_KT
    init = (jnp.full((1, cols), -jnp.inf, F32), jnp.zeros((1, cols), F32), jnp.zeros((NSA_DH, cols), F32))
    c_sel = lax.fori_loop(0, kt_win, lambda kt, c: tile(ks_ref, vs_ref, sel_bias, kt, c), init)
    c_sel, c_win = lax.fori_loop(
        kt_win, kt_hi,
        lambda kt, c: (tile(ks_ref, vs_ref, sel_bias, kt, c[0]), tile(kw_ref, vw_ref, win_bias, kt, c[1])),
        (c_sel, init))
    gate = gate_ref[0, 0, 0]
    o_ref[0, 0, 0] = (gate[0:1, :] * o_cmp + gate[1:2, :] * (c_sel[2] / c_sel[1])
                      + gate[2:3, :] * (c_win[2] / c_win[1]))


def _nsa_prompt(qt, gates, slopes, kc, vct, ksa, kwa, kvt, ovl_t, nbs):
    nb, ng, nqb, qw, cols = qt.shape
    dh = qw - Q_EXTRA
    seq = kvt.shape[3]
    nbc = kc.shape[2]
    qblk = seq // nqb
    nrep = cols // qblk
    qspec = lambda r: pl.BlockSpec((1, 1, 1, r, cols), lambda b, g, i: (b, g, i, 0, 0))
    per_g = lambda r, c: pl.BlockSpec((1, 1, r, c), lambda b, g, i: (b, g, 0, 0))
    tspec = lambda c: pl.BlockSpec((1, 1, dh, seq), lambda b, g, i: (b, c * ng + g, 0, 0))
    expand = (jnp.arange(seq)[:, None] // SEL_BLK == jnp.arange(ovl_t.shape[0])[None, :]).astype(BF16)
    return pl.pallas_call(
        functools.partial(_nsa_prompt_kernel, nbs=nbs, qblk=qblk, nrep=nrep),
        grid=(nb, ng, nqb),
        in_specs=[qspec(qw), qspec(SUBLANES),
                  pl.BlockSpec((1, SUBLANES, cols), lambda b, g, i: (g, 0, 0)),
                  per_g(nbc, dh), per_g(dh, nbc), per_g(seq, qw), tspec(3), per_g(seq, qw), tspec(5),
                  pl.BlockSpec(ovl_t.shape, lambda b, g, i: (0, 0)),
                  pl.BlockSpec(expand.shape, lambda b, g, i: (0, 0))],
        out_specs=qspec(dh),
        out_shape=jax.ShapeDtypeStruct((nb, ng, nqb, dh, cols), F32),
        compiler_params=_cparams(("parallel", "parallel", "arbitrary"), 48),
        name="nsa_prompt",
    )(qt, gates, slopes, kc, vct, ksa, kvt, kwa, kvt, ovl_t, expand)


def _nsa_sample_kernel(pt_ref, q_ref, gate_ref, slope_ref, kc_ref, vc_ref, *refs, nbs, past, nq, nrep, ppt):
    ks_pages, vs_pages = refs[:ppt], refs[ppt:2 * ppt]
    (ksn_ref, vsn_ref, kwp_ref, vwp_ref, kwn_ref, vwn_ref, ovl_ref, o_ref,
     sel_scr, m_scr, l_scr, acc_scr, part_scr) = refs[2 * ppt:]
    keys = ppt * PAGE
    t = pl.program_id(1)
    rows = q_ref.shape[1]
    pw = q_ref.shape[2]
    grows = rows // NSA_KV
    scale = NSA_DH ** -0.5
    q = q_ref[0]
    q16 = q.astype(BF16)
    slope = slope_ref[:, 0:1]
    tq = past + lax.broadcasted_iota(I32, (rows, 1), 0) % nq
    tqf = tq.astype(F32)
    gate = gate_ref[0]

    def logits(qk, kpos):
        return qk * scale - slope * (tqf - kpos.astype(F32))

    @pl.when(t == 0)
    def _():
        kc = kc_ref[0]
        qh, ql = _split(q, 2)
        kh, kl = _split(kc, 2)
        s = (_dot_nt(qh, kh) + _dot_nt(ql, kh) + _dot_nt(qh, kl)) * scale
        e = lax.broadcasted_iota(I32, (1, kc.shape[0]), 1) * CMP_STRIDE + (CMP_BLK - 1)
        valid = e <= tq
        p = jnp.where(valid, _masked_softmax(s - slope * (tqf - e.astype(F32)), valid), 0.0)
        o_cmp = _dot(p.astype(BF16), vc_ref[0].astype(BF16))
        imps = []
        for g in range(NSA_KV):
            a = p[g * grows:g * grows + nq]
            for r in range(1, nrep):
                a = a + p[g * grows + r * nq:g * grows + (r + 1) * nq]
            imps.append(a)
        imp = jnp.concatenate(imps, axis=0)
        tq_s = past + lax.broadcasted_iota(I32, (NSA_KV * nq, 1), 0) % nq
        sel_s = _select_blocks(imp, ovl_ref[...], tq_s, nbs)
        sel = jnp.concatenate([sel_s[g * nq:(g + 1) * nq] for g in range(NSA_KV) for _ in range(nrep)], axis=0)
        sel_scr[...] = sel

        wb = kwp_ref.shape[3]
        npad = LANES - nq
        zpad = jnp.zeros((npad, pw), F32)
        kwn = jnp.concatenate([kwn_ref[0], zpad], axis=0).astype(BF16)
        vwn = jnp.concatenate([vwn_ref[0], zpad], axis=0).astype(BF16)
        qk = jnp.concatenate([_dot(q16, kwp_ref[0, 0].astype(BF16)), _dot_nt(q16, kwn)], axis=1)
        idx = lax.broadcasted_iota(I32, (1, wb + LANES), 1)
        kpos = past - wb + idx
        valid = (idx < wb + nq) & (kpos <= tq) & (kpos > tq - WINDOW)
        pw_ = _masked_softmax(logits(qk, kpos), valid).astype(BF16)
        o_win = _dot_nt(pw_[:, :wb], vwp_ref[0, 0].astype(BF16)) + _dot(pw_[:, wb:], vwn)
        part_scr[...] = gate[:, 0:1] * o_cmp + gate[:, 2:3] * o_win

        kn = jnp.concatenate([ksn_ref[0], zpad], axis=0).astype(BF16)
        vn = jnp.concatenate([vsn_ref[0], zpad], axis=0).astype(BF16)
        idx = lax.broadcasted_iota(I32, (1, LANES), 1)
        kpos = past + idx
        blk = past // SEL_BLK
        valid = (idx < nq) & (kpos <= tq) & (sel[:, blk:blk + 1] > 0.5)
        s = jnp.where(valid, logits(_dot_nt(q16, kn), kpos), NEG)
        m = jnp.max(s, -1, keepdims=True)
        p = jnp.exp(s - m)
        m_scr[...] = m
        l_scr[...] = jnp.sum(p, -1, keepdims=True)
        acc_scr[...] = _dot(p.astype(BF16), vn)

    k = jnp.concatenate([r[0, 0] for r in ks_pages], axis=1).astype(BF16)
    v = jnp.concatenate([r[0, 0] for r in vs_pages], axis=1).astype(BF16)
    kpos = t * keys + lax.broadcasted_iota(I32, (1, keys), 1)
    selk = _expand_sel(sel_scr[...], t * (keys // SEL_BLK), keys)
    s = jnp.where(selk > 0.5, logits(_dot(q16, k), kpos), NEG)
    m = m_scr[...]
    m_new = jnp.maximum(m, jnp.max(s, -1, keepdims=True))
    a = jnp.exp(m - m_new)
    p = jnp.exp(s - m_new)
    m_scr[...] = m_new
    l_scr[...] = l_scr[...] * a + jnp.sum(p, -1, keepdims=True)
    acc_scr[...] = acc_scr[...] * a + _dot_nt(p.astype(BF16), v)

    @pl.when(t == pl.num_programs(1) - 1)
    def _():
        tot = part_scr[...] + gate[:, 1:2] * (acc_scr[...] / l_scr[...])
        o_ref[0] = jnp.concatenate([tot[g * grows:(g + 1) * grows, g * NSA_DH:(g + 1) * NSA_DH]
                                    for g in range(NSA_KV)], axis=0)


def _nsa_sample(qbd, gates, slopes, kc, vc, cache_t, page_table, kv_new, win_t, win_new, ovl, nbs, nq, nrep):
    nb, rows, pw = qbd.shape
    npg = page_table.shape[1]
    past = npg * PAGE
    ppt = max(p for p in (8, 4, 2, 1) if npg % p == 0)
    nt = npg // ppt
    nbc = kc.shape[1]
    wb = win_t.shape[3]
    const = lambda shape: pl.BlockSpec(shape, lambda b, t, pt: (0,) * len(shape))
    per_b = lambda n, w, col=0: pl.BlockSpec((1, n, w), lambda b, t, pt: (b, 0, col))
    page = lambda c, i: pl.BlockSpec((1, 1, pw, PAGE), lambda b, t, pt: (pt[b, ppt * t + i], c, 0, 0))
    wspec = lambda c: pl.BlockSpec((1, 1, pw, wb), lambda b, t, pt: (b, c, 0, 0))
    grid_spec = pltpu.PrefetchScalarGridSpec(
        num_scalar_prefetch=1,
        grid=(nb, nt),
        in_specs=[per_b(rows, pw), per_b(rows, SUBLANES), const((rows, SUBLANES)), per_b(nbc, pw), per_b(nbc, pw)]
                 + [page(2, i) for i in range(ppt)] + [page(3, i) for i in range(ppt)]
                 + [per_b(nq, pw, 2), per_b(nq, pw, 3), wspec(0), wspec(1),
                    per_b(nq, pw, 0), per_b(nq, pw, 1), const(ovl.shape)],
        out_specs=pl.BlockSpec((1, rows, NSA_DH), lambda b, t, pt: (b, 0, 0)),
        scratch_shapes=[pltpu.VMEM((rows, ovl.shape[1]), F32), pltpu.VMEM((rows, 1), F32), pltpu.VMEM((rows, 1), F32),
                        pltpu.VMEM((rows, pw), F32), pltpu.VMEM((rows, pw), F32)])
    return pl.pallas_call(
        functools.partial(_nsa_sample_kernel, nbs=nbs, past=past, nq=nq, nrep=nrep, ppt=ppt),
        grid_spec=grid_spec,
        out_shape=jax.ShapeDtypeStruct((nb, rows, NSA_DH), F32),
        compiler_params=_cparams(("parallel", "arbitrary"), 48),
        name="nsa_sample",
    )(page_table, qbd, gates, slopes, kc, vc, *([cache_t] * (2 * ppt)), kv_new, kv_new, win_t, win_t,
      win_new, win_new, ovl)


def _overlap(nbc, nbs_pad):
    ci = jnp.arange(nbc)[:, None] * CMP_STRIDE
    sj = jnp.arange(nbs_pad)[None, :] * SEL_BLK
    return ((ci < sj + SEL_BLK) & (ci + CMP_BLK > sj)).astype(BF16)


def _alibi_slopes(nh):
    return jnp.exp2(-8.0 * (jnp.arange(nh, dtype=F32) + 1.0) / nh)


def _nsa_in(x, w_in, nh):
    qw = nh * NSA_DH
    kvw = 6 * NSA_KV * NSA_DH
    proj = _matmul(x, w_in[:, :qw + kvw].astype(BF16), tn=512)
    ng = 3 * nh
    gates = _matmul(x, jnp.pad(w_in[:, qw + kvw:], ((0, 0), (0, LANES - ng))).astype(BF16), act="sigmoid")
    kv4w = 4 * NSA_KV * NSA_DH
    return proj[:, :qw], proj[:, qw:qw + kv4w], proj[:, qw + kv4w:], gates[:, :ng]


def _mm_t_kernel(w_ref, x_ref, o_ref):
    o_ref[0] = _dot_nt(w_ref[...], x_ref[...].astype(BF16))


def _matmul_t(x, wt, nb):
    m, k = x.shape
    n = wt.shape[0]
    seq = m // nb
    tl = _row_tile(seq, (1024, 512, 256, 128))
    tn = _row_tile(n, (512, 256, 128))
    return pl.pallas_call(
        _mm_t_kernel,
        grid=(nb, seq // tl, n // tn),
        in_specs=[pl.BlockSpec((tn, k), lambda b, l, j: (j, 0)),
                  pl.BlockSpec((tl, k), lambda b, l, j: (b * (seq // tl) + l, 0))],
        out_specs=pl.BlockSpec((1, tn, tl), lambda b, l, j: (b, j, l)),
        out_shape=jax.ShapeDtypeStruct((nb, n, seq), F32),
        compiler_params=_cparams(("parallel", "parallel", "arbitrary"), 48),
        name="matmul_t",
    )(wt, x)


def _nsa_prompt_mixer(x, nb, seq, prm):
    w_in, pos, w1, b1, w2, b2 = prm
    nh = (w_in.shape[1] - 6 * NSA_KV * NSA_DH) // (NSA_DH + 3)
    nrep = nh // NSA_KV
    pw = NSA_KV * NSA_DH
    qw = nh * NSA_DH
    qblk = 128
    nqb = seq // qblk
    rows = qblk * nrep
    w_tok = jnp.concatenate([w_in[:, :qw + 3 * pw], w_in[:, qw + 4 * pw:qw + 5 * pw]], axis=1)
    proj = _matmul(x, w_tok.astype(BF16), tn=512)
    kvt = _matmul_t(x, w_in[:, qw:qw + 6 * pw].T.astype(BF16), nb)
    ng = 3 * nh
    gates = _matmul(x, jnp.pad(w_in[:, qw + 6 * pw:], ((0, 0), (0, LANES - ng))).astype(BF16), act="sigmoid")[:, :ng]
    npg = seq // PAGE
    cmp_out = _compress(proj.reshape(nb * npg, PAGE, proj.shape[1]), jnp.arange(nb * npg, dtype=I32).reshape(nb, npg),
                        pos, w1, b1, w2, b2, col0=qw)
    q = proj[:, :qw] * (NSA_DH ** -0.5)
    qt = q.reshape(nb, nqb, qblk, NSA_KV, nrep, NSA_DH).transpose(0, 3, 1, 5, 4, 2).reshape(nb, NSA_KV, nqb, NSA_DH, rows)
    slf = jnp.repeat(_alibi_slopes(nh).reshape(NSA_KV, nrep), qblk, axis=1)
    s_hi = slf.astype(BF16).astype(F32)
    s_lo = (slf - s_hi).astype(BF16).astype(F32)
    extra = jnp.stack([POS_SPLIT * s_hi, s_hi, POS_SPLIT * s_lo, s_lo] + [jnp.zeros_like(slf)] * (Q_EXTRA - 4), axis=1)
    qt = jnp.concatenate([qt, jnp.broadcast_to(extra[None, :, None], (nb, NSA_KV, nqb, Q_EXTRA, rows))], axis=3)
    gs = gates.reshape(nb, nqb, qblk, 3, NSA_KV, nrep).transpose(0, 4, 1, 3, 5, 2).reshape(nb, NSA_KV, nqb, 3, rows)
    gs = jnp.pad(gs, ((0, 0),) * 3 + ((0, SUBLANES - 3), (0, 0)))
    sl = jnp.pad(slf[:, None, :], ((0, 0), (0, SUBLANES - 1), (0, 0)))
    t = jnp.arange(seq, dtype=I32)
    pos_cols = jnp.stack([t // POS_SPLIT, t % POS_SPLIT] * 2 + [jnp.zeros_like(t)] * (Q_EXTRA - 4), axis=1).astype(F32)
    pos_cols = jnp.broadcast_to(pos_cols[None, None], (nb, NSA_KV, seq, Q_EXTRA))
    with_pos = lambda cols: jnp.concatenate(
        [cols.reshape(nb, seq, NSA_KV, NSA_DH).transpose(0, 2, 1, 3), pos_cols], axis=-1).astype(BF16)
    ksa = with_pos(proj[:, qw + 2 * pw:qw + 3 * pw])
    kwa = with_pos(proj[:, qw + 3 * pw:qw + 4 * pw])
    nbs = -(-seq // SEL_BLK)
    nbs_r = -(-nbs // SUBLANES) * SUBLANES
    o = _nsa_prompt(qt, gs, sl, cmp_out[:, 0], cmp_out[:, 1].transpose(0, 1, 3, 2), ksa, kwa,
                    kvt.reshape(nb, 6 * NSA_KV, NSA_DH, seq), _overlap(seq // CMP_STRIDE, nbs_r).T, nbs)
    o = o.reshape(nb, NSA_KV, nqb, NSA_DH, nrep, qblk).transpose(0, 2, 5, 1, 4, 3).reshape(nb * seq, nh * NSA_DH)
    kv6 = kvt.reshape(nb, 6, NSA_KV, NSA_DH, seq).transpose(0, 4, 1, 2, 3)
    return o, kv6[:, :, :4], kv6[:, :, 4:]


def _nsa_sample_mixer(x, nb, nq, cache, page_table, win_buf, prm):
    w_in, pos, w1, b1, w2, b2 = prm
    nh = (w_in.shape[1] - 6 * NSA_KV * NSA_DH) // (NSA_DH + 3)
    nrep = nh // NSA_KV
    pw = NSA_KV * NSA_DH
    npg = page_table.shape[1]
    past = npg * PAGE
    q, kv4, win, gates = _nsa_in(x, w_in, nh)
    n_pool = cache.shape[0]
    cache_t = cache.transpose(0, 2, 3, 4, 1).reshape(n_pool, 4, pw, PAGE)
    win_t = win_buf.transpose(0, 2, 3, 4, 1).reshape(nb, 2, pw, win_buf.shape[1])
    cmp_out = _compress(cache_t, page_table, pos, w1, b1, w2, b2, token_minor=True)
    nsub = cmp_out.shape[3]
    kc = cmp_out[:, 0].transpose(0, 2, 1, 3).reshape(nb, nsub, pw)
    vc = cmp_out[:, 1].transpose(0, 2, 1, 3).reshape(nb, nsub, pw)
    rows = NSA_KV * nrep * nq
    q5 = q.reshape(nb, nq, NSA_KV, nrep, NSA_DH).transpose(0, 2, 3, 1, 4)
    qbd = jnp.einsum("bgrqd,gh->bgrqhd", q5, jnp.eye(NSA_KV, dtype=F32)).reshape(nb, rows, pw)
    gs = gates.reshape(nb, nq, 3, NSA_KV, nrep).transpose(0, 3, 4, 1, 2).reshape(nb, rows, 3)
    gs = jnp.pad(gs, ((0, 0), (0, 0), (0, SUBLANES - 3)))
    sl = jnp.repeat(_alibi_slopes(nh), nq)[:, None]
    sl = jnp.pad(sl, ((0, 0), (0, SUBLANES - 1)))
    nbs = -(-(past + nq) // SEL_BLK)
    nbs_pad = -(-nbs // LANES) * LANES
    o = _nsa_sample(qbd, gs, sl, kc, vc, cache_t, page_table, kv4.reshape(nb, nq, 4 * pw),
                    win_t, win.reshape(nb, nq, 2 * pw), _overlap(nsub, nbs_pad), nbs, nq, nrep)
    o = o.reshape(nb, NSA_KV, nrep, nq, NSA_DH).transpose(0, 3, 1, 2, 4).reshape(nb * nq, nh * NSA_DH)
    return o, kv4.reshape(nb, nq, 4, NSA_KV, NSA_DH), win.reshape(nb, nq, 2, NSA_KV, NSA_DH)


def kernel(x_prompt, x_sample, state_ssm, state_conv, cache_kv, state_win, page_table, ln_mix_g, ln_mix_b, ln_ffn_g, ln_ffn_b, ssd_w_in, ssd_conv_w, ssd_conv_b, ssd_dt_bias, ssd_a_log, ssd_d, ssd_norm_w, ssd_w_out, gm_w_in, gm_b_in, gm_ln_g, gm_ln_b, gm_w_s, gm_b_s, gm_w_out, nsa_w_in, nsa_cmp_pos, nsa_cmp_w1, nsa_cmp_b1, nsa_cmp_w2, nsa_cmp_b2, nsa_w_out, ffn_w_gate, ffn_w_up, ffn_w_down, moe_w_router, moe_b_router, moe_w_gate, moe_w_up, moe_w_down):
    depth = ln_mix_g.shape[0]
    alpha = (2 * depth) ** 0.25
    bp, seq, d = x_prompt.shape
    db, dseq, _ = x_sample.shape
    xp = x_prompt.reshape(bp * seq, d)
    xs = x_sample.reshape(db * dseq, d)
    outs = {k: [] for k in ("ssm_p", "conv_p", "ssm_s", "conv_s", "gmv_s", "kv_p", "win_p", "kv_s", "win_s")}
    row = lambda a, i: a[i][None, :]
    for l in range(depth):
        kind, j = l % N_MIXERS, l // N_MIXERS
        mg, mb = row(ln_mix_g, l), row(ln_mix_b, l)
        if kind == 0:
            prm = (ssd_w_in[j], ssd_conv_w[j], ssd_conv_b[j], ssd_dt_bias[j], ssd_a_log[j], ssd_d[j], ssd_norm_w[j])
            cdim = ssd_conv_w.shape[2]
            w_out = ssd_w_out[j].astype(BF16)
            yp, cp, hp = _ssd_mixer(xp, bp, SSD_CHUNK, jnp.zeros((bp, 8, cdim), F32),
                                    jnp.zeros((bp,) + state_ssm.shape[2:], F32), prm)
            ys, cs, hs = _ssd_mixer(xs, db, dseq, jnp.pad(state_conv[j], ((0, 0), (8 - (SSD_CONV - 1), 0), (0, 0))),
                                    state_ssm[j], prm)
            outs["ssm_p"].append(hp), outs["conv_p"].append(cp), outs["ssm_s"].append(hs), outs["conv_s"].append(cs)
            xp = _mm_res_ln(yp, xp, w_out, mg, mb, alpha)
            xs = _mm_res_ln(ys, xs, w_out, mg, mb, alpha)
        elif kind == 1:
            prm = (gm_w_in[j], gm_b_in[j], gm_ln_g[j], gm_ln_b[j], gm_w_s[j], gm_b_s[j], gm_w_out[j])
            xp, _ = _gmlp_mixer(xp, bp, seq, prm, mg, mb, alpha)
            xs, vs = _gmlp_mixer(xs, db, dseq, prm, mg, mb, alpha)
            outs["gmv_s"].append(vs.reshape(db, dseq, -1))
        else:
            prm = (nsa_w_in[j], nsa_cmp_pos[j], nsa_cmp_w1[j], nsa_cmp_b1[j], nsa_cmp_w2[j], nsa_cmp_b2[j])
            w_out = nsa_w_out[j].astype(BF16)
            op, kvp, wp = _nsa_prompt_mixer(xp, bp, seq, prm)
            os_, kvs, wsn = _nsa_sample_mixer(xs, db, dseq, cache_kv[j], page_table, state_win[j], prm)
            outs["kv_p"].append(kvp), outs["win_p"].append(wp[:, seq - min(WINDOW, seq):])
            outs["kv_s"].append(kvs), outs["win_s"].append(wsn)
            xp = _mm_res_ln(op, xp, w_out, mg, mb, alpha)
            xs = _mm_res_ln(os_, xs, w_out, mg, mb, alpha)
        f = l // 2
        fg, fb = row(ln_ffn_g, l), row(ln_ffn_b, l)
        if l % 2 == 0:
            wg, wu, wd = ffn_w_gate[f].astype(BF16), ffn_w_up[f].astype(BF16), ffn_w_down[f].astype(BF16)
            xp = _swiglu_res_ln(xp, wg, wu, wd, fg, fb, alpha)
            xs = _swiglu_res_ln(xs, wg, wu, wd, fg, fb, alpha)
        else:
            wg, wu, wd = moe_w_gate[f], moe_w_up[f], moe_w_down[f]
            xp = _moe_res_ln(xp, moe_w_router[f], moe_b_router[f], wg, wu, wd, fg, fb, alpha)
            xs = _moe_res_ln(xs, moe_w_router[f], moe_b_router[f], wg, wu, wd, fg, fb, alpha)
    st = lambda k: jnp.stack(outs[k])
    return (xp.reshape(bp, seq, d), xs.reshape(db, dseq, d), st("ssm_p"), st("conv_p"), st("ssm_s"), st("conv_s"),
            st("gmv_s"), st("kv_p"), st("win_p"), st("kv_s"), st("win_s"))
```

```python
import functools

import jax
import jax.numpy as jnp
from jax import lax
from jax.experimental import pallas as pl
from jax.experimental.pallas import tpu as pltpu

F32 = jnp.float32
BF16 = jnp.bfloat16
I32 = jnp.int32

LN_EPS = 1e-5
NEG = -1e30
N_MIXERS = 3
SSD_P = 64
SSD_N = 128
SSD_G = 4
SSD_CONV = 4
SSD_CHUNK = 128
GM_CHUNK = 128
GM_G = 8
NSA_KV = 4
NSA_DH = 64
CMP_BLK = 32
CMP_STRIDE = 16
SEL_BLK = 64
N_SEL = 16
WINDOW = 512
FORCE = 1e4
TOP_K = 2

LANES = 128
SUBLANES = 8
MIB = 1 << 20


def _cparams(sem, vmem_mib):
    return pltpu.CompilerParams(dimension_semantics=sem, vmem_limit_bytes=vmem_mib * MIB)


def _row_tile(m, cands=(1024, 512, 256, 128, 64, 32, 16, 8)):
    for c in cands:
        if m % c == 0:
            return c
    raise ValueError(f"no row tile for {m}")


def _ln(v, g, b):
    mu = jnp.mean(v, -1, keepdims=True)
    d = v - mu
    var = jnp.mean(d * d, -1, keepdims=True)
    return d * lax.rsqrt(var + LN_EPS) * g + b


def _split(x, n):
    out = []
    r = x
    for k in range(n):
        h = r.astype(BF16)
        out.append(h)
        if k + 1 < n:
            r = r - h.astype(F32)
    return out


def _dot(a, b):
    return jnp.dot(a, b, preferred_element_type=F32)


def _dot_nt(a, b):
    return lax.dot_general(a, b, (((1,), (1,)), ((), ())), preferred_element_type=F32)


def _dot_tn(a, b):
    return lax.dot_general(a, b, (((0,), (0,)), ((), ())), preferred_element_type=F32)


def _silu(x):
    return x * jax.nn.sigmoid(x)


def _gelu_tanh(x):
    return 0.5 * x * (1.0 + jnp.tanh(0.7978845608028654 * (x + 0.044715 * (x * x * x))))


def _mm_kernel(x_ref, w_ref, b_ref, o_ref, *, act):
    acc = _dot(x_ref[...].astype(BF16), w_ref[...]) + b_ref[...]
    if act == "sigmoid":
        acc = jax.nn.sigmoid(acc)
    o_ref[...] = acc.astype(o_ref.dtype)


def _matmul(x, w, b=None, act=None, tn=None):
    m, k = x.shape
    n = w.shape[1]
    tm = _row_tile(m)
    tn = n if tn is None else tn
    if b is None:
        b = jnp.zeros((1, n), F32)
    return pl.pallas_call(
        functools.partial(_mm_kernel, act=act),
        grid=(m // tm, n // tn),
        in_specs=[pl.BlockSpec((tm, k), lambda i, j: (i, 0)),
                  pl.BlockSpec((k, tn), lambda i, j: (0, j)),
                  pl.BlockSpec((1, tn), lambda i, j: (0, j))],
        out_specs=pl.BlockSpec((tm, tn), lambda i, j: (i, j)),
        out_shape=jax.ShapeDtypeStruct((m, n), F32),
        compiler_params=_cparams(("parallel", "arbitrary"), 48),
        name="matmul",
    )(x, w, b)


def _mm_res_ln_kernel(y_ref, x_ref, w_ref, g_ref, b_ref, o_ref, *, alpha):
    f = _dot(y_ref[...].astype(BF16), w_ref[...])
    o_ref[...] = _ln(alpha * x_ref[...] + f, g_ref[...], b_ref[...])


def _mm_res_ln(y, x, w, g, b, alpha):
    m, k = y.shape
    d = x.shape[1]
    tm = _row_tile(m, (512, 256, 128))
    return pl.pallas_call(
        functools.partial(_mm_res_ln_kernel, alpha=alpha),
        grid=(m // tm,),
        in_specs=[pl.BlockSpec((tm, k), lambda i: (i, 0)),
                  pl.BlockSpec((tm, d), lambda i: (i, 0)),
                  pl.BlockSpec((k, d), lambda i: (0, 0)),
                  pl.BlockSpec((1, d), lambda i: (0, 0)),
                  pl.BlockSpec((1, d), lambda i: (0, 0))],
        out_specs=pl.BlockSpec((tm, d), lambda i: (i, 0)),
        out_shape=jax.ShapeDtypeStruct((m, d), F32),
        compiler_params=_cparams(("parallel",), 48),
        name="mm_res_ln",
    )(y, x, w, g, b)


def _swiglu_kernel(x_ref, wg_ref, wu_ref, wd_ref, g_ref, b_ref, o_ref, acc_ref, xb_ref, *, alpha):
    f = pl.program_id(1)

    @pl.when(f == 0)
    def _():
        acc_ref[...] = jnp.zeros_like(acc_ref)
        xb_ref[...] = x_ref[...].astype(BF16)

    xb = xb_ref[...]
    h = _silu(_dot(xb, wg_ref[...])) * _dot(xb, wu_ref[...])
    acc_ref[...] += _dot(h.astype(BF16), wd_ref[...])

    @pl.when(f == pl.num_programs(1) - 1)
    def _():
        o_ref[...] = _ln(alpha * x_ref[...] + acc_ref[...], g_ref[...], b_ref[...])


def _ff_tile(dff):
    for c in (512, 256, 128):
        if dff % c == 0:
            return c
    return dff


def _swiglu_res_ln(x, wg, wu, wd, g, b, alpha):
    m, d = x.shape
    dff = wg.shape[1]
    tm = _row_tile(m)
    tf = _ff_tile(dff)
    return pl.pallas_call(
        functools.partial(_swiglu_kernel, alpha=alpha),
        grid=(m // tm, dff // tf),
        in_specs=[pl.BlockSpec((tm, d), lambda i, f: (i, 0)),
                  pl.BlockSpec((d, tf), lambda i, f: (0, f)),
                  pl.BlockSpec((d, tf), lambda i, f: (0, f)),
                  pl.BlockSpec((tf, d), lambda i, f: (f, 0)),
                  pl.BlockSpec((1, d), lambda i, f: (0, 0)),
                  pl.BlockSpec((1, d), lambda i, f: (0, 0))],
        out_specs=pl.BlockSpec((tm, d), lambda i, f: (i, 0)),
        out_shape=jax.ShapeDtypeStruct((m, d), F32),
        scratch_shapes=[pltpu.VMEM((tm, d), F32), pltpu.VMEM((tm, d), BF16)],
        compiler_params=_cparams(("parallel", "arbitrary"), 48),
        name="swiglu_res_ln",
    )(x, wg, wu, wd, g, b)


def _ssd_kernel(xbc_ref, z_ref, dt_ref, conv0_ref, h0_ref, cw_ref, cb_ref, dtb_ref, alog_ref,
                dexp_ref, nw_ref, e_ref, ltri_ref, y_ref, h_ref, xp_scr, dtp_scr, yacc_scr,
                *, lv, n_heads):
    q = SSD_CHUNK
    di = n_heads * SSD_P
    gn = SSD_G * SSD_N
    hpg = n_heads // SSD_G
    gw = hpg * SSD_P
    ci = pl.program_id(1)

    @pl.when(ci == 0)
    def _():
        xp_scr[0:8, :] = conv0_ref[0]
        h_ref[0] = h0_ref[0]

    @pl.when(ci > 0)
    def _():
        xp_scr[0:8, :] = xp_scr[q:q + 8, :]

    if lv < q:
        xp_scr[8 + lv:, :] = jnp.zeros((q - lv, xp_scr.shape[1]), F32)
        dtp_scr[...] = jnp.zeros_like(dtp_scr)
        dtp_scr[0:lv, :] = dt_ref[...]
        dt_raw = dtp_scr[...]
    else:
        dt_raw = dt_ref[...]
    xp_scr[8:8 + lv, :] = xbc_ref[...]

    conv = cb_ref[...]
    for k in range(SSD_CONV):
        conv = conv + cw_ref[k:k + 1, :] * xp_scr[5 + k:5 + k + q, :]
    xc = _silu(conv)
    row = lax.broadcasted_iota(I32, (q, LANES), 0)
    lane = lax.broadcasted_iota(I32, (q, LANES), 1)
    if lv < q:
        xc = jnp.where(lax.broadcasted_iota(I32, xc.shape, 0) < lv, xc, 0.0)

    v = dt_raw + dtb_ref[...]
    dt = jnp.maximum(v, 0.0) + jnp.log1p(jnp.exp(-jnp.abs(v)))
    dt = jnp.where((row < lv) & (lane < n_heads), dt, 0.0)
    adt = dt * (-jnp.exp(alog_ref[...]))
    acum = sum(_dot(ltri_ref[...], t) for t in _split(adt, 3))
    acum_t = acum.T
    dt_t = dt.T
    ea = jnp.exp(acum)
    w = dt * jnp.exp(acum[q - 1:q, :] - acum)
    w_exp = sum(_dot(t, e_ref[...]) for t in _split(w, 2))
    ea_exp = sum(_dot(t, e_ref[...]) for t in _split(ea[:lv], 2))
    cdm = jnp.exp(jnp.broadcast_to(acum_t[:, q - 1:q], (LANES, LANES)))

    xs = xc[:, :di]
    xd = (xs * w_exp).astype(BF16)
    causal = (lax.broadcasted_iota(I32, (lv, q), 0) >= lax.broadcasted_iota(I32, (lv, q), 1))
    lo = lane < SSD_P

    for g in range(SSD_G):
        bg = xc[:, di + g * SSD_N:di + (g + 1) * SSD_N].astype(BF16)
        cg = xc[:lv, di + gn + g * SSD_N:di + gn + (g + 1) * SSD_N].astype(BF16)
        cbm = _dot_nt(cg, bg)
        hg = h_ref[0, g * gw:(g + 1) * gw, :].astype(BF16)
        yoff = _dot_nt(cg, hg)
        st = _dot_tn(xd[:, g * gw:(g + 1) * gw], bg)
        for pr in range(hpg // 2):
            col = g * gw + pr * LANES
            ms = []
            for hh in (2 * pr, 2 * pr + 1):
                h = g * hpg + hh
                seg = acum[:lv, h:h + 1] - acum_t[h:h + 1, :]
                dec = jnp.exp(jnp.where(causal, seg, -jnp.inf))
                ms.append((cbm * dec * dt_t[h:h + 1, :]).astype(BF16))
            xpair = xs[:, col:col + LANES]
            rhs = jnp.concatenate([jnp.where(lo, xpair, 0.0), jnp.where(lo, 0.0, xpair)], axis=0).astype(BF16)
            yd = _dot(jnp.concatenate(ms, axis=1), rhs)
            yacc_scr[:, col:col + LANES] = (yd + yoff[:, pr * LANES:(pr + 1) * LANES] * ea_exp[:, col:col + LANES]
                                            + dexp_ref[:, col:col + LANES] * xs[:lv, col:col + LANES])
        for hh in range(hpg):
            h = g * hpg + hh
            r0 = h * SSD_P
            h_ref[0, r0:r0 + SSD_P, :] = (h_ref[0, r0:r0 + SSD_P, :] * cdm[h:h + 1, :]
                                          + st[hh * SSD_P:(hh + 1) * SSD_P, :])

    y = yacc_scr[...] * _silu(z_ref[...])
    ng = di // SSD_G
    for g in range(SSD_G):
        yg = y[:, g * ng:(g + 1) * ng]
        ms_ = jnp.mean(yg * yg, -1, keepdims=True)
        yn = yg * lax.rsqrt(ms_ + LN_EPS) * nw_ref[:, g * ng:(g + 1) * ng]
        y_ref[:, g * ng:(g + 1) * ng] = yn.astype(y_ref.dtype)


def _ssd_scan(xbc, z, dt, conv0, h0, cw, cb, dtb, alog, dexp, nw, e, ltri, nb, lv):
    m, cdim = xbc.shape
    di = z.shape[1]
    n_heads = di // SSD_P
    nc = m // (nb * lv)
    y_dtype = BF16 if lv % (2 * SUBLANES) == 0 else F32
    return pl.pallas_call(
        functools.partial(_ssd_kernel, lv=lv, n_heads=n_heads),
        grid=(nb, nc),
        in_specs=[pl.BlockSpec((lv, cdim), lambda b, c: (b * nc + c, 0)),
                  pl.BlockSpec((lv, di), lambda b, c: (b * nc + c, 0)),
                  pl.BlockSpec((lv, LANES), lambda b, c: (b * nc + c, 0)),
                  pl.BlockSpec((1, 8, cdim), lambda b, c: (b, 0, 0)),
                  pl.BlockSpec((1, di, SSD_N), lambda b, c: (b, 0, 0)),
                  pl.BlockSpec((SSD_CONV, cdim), lambda b, c: (0, 0)),
                  pl.BlockSpec((1, cdim), lambda b, c: (0, 0)),
                  pl.BlockSpec((1, LANES), lambda b, c: (0, 0)),
                  pl.BlockSpec((1, LANES), lambda b, c: (0, 0)),
                  pl.BlockSpec((1, di), lambda b, c: (0, 0)),
                  pl.BlockSpec((1, di), lambda b, c: (0, 0)),
                  pl.BlockSpec((LANES, di), lambda b, c: (0, 0)),
                  pl.BlockSpec((SSD_CHUNK, SSD_CHUNK), lambda b, c: (0, 0))],
        out_specs=[pl.BlockSpec((lv, di), lambda b, c: (b * nc + c, 0)),
                   pl.BlockSpec((1, di, SSD_N), lambda b, c: (b, 0, 0))],
        out_shape=[jax.ShapeDtypeStruct((m, di), y_dtype),
                   jax.ShapeDtypeStruct((nb, di, SSD_N), F32)],
        scratch_shapes=[pltpu.VMEM((SSD_CHUNK + 8, cdim), F32),
                        pltpu.VMEM((SSD_CHUNK, LANES), F32),
                        pltpu.VMEM((lv, di), F32)],
        compiler_params=_cparams(("parallel", "arbitrary"), 56),
        name="ssd_scan",
    )(xbc, z, dt, conv0, h0, cw, cb, dtb, alog, dexp, nw, e, ltri)


def _ssd_mixer(x, nb, lv, conv0, h0, prm):
    w_in, conv_w, conv_b, dt_bias, a_log, d_skip, norm_w = prm
    di = norm_w.shape[0]
    cdim = conv_w.shape[1]
    n_heads = di // SSD_P
    seq = x.shape[0] // nb
    w_z = w_in[:, :di].astype(BF16)
    w_x = w_in[:, di:di + cdim].astype(BF16)
    w_dt = jnp.pad(w_in[:, di + cdim:], ((0, 0), (0, LANES - n_heads))).astype(BF16)
    z = _matmul(x, w_z, tn=1024)
    xbc = _matmul(x, w_x, tn=1024)
    dt = _matmul(x, w_dt)
    pad1 = lambda a: jnp.pad(a.astype(F32), (0, LANES - n_heads))[None, :]
    e = (jnp.arange(di)[None, :] // SSD_P == jnp.arange(LANES)[:, None]).astype(BF16)
    ltri = (jnp.arange(SSD_CHUNK)[:, None] >= jnp.arange(SSD_CHUNK)[None, :]).astype(BF16)
    y, h_new = _ssd_scan(xbc, z, dt, conv0, h0.reshape(nb, di, SSD_N), conv_w, conv_b[None, :],
                         pad1(dt_bias), pad1(a_log), jnp.repeat(d_skip, SSD_P)[None, :], norm_w[None, :],
                         e, ltri, nb, lv)
    conv_new = xbc.reshape(nb, seq, cdim)[:, seq - (SSD_CONV - 1):]
    return y, conv_new, h_new.reshape(nb, n_heads, SSD_P, SSD_N)


def _gm_in_kernel(x_ref, w_ref, b_ref, g_ref, bb_ref, u_ref, v_ref):
    h = _gelu_tanh(_dot(x_ref[...].astype(BF16), w_ref[...]) + b_ref[...])
    d = u_ref.shape[1]
    u_ref[...] = h[:, :d]
    v_ref[...] = _ln(h[:, d:], g_ref[...], bb_ref[...])


def _gm_in(x, w, b, g, bb):
    m, k = x.shape
    d = w.shape[1] // 2
    tm = _row_tile(m, (512, 256))
    return pl.pallas_call(
        _gm_in_kernel,
        grid=(m // tm,),
        in_specs=[pl.BlockSpec((tm, k), lambda i: (i, 0)),
                  pl.BlockSpec((k, 2 * d), lambda i: (0, 0)),
                  pl.BlockSpec((1, 2 * d), lambda i: (0, 0)),
                  pl.BlockSpec((1, d), lambda i: (0, 0)),
                  pl.BlockSpec((1, d), lambda i: (0, 0))],
        out_specs=[pl.BlockSpec((tm, d), lambda i: (i, 0)), pl.BlockSpec((tm, d), lambda i: (i, 0))],
        out_shape=[jax.ShapeDtypeStruct((m, d), F32), jax.ShapeDtypeStruct((m, d), F32)],
        compiler_params=_cparams(("parallel",), 48),
        name="gm_in",
    )(x, w, b, g, bb)


def _gm_out_kernel(u_ref, v_ref, x_ref, s_ref, sb_ref, w_ref, g_ref, b_ref, o_ref, gated_scr, *, alpha, r):
    tb = u_ref.shape[0]
    gd = u_ref.shape[1] // GM_G
    for s in range(tb // r):
        rows = slice(s * r, (s + 1) * r)
        for g in range(GM_G):
            cols = slice(g * gd, (g + 1) * gd)
            mixed = _dot(s_ref[g], v_ref[rows, cols].astype(BF16)) + sb_ref[rows, cols]
            gated_scr[rows, cols] = (u_ref[rows, cols] * mixed).astype(BF16)
    f = _dot(gated_scr[...], w_ref[...])
    o_ref[...] = _ln(alpha * x_ref[...] + f, g_ref[...], b_ref[...])


def _gm_out(u, v, x, smat, sbias, w, g, b, alpha, tb):
    m, d = u.shape
    r = smat.shape[1]
    return pl.pallas_call(
        functools.partial(_gm_out_kernel, alpha=alpha, r=r),
        grid=(m // tb,),
        in_specs=[pl.BlockSpec((tb, d), lambda i: (i, 0)),
                  pl.BlockSpec((tb, d), lambda i: (i, 0)),
                  pl.BlockSpec((tb, d), lambda i: (i, 0)),
                  pl.BlockSpec((GM_G, r, r), lambda i: (0, 0, 0)),
                  pl.BlockSpec((tb, d), lambda i: (0, 0)),
                  pl.BlockSpec((d, d), lambda i: (0, 0)),
                  pl.BlockSpec((1, d), lambda i: (0, 0)),
                  pl.BlockSpec((1, d), lambda i: (0, 0))],
        out_specs=pl.BlockSpec((tb, d), lambda i: (i, 0)),
        out_shape=jax.ShapeDtypeStruct((m, d), F32),
        scratch_shapes=[pltpu.VMEM((tb, d), BF16)],
        compiler_params=_cparams(("parallel",), 48),
        name="gm_out",
    )(u, v, x, smat, sbias, w, g, b)


def _gmlp_mixer(x, nb, seq, prm, ln_g, ln_b, alpha):
    w_in, b_in, g_in, bb_in, w_s, b_s, w_out = prm
    d = w_out.shape[0]
    gd = d // GM_G
    u, v = _gm_in(x, w_in.astype(BF16), b_in[None, :], g_in[None, :], bb_in[None, :])
    ws = jnp.tril(w_s)
    if seq % GM_CHUNK == 0:
        tb = 512
        smat = ws.astype(BF16)
        bias_rows = jnp.repeat(b_s.T, gd, axis=1)
        sbias = jnp.tile(bias_rows, (tb // GM_CHUNK, 1))
    else:
        tb = nb * seq
        smat = jnp.einsum("ab,gts->gatbs", jnp.eye(nb, dtype=F32), ws[:, :seq, :seq]).reshape(GM_G, tb, tb).astype(BF16)
        sbias = jnp.tile(jnp.repeat(b_s.T[:seq], gd, axis=1), (nb, 1))
    out = _gm_out(u, v, x, smat, sbias, w_out.astype(BF16), ln_g, ln_b, alpha, tb)
    return out, v


def _router_kernel(x_ref, wh_ref, wl_ref, b_ref, eid_ref, gate_ref):
    x = x_ref[...]
    xh, xl = _split(x, 2)
    logits = _dot(xh, wh_ref[...]) + _dot(xl, wh_ref[...]) + _dot(xh, wl_ref[...]) + b_ref[...]
    lane = lax.broadcasted_iota(I32, logits.shape, 1)
    m1 = jnp.max(logits, -1, keepdims=True)
    i1 = jnp.min(jnp.where(logits == m1, lane, LANES), -1, keepdims=True)
    rest = jnp.where(lane == i1, NEG * 2, logits)
    m2 = jnp.max(rest, -1, keepdims=True)
    i2 = jnp.min(jnp.where(rest == m2, lane, LANES), -1, keepdims=True)
    e = jnp.exp(m2 - m1)
    g1 = 1.0 / (1.0 + e)
    eid_ref[...] = jnp.where(lane == 0, i1, jnp.where(lane == 1, i2, 0))
    gate_ref[...] = jnp.where(lane == 0, g1, jnp.where(lane == 1, e * g1, 0.0))


def _router(x, w_router, b_router):
    m, d = x.shape
    ne = w_router.shape[1]
    w = jnp.pad(w_router, ((0, 0), (0, LANES - ne)))
    wh = w.astype(BF16)
    wl = (w - wh.astype(F32)).astype(BF16)
    b = jnp.pad(b_router.astype(F32), (0, LANES - ne), constant_values=NEG)[None, :]
    tm = _row_tile(m, (512, 256))
    return pl.pallas_call(
        _router_kernel,
        grid=(m // tm,),
        in_specs=[pl.BlockSpec((tm, d), lambda i: (i, 0)),
                  pl.BlockSpec((d, LANES), lambda i: (0, 0)),
                  pl.BlockSpec((d, LANES), lambda i: (0, 0)),
                  pl.BlockSpec((1, LANES), lambda i: (0, 0))],
        out_specs=[pl.BlockSpec((tm, LANES), lambda i: (i, 0)), pl.BlockSpec((tm, LANES), lambda i: (i, 0))],
        out_shape=[jax.ShapeDtypeStruct((m, LANES), I32), jax.ShapeDtypeStruct((m, LANES), F32)],
        compiler_params=_cparams(("parallel",), 32),
        name="router",
    )(x, wh, wl, b)


def _row_copy(src_hbm, dst, s_row, d_row, sem):
    return pltpu.make_async_copy(src_hbm.at[pl.ds(s_row, 1), :], dst.at[pl.ds(d_row, 1), :], sem)


def _row_gather(src_hbm, dst, idx_ref, n, per_row, sem, start):
    if not start:
        for k in range(per_row):
            pltpu.make_async_copy(src_hbm.at[pl.ds(0, n), :], dst.at[k], sem).wait()
        return

    def body(j, c):
        for k in range(per_row):
            _row_copy(src_hbm, dst.at[k], idx_ref[0, 0, per_row * j + k], j, sem).start()
        return c

    lax.fori_loop(0, n, body, 0, unroll=8)


def _moe_ffn_kernel(be_ref, nu_ref, idx_ref, idxn_ref, x_hbm, wg_ref, wu_ref, wd_ref, o_ref,
                    acc_ref, xg_ref, xb_ref, sem, *, tb, rps):
    i = pl.program_id(0)
    f = pl.program_id(1)
    last = pl.num_programs(1) - 1
    used = i < nu_ref[0]
    slot = i % 2
    tbp = xg_ref.shape[2]

    def start_next_rows():
        base = f * rps
        for j in range(rps):
            _row_copy(x_hbm, xg_ref.at[1 - slot, 0], idxn_ref[0, 0, base + j], base + j, sem.at[1 - slot]).start()

    @pl.when((f == 0) & (i == 0))
    def _():
        _row_gather(x_hbm, xg_ref.at[0], idx_ref, tbp, 1, sem.at[0], True)

    @pl.when(f == 0)
    def _():
        _row_gather(x_hbm, xg_ref.at[slot], idx_ref, tbp, 1, sem.at[slot], False)

    @pl.when(used & (f == 0))
    def _():
        acc_ref[...] = jnp.zeros_like(acc_ref)
        xb_ref[...] = xg_ref[slot, 0, 0:tb, :].astype(BF16)

    @pl.when(used)
    def _():
        start_next_rows()
        xb = xb_ref[...]
        h = _silu(_dot(xb, wg_ref[0])) * _dot(xb, wu_ref[0])
        acc_ref[...] += _dot(h.astype(BF16), wd_ref[0])

    @pl.when(jnp.logical_not(used))
    def _():
        start_next_rows()

    @pl.when(used & (f == last))
    def _():
        o_ref[...] = acc_ref[...]

    @pl.when(jnp.logical_not(used) & (f == last))
    def _():
        o_ref[...] = jnp.zeros_like(o_ref)

    @pl.when((i == pl.num_programs(0) - 1) & (f == last))
    def _():
        _row_gather(x_hbm, xg_ref.at[1 - slot], idx_ref, tbp, 1, sem.at[1 - slot], False)


def _moe_ffn(x, buf_tok, blk_e, n_used, wg, wu, wd, tb):
    d = x.shape[1]
    nblk = buf_tok.shape[0] // tb
    dff = wg.shape[2]
    tf = _ff_tile(dff)
    nf = dff // tf
    rps = -(-tb // (nf * SUBLANES)) * SUBLANES
    tbp = rps * nf

    def fe(i, f, be, nu):
        return jnp.where(i < nu[0], f, nf - 1)

    grid_spec = pltpu.PrefetchScalarGridSpec(
        num_scalar_prefetch=2,
        grid=(nblk, nf),
        in_specs=[pl.BlockSpec((1, 1, tbp), lambda i, f, be, nu: (i, 0, 0), memory_space=pltpu.SMEM),
                  pl.BlockSpec((1, 1, tbp), lambda i, f, be, nu: (jnp.minimum(i + 1, nblk - 1), 0, 0),
                               memory_space=pltpu.SMEM),
                  pl.BlockSpec(memory_space=pl.ANY),
                  pl.BlockSpec((1, d, tf), lambda i, f, be, nu: (be[i], 0, fe(i, f, be, nu))),
                  pl.BlockSpec((1, d, tf), lambda i, f, be, nu: (be[i], 0, fe(i, f, be, nu))),
                  pl.BlockSpec((1, tf, d), lambda i, f, be, nu: (be[i], fe(i, f, be, nu), 0))],
        out_specs=pl.BlockSpec((tb, d), lambda i, f, be, nu: (i, 0)),
        scratch_shapes=[pltpu.VMEM((tb, d), F32), pltpu.VMEM((2, 1, tbp, d), F32), pltpu.VMEM((tb, d), BF16),
                        pltpu.SemaphoreType.DMA((2,))])
    idx = jnp.pad(buf_tok.reshape(nblk, 1, tb), ((0, 0), (0, 0), (0, tbp - tb)))
    return pl.pallas_call(
        functools.partial(_moe_ffn_kernel, tb=tb, rps=rps),
        grid_spec=grid_spec,
        out_shape=jax.ShapeDtypeStruct((nblk * tb, d), F32),
        compiler_params=_cparams(("arbitrary", "arbitrary"), 56),
        name="moe_ffn",
    )(blk_e, n_used, idx, idx, x, wg, wu, wd)


def _combine_kernel(idx_ref, idxn_ref, yb_hbm, x_ref, gate_ref, g_ref, b_ref, o_ref, ybuf, sem, *, alpha, tm):
    i = pl.program_id(0)
    slot = i % 2

    @pl.when(i == 0)
    def _():
        _row_gather(yb_hbm, ybuf.at[0], idx_ref, tm, TOP_K, sem.at[0], True)

    @pl.when(i + 1 < pl.num_programs(0))
    def _():
        _row_gather(yb_hbm, ybuf.at[1 - slot], idxn_ref, tm, TOP_K, sem.at[1 - slot], True)

    _row_gather(yb_hbm, ybuf.at[slot], idx_ref, tm, TOP_K, sem.at[slot], False)
    gate = gate_ref[...]
    y = gate[:, 0:1] * ybuf[slot, 0] + gate[:, 1:2] * ybuf[slot, 1]
    o_ref[...] = _ln(alpha * x_ref[...] + y, g_ref[...], b_ref[...])


def _moe_combine(yb, dest, x, gate, g, b, alpha):
    m, d = x.shape
    tm = _row_tile(m, (256, 128))
    nblk = m // tm
    idx = dest.reshape(nblk, 1, TOP_K * tm)
    return pl.pallas_call(
        functools.partial(_combine_kernel, alpha=alpha, tm=tm),
        grid=(nblk,),
        in_specs=[pl.BlockSpec((1, 1, TOP_K * tm), lambda i: (i, 0, 0), memory_space=pltpu.SMEM),
                  pl.BlockSpec((1, 1, TOP_K * tm), lambda i: (jnp.minimum(i + 1, nblk - 1), 0, 0),
                               memory_space=pltpu.SMEM),
                  pl.BlockSpec(memory_space=pl.ANY),
                  pl.BlockSpec((tm, d), lambda i: (i, 0)),
                  pl.BlockSpec((tm, LANES), lambda i: (i, 0)),
                  pl.BlockSpec((1, d), lambda i: (0, 0)),
                  pl.BlockSpec((1, d), lambda i: (0, 0))],
        out_specs=pl.BlockSpec((tm, d), lambda i: (i, 0)),
        out_shape=jax.ShapeDtypeStruct((m, d), F32),
        scratch_shapes=[pltpu.VMEM((2, TOP_K, tm, d), F32), pltpu.SemaphoreType.DMA((2,))],
        compiler_params=_cparams(("arbitrary",), 32),
        name="moe_combine",
    )(idx, idx, yb, x, gate, g, b)


def _moe_res_ln(x, w_router, b_router, wg, wu, wd, g, b, alpha):
    m, d = x.shape
    ne = w_router.shape[1]
    eid, gate = _router(x, w_router, b_router)
    tk = m * TOP_K
    tb = 1024 if tk >= 16384 else 256
    ef = eid[:, :TOP_K].reshape(-1)
    onehot = (ef[:, None] == jnp.arange(ne, dtype=I32)[None, :]).astype(I32)
    csum = jnp.cumsum(onehot, axis=0)
    rank = jnp.sum(csum * onehot, axis=1) - 1
    counts = csum[-1]
    padded = (counts + tb - 1) // tb * tb
    pend = jnp.cumsum(padded)
    dest = (pend - padded)[ef] + rank
    nblk = -(-tk // tb) + ne
    buf_tok = jnp.zeros((nblk * tb,), I32).at[dest].set(jnp.arange(tk, dtype=I32) // TOP_K)
    blk_start = jnp.arange(nblk, dtype=I32) * tb
    blk_e = jnp.minimum(jnp.sum((pend[None, :] <= blk_start[:, None]).astype(I32), axis=1), ne - 1).astype(I32)
    n_used = (pend[-1:] // tb).astype(I32)
    yb = _moe_ffn(x, buf_tok, blk_e, n_used, wg, wu, wd, tb)
    return _moe_combine(yb, dest.astype(I32), x, gate, g, b, alpha)


PAGE = 128


def _compress_kernel(pt_ref, pages_hbm, w1_ref, pos_ref, b1_ref, w2_ref, b2_ref, o_ref, scr, acc_scr, cv_scr, sem,
                     *stage, npg, col0):
    b = pl.program_id(0)
    pw = NSA_KV * NSA_DH
    nsub = npg * PAGE // CMP_STRIDE
    hid = b1_ref.shape[2]

    nslab = 2 * pw // LANES

    def position_term():
        for c in range(2):
            cv = jnp.zeros((SUBLANES, hid), F32) + b1_ref[c]
            for o in range(CMP_STRIDE):
                cv = cv + _dot(pos_ref[c, o].astype(BF16), w1_ref[c, o])[:, :hid]
                cv = cv + _dot(pos_ref[c, CMP_STRIDE + o].astype(BF16), w1_ref[c, o])[:, hid:]
            cv_scr[c] = cv

    if not stage:
        def start(p, c_):
            row = pl.multiple_of(p * PAGE, PAGE)
            for sl in range(nslab):
                pltpu.make_async_copy(pages_hbm.at[pt_ref[b, p], :, pl.ds(col0 + sl * LANES, LANES)],
                                      scr.at[sl, pl.ds(row, PAGE), :], sem.at[0]).start()
            return c_

        lax.fori_loop(0, npg, start, 0)
        pl.when(b == 0)(position_term)
        for sl in range(nslab):
            pltpu.make_async_copy(scr.at[sl], scr.at[sl], sem.at[0]).wait()
    else:
        stg = stage[0]
        cpg = stg.shape[2]
        nch = npg // cpg

        def start_chunk(bb, k, slot):
            def body(p, c_):
                for c in range(2):
                    pltpu.make_async_copy(pages_hbm.at[pt_ref[bb, k * cpg + p], c], stg.at[slot, c, p],
                                          sem.at[slot]).start()
                return c_

            lax.fori_loop(0, cpg, body, 0)

        @pl.when(b == 0)
        def _():
            start_chunk(b, 0, 0)
            position_term()

        for k in range(nch):
            slot = k % 2
            if k + 1 < nch:
                start_chunk(b, k + 1, 1 - slot)
            else:
                pl.when(b + 1 < pl.num_programs(0))(functools.partial(start_chunk, b + 1, 0, (k + 1) % 2))
            for c in range(2):
                pltpu.make_async_copy(stg.at[slot, c], stg.at[slot, c], sem.at[slot]).wait()

            def to_token_major(p, c_):
                row = pl.multiple_of((k * cpg + p) * PAGE, PAGE)
                for sl in range(nslab):
                    c, half = divmod(sl, nslab // 2)
                    scr[sl, pl.ds(row, PAGE), :] = stg[slot, c, p, half * LANES:(half + 1) * LANES, :].T
                return c_

            lax.fori_loop(0, cpg, to_token_major, 0)

    gps = LANES // NSA_DH
    for c in range(2):
        acc_scr[...] = jnp.zeros_like(acc_scr)
        for o in range(CMP_STRIDE):
            for sl in range(nslab // 2):
                xo = scr[c * (nslab // 2) + sl, pl.ds(o, nsub, stride=CMP_STRIDE), :].astype(BF16)
                for gg in range(gps):
                    acc_scr[sl * gps + gg] += _dot(xo[:, gg * NSA_DH:(gg + 1) * NSA_DH], w1_ref[c, o])
        for g in range(NSA_KV):
            acc = acc_scr[g]
            pre = acc[:, :hid] + pltpu.roll(acc[:, hid:], nsub - 1, 0) + cv_scr[c, 0:1, :]
            o_ref[0, c, g] = _dot(_gelu_tanh(pre).astype(BF16), w2_ref[c]) + b2_ref[c]


def _compress(pages, page_table, pos, w1, b1, w2, b2, col0=0, token_minor=False):
    nb, npg = page_table.shape
    nsub = npg * PAGE // CMP_STRIDE
    hid = w1.shape[-1]
    pw = NSA_KV * NSA_DH
    w1cat = jnp.concatenate([w1[:, :CMP_STRIDE], w1[:, CMP_STRIDE:]], axis=-1).astype(BF16)
    posb = jnp.broadcast_to(pos[:, :, None, :], (2, CMP_BLK, SUBLANES, NSA_DH))
    chunk = 16
    assert not token_minor or (npg % chunk == 0 and (npg // chunk) % 2 == 0)
    grid_spec = pltpu.PrefetchScalarGridSpec(
        num_scalar_prefetch=1,
        grid=(nb,),
        in_specs=[pl.BlockSpec(memory_space=pl.ANY),
                  pl.BlockSpec((2, CMP_STRIDE, NSA_DH, 2 * hid), lambda b, pt: (0, 0, 0, 0)),
                  pl.BlockSpec((2, CMP_BLK, SUBLANES, NSA_DH), lambda b, pt: (0, 0, 0, 0)),
                  pl.BlockSpec((2, 1, hid), lambda b, pt: (0, 0, 0)),
                  pl.BlockSpec((2, hid, NSA_DH), lambda b, pt: (0, 0, 0)),
                  pl.BlockSpec((2, 1, NSA_DH), lambda b, pt: (0, 0, 0))],
        out_specs=pl.BlockSpec((1, 2, NSA_KV, nsub, NSA_DH), lambda b, pt: (b, 0, 0, 0, 0)),
        scratch_shapes=[pltpu.VMEM((2 * pw // LANES, npg * PAGE, LANES), F32),
                        pltpu.VMEM((NSA_KV, nsub, 2 * hid), F32),
                        pltpu.VMEM((2, SUBLANES, hid), F32),
                        pltpu.SemaphoreType.DMA((2,))]
                       + ([pltpu.VMEM((2, 2, chunk, pw, PAGE), F32)] if token_minor else []))
    return pl.pallas_call(
        functools.partial(_compress_kernel, npg=npg, col0=col0),
        grid_spec=grid_spec,
        out_shape=jax.ShapeDtypeStruct((nb, 2, NSA_KV, nsub, NSA_DH), F32),
        compiler_params=_cparams(("arbitrary",), 56),
        name="nsa_compress",
    )(page_table, pages, w1cat, posb, b1[:, None, :], w2.astype(BF16), b2[:, None, :])


def _masked_softmax(s, valid):
    s = jnp.where(valid, s, NEG)
    p = jnp.exp(s - jnp.max(s, -1, keepdims=True))
    return p / jnp.sum(p, -1, keepdims=True)


def _select_blocks(imp, ovl, tq, nbs):
    score = sum(_dot(t, ovl) for t in _split(imp, 3))
    jb = lax.broadcasted_iota(I32, score.shape, 1)
    ok = jb * SEL_BLK <= tq
    cur = tq // SEL_BLK
    forced = (jb == 0) | (jb == cur) | (jb == cur - 1)
    score = jnp.where(ok, score + jnp.where(forced, FORCE, 0.0), NEG)
    rank = jnp.zeros(score.shape, F32)
    for j2 in range(nbs):
        cj = score[:, j2:j2 + 1]
        beats = (cj > score) | ((cj == score) & (j2 < jb))
        rank = rank + jnp.where(beats, 1.0, 0.0)
    return jnp.where(ok & (rank < N_SEL), 1.0, 0.0)


def _expand_sel(sel, first_blk, nkeys):
    jj = lax.broadcasted_iota(I32, (sel.shape[1], nkeys), 0)
    kk = lax.broadcasted_iota(I32, (sel.shape[1], nkeys), 1)
    expand = jnp.where(kk // SEL_BLK + first_blk == jj, 1.0, 0.0).astype(BF16)
    return _dot(sel.astype(BF16), expand)


POS_SPLIT = 64
Q_EXTRA = 16


PROMPT_KT = 512


def _nsa_prompt_kernel(q_ref, gate_ref, slope_ref, kc_ref, vc_ref, ks_ref, vs_ref, kw_ref, vw_ref, ovl_ref, exp_ref,
                       o_ref, *, nbs, qblk, nrep):
    qb = pl.program_id(2)
    cols = qblk * nrep
    q = q_ref[0, 0, 0]
    q16 = q.astype(BF16)
    slope = slope_ref[0][0:1, :]
    tq = qb * qblk + lax.broadcasted_iota(I32, (1, cols), 1) % qblk
    tq_row = tq[:, 0:qblk]

    kc = kc_ref[0, 0]
    qh, ql = _split(q[:NSA_DH, :], 2)
    kh, kl = _split(kc, 2)
    s = _dot(kh, qh) + _dot(kh, ql) + _dot(kl, qh)
    e = lax.broadcasted_iota(I32, (kc.shape[0], 1), 0) * CMP_STRIDE + (CMP_BLK - 1)
    valid = e <= tq
    s = jnp.where(valid, s + slope * e.astype(F32), NEG)
    p = jnp.exp(s - jnp.max(s, 0, keepdims=True))
    p = jnp.where(valid, p / jnp.sum(p, 0, keepdims=True), 0.0)
    o_cmp = _dot(vc_ref[0, 0].astype(BF16), p.astype(BF16))
    imp = p[:, 0:qblk]
    for r in range(1, nrep):
        imp = imp + p[:, r * qblk:(r + 1) * qblk]

    score = sum(_dot(ovl_ref[...], t) for t in _split(imp, 3))
    jb = lax.broadcasted_iota(I32, score.shape, 0)
    ok = jb * SEL_BLK <= tq_row
    cur = tq_row // SEL_BLK
    forced = (jb == 0) | (jb == cur) | (jb == cur - 1)
    score = jnp.where(ok, score + jnp.where(forced, FORCE, 0.0), NEG)
    rank = jnp.zeros(score.shape, F32)
    for j2 in range(nbs):
        cj = score[j2:j2 + 1, :]
        beats = (cj > score) | ((cj == score) & (j2 < jb))
        rank = rank + jnp.where(beats, 1.0, 0.0)
    selneg = jnp.where(ok & (rank < N_SEL), 0.0, NEG).astype(BF16)

    def tile(k_ref, v_ref, bias_fn, kt, carry):
        m, l, acc = carry
        off = pl.multiple_of(kt * PROMPT_KT, PROMPT_KT)
        kpos = off + lax.broadcasted_iota(I32, (PROMPT_KT, 1), 0)
        s_ = _dot(k_ref[0, 0, pl.ds(off, PROMPT_KT), :], q16)
        s_ = s_ + jnp.concatenate([bias_fn(off, kpos)] * nrep, axis=1)
        m_new = jnp.maximum(m, jnp.max(s_, 0, keepdims=True))
        a = jnp.exp(m - m_new)
        p_ = jnp.exp(s_ - m_new)
        v = v_ref[0, 0, :, pl.ds(off, PROMPT_KT)].astype(BF16)
        return m_new, l * a + jnp.sum(p_, 0, keepdims=True), acc * a + _dot(v, p_.astype(BF16))

    def sel_bias(off, kpos):
        return jnp.where(kpos <= tq_row, _dot(exp_ref[pl.ds(off, PROMPT_KT), :], selneg), NEG)

    def win_bias(off, kpos):
        return jnp.where((kpos <= tq_row) & (kpos > tq_row - WINDOW), 0.0, NEG)

    kt_hi = (qb * qblk + qblk - 1) // PROMPT_KT + 1
    kt_win = jnp.maximum(qb * qblk - (WINDOW - 1), 0) // PROMPT_KT
    init = (jnp.full((1, cols), -jnp.inf, F32), jnp.zeros((1, cols), F32), jnp.zeros((NSA_DH, cols), F32))
    c_sel = lax.fori_loop(0, kt_win, lambda kt, c: tile(ks_ref, vs_ref, sel_bias, kt, c), init)
    c_sel, c_win = lax.fori_loop(
        kt_win, kt_hi,
        lambda kt, c: (tile(ks_ref, vs_ref, sel_bias, kt, c[0]), tile(kw_ref, vw_ref, win_bias, kt, c[1])),
        (c_sel, init))
    gate = gate_ref[0, 0, 0]
    o_ref[0, 0, 0] = (gate[0:1, :] * o_cmp + gate[1:2, :] * (c_sel[2] / c_sel[1])
                      + gate[2:3, :] * (c_win[2] / c_win[1]))


def _nsa_prompt(qt, gates, slopes, kc, vct, ksa, kwa, kvt, ovl_t, nbs):
    nb, ng, nqb, qw, cols = qt.shape
    dh = qw - Q_EXTRA
    seq = kvt.shape[3]
    nbc = kc.shape[2]
    qblk = seq // nqb
    nrep = cols // qblk
    qspec = lambda r: pl.BlockSpec((1, 1, 1, r, cols), lambda b, g, i: (b, g, i, 0, 0))
    per_g = lambda r, c: pl.BlockSpec((1, 1, r, c), lambda b, g, i: (b, g, 0, 0))
    tspec = lambda c: pl.BlockSpec((1, 1, dh, seq), lambda b, g, i: (b, c * ng + g, 0, 0))
    expand = (jnp.arange(seq)[:, None] // SEL_BLK == jnp.arange(ovl_t.shape[0])[None, :]).astype(BF16)
    return pl.pallas_call(
        functools.partial(_nsa_prompt_kernel, nbs=nbs, qblk=qblk, nrep=nrep),
        grid=(nb, ng, nqb),
        in_specs=[qspec(qw), qspec(SUBLANES),
                  pl.BlockSpec((1, SUBLANES, cols), lambda b, g, i: (g, 0, 0)),
                  per_g(nbc, dh), per_g(dh, nbc), per_g(seq, qw), tspec(3), per_g(seq, qw), tspec(5),
                  pl.BlockSpec(ovl_t.shape, lambda b, g, i: (0, 0)),
                  pl.BlockSpec(expand.shape, lambda b, g, i: (0, 0))],
        out_specs=qspec(dh),
        out_shape=jax.ShapeDtypeStruct((nb, ng, nqb, dh, cols), F32),
        compiler_params=_cparams(("parallel", "parallel", "arbitrary"), 48),
        name="nsa_prompt",
    )(qt, gates, slopes, kc, vct, ksa, kvt, kwa, kvt, ovl_t, expand)


def _nsa_sample_kernel(pt_ref, q_ref, gate_ref, slope_ref, kc_ref, vc_ref, *refs, nbs, past, nq, nrep, ppt):
    ks_pages, vs_pages = refs[:ppt], refs[ppt:2 * ppt]
    (ksn_ref, vsn_ref, kwp_ref, vwp_ref, kwn_ref, vwn_ref, ovl_ref, o_ref,
     sel_scr, m_scr, l_scr, acc_scr, part_scr) = refs[2 * ppt:]
    keys = ppt * PAGE
    t = pl.program_id(1)
    rows = q_ref.shape[1]
    pw = q_ref.shape[2]
    grows = rows // NSA_KV
    scale = NSA_DH ** -0.5
    q = q_ref[0]
    q16 = q.astype(BF16)
    slope = slope_ref[:, 0:1]
    tq = past + lax.broadcasted_iota(I32, (rows, 1), 0) % nq
    tqf = tq.astype(F32)
    gate = gate_ref[0]

    def logits(qk, kpos):
        return qk * scale - slope * (tqf - kpos.astype(F32))

    @pl.when(t == 0)
    def _():
        kc = kc_ref[0]
        qh, ql = _split(q, 2)
        kh, kl = _split(kc, 2)
        s = (_dot_nt(qh, kh) + _dot_nt(ql, kh) + _dot_nt(qh, kl)) * scale
        e = lax.broadcasted_iota(I32, (1, kc.shape[0]), 1) * CMP_STRIDE + (CMP_BLK - 1)
        valid = e <= tq
        p = jnp.where(valid, _masked_softmax(s - slope * (tqf - e.astype(F32)), valid), 0.0)
        o_cmp = _dot(p.astype(BF16), vc_ref[0].astype(BF16))
        imps = []
        for g in range(NSA_KV):
            a = p[g * grows:g * grows + nq]
            for r in range(1, nrep):
                a = a + p[g * grows + r * nq:g * grows + (r + 1) * nq]
            imps.append(a)
        imp = jnp.concatenate(imps, axis=0)
        tq_s = past + lax.broadcasted_iota(I32, (NSA_KV * nq, 1), 0) % nq
        sel_s = _select_blocks(imp, ovl_ref[...], tq_s, nbs)
        sel = jnp.concatenate([sel_s[g * nq:(g + 1) * nq] for g in range(NSA_KV) for _ in range(nrep)], axis=0)
        sel_scr[...] = sel

        wb = kwp_ref.shape[3]
        npad = LANES - nq
        zpad = jnp.zeros((npad, pw), F32)
        kwn = jnp.concatenate([kwn_ref[0], zpad], axis=0).astype(BF16)
        vwn = jnp.concatenate([vwn_ref[0], zpad], axis=0).astype(BF16)
        qk = jnp.concatenate([_dot(q16, kwp_ref[0, 0].astype(BF16)), _dot_nt(q16, kwn)], axis=1)
        idx = lax.broadcasted_iota(I32, (1, wb + LANES), 1)
        kpos = past - wb + idx
        valid = (idx < wb + nq) & (kpos <= tq) & (kpos > tq - WINDOW)
        pw_ = _masked_softmax(logits(qk, kpos), valid).astype(BF16)
        o_win = _dot_nt(pw_[:, :wb], vwp_ref[0, 0].astype(BF16)) + _dot(pw_[:, wb:], vwn)
        part_scr[...] = gate[:, 0:1] * o_cmp + gate[:, 2:3] * o_win

        kn = jnp.concatenate([ksn_ref[0], zpad], axis=0).astype(BF16)
        vn = jnp.concatenate([vsn_ref[0], zpad], axis=0).astype(BF16)
        idx = lax.broadcasted_iota(I32, (1, LANES), 1)
        kpos = past + idx
        blk = past // SEL_BLK
        valid = (idx < nq) & (kpos <= tq) & (sel[:, blk:blk + 1] > 0.5)
        s = jnp.where(valid, logits(_dot_nt(q16, kn), kpos), NEG)
        m = jnp.max(s, -1, keepdims=True)
        p = jnp.exp(s - m)
        m_scr[...] = m
        l_scr[...] = jnp.sum(p, -1, keepdims=True)
        acc_scr[...] = _dot(p.astype(BF16), vn)

    k = jnp.concatenate([r[0, 0] for r in ks_pages], axis=1).astype(BF16)
    v = jnp.concatenate([r[0, 0] for r in vs_pages], axis=1).astype(BF16)
    kpos = t * keys + lax.broadcasted_iota(I32, (1, keys), 1)
    selk = _expand_sel(sel_scr[...], t * (keys // SEL_BLK), keys)
    s = jnp.where(selk > 0.5, logits(_dot(q16, k), kpos), NEG)
    m = m_scr[...]
    m_new = jnp.maximum(m, jnp.max(s, -1, keepdims=True))
    a = jnp.exp(m - m_new)
    p = jnp.exp(s - m_new)
    m_scr[...] = m_new
    l_scr[...] = l_scr[...] * a + jnp.sum(p, -1, keepdims=True)
    acc_scr[...] = acc_scr[...] * a + _dot_nt(p.astype(BF16), v)

    @pl.when(t == pl.num_programs(1) - 1)
    def _():
        tot = part_scr[...] + gate[:, 1:2] * (acc_scr[...] / l_scr[...])
        o_ref[0] = jnp.concatenate([tot[g * grows:(g + 1) * grows, g * NSA_DH:(g + 1) * NSA_DH]
                                    for g in range(NSA_KV)], axis=0)


def _nsa_sample(qbd, gates, slopes, kc, vc, cache_t, page_table, kv_new, win_t, win_new, ovl, nbs, nq, nrep):
    nb, rows, pw = qbd.shape
    npg = page_table.shape[1]
    past = npg * PAGE
    ppt = max(p for p in (8, 4, 2, 1) if npg % p == 0)
    nt = npg // ppt
    nbc = kc.shape[1]
    wb = win_t.shape[3]
    const = lambda shape: pl.BlockSpec(shape, lambda b, t, pt: (0,) * len(shape))
    per_b = lambda n, w, col=0: pl.BlockSpec((1, n, w), lambda b, t, pt: (b, 0, col))
    page = lambda c, i: pl.BlockSpec((1, 1, pw, PAGE), lambda b, t, pt: (pt[b, ppt * t + i], c, 0, 0))
    wspec = lambda c: pl.BlockSpec((1, 1, pw, wb), lambda b, t, pt: (b, c, 0, 0))
    grid_spec = pltpu.PrefetchScalarGridSpec(
        num_scalar_prefetch=1,
        grid=(nb, nt),
        in_specs=[per_b(rows, pw), per_b(rows, SUBLANES), const((rows, SUBLANES)), per_b(nbc, pw), per_b(nbc, pw)]
                 + [page(2, i) for i in range(ppt)] + [page(3, i) for i in range(ppt)]
                 + [per_b(nq, pw, 2), per_b(nq, pw, 3), wspec(0), wspec(1),
                    per_b(nq, pw, 0), per_b(nq, pw, 1), const(ovl.shape)],
        out_specs=pl.BlockSpec((1, rows, NSA_DH), lambda b, t, pt: (b, 0, 0)),
        scratch_shapes=[pltpu.VMEM((rows, ovl.shape[1]), F32), pltpu.VMEM((rows, 1), F32), pltpu.VMEM((rows, 1), F32),
                        pltpu.VMEM((rows, pw), F32), pltpu.VMEM((rows, pw), F32)])
    return pl.pallas_call(
        functools.partial(_nsa_sample_kernel, nbs=nbs, past=past, nq=nq, nrep=nrep, ppt=ppt),
        grid_spec=grid_spec,
        out_shape=jax.ShapeDtypeStruct((nb, rows, NSA_DH), F32),
        compiler_params=_cparams(("parallel", "arbitrary"), 48),
        name="nsa_sample",
    )(page_table, qbd, gates, slopes, kc, vc, *([cache_t] * (2 * ppt)), kv_new, kv_new, win_t, win_t,
      win_new, win_new, ovl)


def _overlap(nbc, nbs_pad):
    ci = jnp.arange(nbc)[:, None] * CMP_STRIDE
    sj = jnp.arange(nbs_pad)[None, :] * SEL_BLK
    return ((ci < sj + SEL_BLK) & (ci + CMP_BLK > sj)).astype(BF16)


def _alibi_slopes(nh):
    return jnp.exp2(-8.0 * (jnp.arange(nh, dtype=F32) + 1.0) / nh)


def _nsa_in(x, w_in, nh):
    qw = nh * NSA_DH
    kvw = 6 * NSA_KV * NSA_DH
    proj = _matmul(x, w_in[:, :qw + kvw].astype(BF16), tn=512)
    ng = 3 * nh
    gates = _matmul(x, jnp.pad(w_in[:, qw + kvw:], ((0, 0), (0, LANES - ng))).astype(BF16), act="sigmoid")
    kv4w = 4 * NSA_KV * NSA_DH
    return proj[:, :qw], proj[:, qw:qw + kv4w], proj[:, qw + kv4w:], gates[:, :ng]


def _mm_t_kernel(w_ref, x_ref, o_ref):
    o_ref[0] = _dot_nt(w_ref[...], x_ref[...].astype(BF16))


def _matmul_t(x, wt, nb):
    m, k = x.shape
    n = wt.shape[0]
    seq = m // nb
    tl = _row_tile(seq, (1024, 512, 256, 128))
    tn = _row_tile(n, (512, 256, 128))
    return pl.pallas_call(
        _mm_t_kernel,
        grid=(nb, seq // tl, n // tn),
        in_specs=[pl.BlockSpec((tn, k), lambda b, l, j: (j, 0)),
                  pl.BlockSpec((tl, k), lambda b, l, j: (b * (seq // tl) + l, 0))],
        out_specs=pl.BlockSpec((1, tn, tl), lambda b, l, j: (b, j, l)),
        out_shape=jax.ShapeDtypeStruct((nb, n, seq), F32),
        compiler_params=_cparams(("parallel", "parallel", "arbitrary"), 48),
        name="matmul_t",
    )(wt, x)


def _nsa_prompt_mixer(x, nb, seq, prm):
    w_in, pos, w1, b1, w2, b2 = prm
    nh = (w_in.shape[1] - 6 * NSA_KV * NSA_DH) // (NSA_DH + 3)
    nrep = nh // NSA_KV
    pw = NSA_KV * NSA_DH
    qw = nh * NSA_DH
    qblk = 128
    nqb = seq // qblk
    rows = qblk * nrep
    w_tok = jnp.concatenate([w_in[:, :qw + 3 * pw], w_in[:, qw + 4 * pw:qw + 5 * pw]], axis=1)
    proj = _matmul(x, w_tok.astype(BF16), tn=512)
    kvt = _matmul_t(x, w_in[:, qw:qw + 6 * pw].T.astype(BF16), nb)
    ng = 3 * nh
    gates = _matmul(x, jnp.pad(w_in[:, qw + 6 * pw:], ((0, 0), (0, LANES - ng))).astype(BF16), act="sigmoid")[:, :ng]
    npg = seq // PAGE
    cmp_out = _compress(proj.reshape(nb * npg, PAGE, proj.shape[1]), jnp.arange(nb * npg, dtype=I32).reshape(nb, npg),
                        pos, w1, b1, w2, b2, col0=qw)
    q = proj[:, :qw] * (NSA_DH ** -0.5)
    qt = q.reshape(nb, nqb, qblk, NSA_KV, nrep, NSA_DH).transpose(0, 3, 1, 5, 4, 2).reshape(nb, NSA_KV, nqb, NSA_DH, rows)
    slf = jnp.repeat(_alibi_slopes(nh).reshape(NSA_KV, nrep), qblk, axis=1)
    s_hi = slf.astype(BF16).astype(F32)
    s_lo = (slf - s_hi).astype(BF16).astype(F32)
    extra = jnp.stack([POS_SPLIT * s_hi, s_hi, POS_SPLIT * s_lo, s_lo] + [jnp.zeros_like(slf)] * (Q_EXTRA - 4), axis=1)
    qt = jnp.concatenate([qt, jnp.broadcast_to(extra[None, :, None], (nb, NSA_KV, nqb, Q_EXTRA, rows))], axis=3)
    gs = gates.reshape(nb, nqb, qblk, 3, NSA_KV, nrep).transpose(0, 4, 1, 3, 5, 2).reshape(nb, NSA_KV, nqb, 3, rows)
    gs = jnp.pad(gs, ((0, 0),) * 3 + ((0, SUBLANES - 3), (0, 0)))
    sl = jnp.pad(slf[:, None, :], ((0, 0), (0, SUBLANES - 1), (0, 0)))
    t = jnp.arange(seq, dtype=I32)
    pos_cols = jnp.stack([t // POS_SPLIT, t % POS_SPLIT] * 2 + [jnp.zeros_like(t)] * (Q_EXTRA - 4), axis=1).astype(F32)
    pos_cols = jnp.broadcast_to(pos_cols[None, None], (nb, NSA_KV, seq, Q_EXTRA))
    with_pos = lambda cols: jnp.concatenate(
        [cols.reshape(nb, seq, NSA_KV, NSA_DH).transpose(0, 2, 1, 3), pos_cols], axis=-1).astype(BF16)
    ksa = with_pos(proj[:, qw + 2 * pw:qw + 3 * pw])
    kwa = with_pos(proj[:, qw + 3 * pw:qw + 4 * pw])
    nbs = -(-seq // SEL_BLK)
    nbs_r = -(-nbs // SUBLANES) * SUBLANES
    o = _nsa_prompt(qt, gs, sl, cmp_out[:, 0], cmp_out[:, 1].transpose(0, 1, 3, 2), ksa, kwa,
                    kvt.reshape(nb, 6 * NSA_KV, NSA_DH, seq), _overlap(seq // CMP_STRIDE, nbs_r).T, nbs)
    o = o.reshape(nb, NSA_KV, nqb, NSA_DH, nrep, qblk).transpose(0, 2, 5, 1, 4, 3).reshape(nb * seq, nh * NSA_DH)
    kv6 = kvt.reshape(nb, 6, NSA_KV, NSA_DH, seq).transpose(0, 4, 1, 2, 3)
    return o, kv6[:, :, :4], kv6[:, :, 4:]


def _nsa_sample_mixer(x, nb, nq, cache, page_table, win_buf, prm):
    w_in, pos, w1, b1, w2, b2 = prm
    nh = (w_in.shape[1] - 6 * NSA_KV * NSA_DH) // (NSA_DH + 3)
    nrep = nh // NSA_KV
    pw = NSA_KV * NSA_DH
    npg = page_table.shape[1]
    past = npg * PAGE
    q, kv4, win, gates = _nsa_in(x, w_in, nh)
    n_pool = cache.shape[0]
    cache_t = cache.transpose(0, 2, 3, 4, 1).reshape(n_pool, 4, pw, PAGE)
    win_t = win_buf.transpose(0, 2, 3, 4, 1).reshape(nb, 2, pw, win_buf.shape[1])
    cmp_out = _compress(cache_t, page_table, pos, w1, b1, w2, b2, token_minor=True)
    nsub = cmp_out.shape[3]
    kc = cmp_out[:, 0].transpose(0, 2, 1, 3).reshape(nb, nsub, pw)
    vc = cmp_out[:, 1].transpose(0, 2, 1, 3).reshape(nb, nsub, pw)
    rows = NSA_KV * nrep * nq
    q5 = q.reshape(nb, nq, NSA_KV, nrep, NSA_DH).transpose(0, 2, 3, 1, 4)
    qbd = jnp.einsum("bgrqd,gh->bgrqhd", q5, jnp.eye(NSA_KV, dtype=F32)).reshape(nb, rows, pw)
    gs = gates.reshape(nb, nq, 3, NSA_KV, nrep).transpose(0, 3, 4, 1, 2).reshape(nb, rows, 3)
    gs = jnp.pad(gs, ((0, 0), (0, 0), (0, SUBLANES - 3)))
    sl = jnp.repeat(_alibi_slopes(nh), nq)[:, None]
    sl = jnp.pad(sl, ((0, 0), (0, SUBLANES - 1)))
    nbs = -(-(past + nq) // SEL_BLK)
    nbs_pad = -(-nbs // LANES) * LANES
    o = _nsa_sample(qbd, gs, sl, kc, vc, cache_t, page_table, kv4.reshape(nb, nq, 4 * pw),
                    win_t, win.reshape(nb, nq, 2 * pw), _overlap(nsub, nbs_pad), nbs, nq, nrep)
    o = o.reshape(nb, NSA_KV, nrep, nq, NSA_DH).transpose(0, 3, 1, 2, 4).reshape(nb * nq, nh * NSA_DH)
    return o, kv4.reshape(nb, nq, 4, NSA_KV, NSA_DH), win.reshape(nb, nq, 2, NSA_KV, NSA_DH)


def kernel(x_prompt, x_sample, state_ssm, state_conv, cache_kv, state_win, page_table, ln_mix_g, ln_mix_b, ln_ffn_g, ln_ffn_b, ssd_w_in, ssd_conv_w, ssd_conv_b, ssd_dt_bias, ssd_a_log, ssd_d, ssd_norm_w, ssd_w_out, gm_w_in, gm_b_in, gm_ln_g, gm_ln_b, gm_w_s, gm_b_s, gm_w_out, nsa_w_in, nsa_cmp_pos, nsa_cmp_w1, nsa_cmp_b1, nsa_cmp_w2, nsa_cmp_b2, nsa_w_out, ffn_w_gate, ffn_w_up, ffn_w_down, moe_w_router, moe_b_router, moe_w_gate, moe_w_up, moe_w_down):
    depth = ln_mix_g.shape[0]
    alpha = (2 * depth) ** 0.25
    bp, seq, d = x_prompt.shape
    db, dseq, _ = x_sample.shape
    xp = x_prompt.reshape(bp * seq, d)
    xs = x_sample.reshape(db * dseq, d)
    outs = {k: [] for k in ("ssm_p", "conv_p", "ssm_s", "conv_s", "gmv_s", "kv_p", "win_p", "kv_s", "win_s")}
    row = lambda a, i: a[i][None, :]
    for l in range(depth):
        kind, j = l % N_MIXERS, l // N_MIXERS
        mg, mb = row(ln_mix_g, l), row(ln_mix_b, l)
        if kind == 0:
            prm = (ssd_w_in[j], ssd_conv_w[j], ssd_conv_b[j], ssd_dt_bias[j], ssd_a_log[j], ssd_d[j], ssd_norm_w[j])
            cdim = ssd_conv_w.shape[2]
            w_out = ssd_w_out[j].astype(BF16)
            yp, cp, hp = _ssd_mixer(xp, bp, SSD_CHUNK, jnp.zeros((bp, 8, cdim), F32),
                                    jnp.zeros((bp,) + state_ssm.shape[2:], F32), prm)
            ys, cs, hs = _ssd_mixer(xs, db, dseq, jnp.pad(state_conv[j], ((0, 0), (8 - (SSD_CONV - 1), 0), (0, 0))),
                                    state_ssm[j], prm)
            outs["ssm_p"].append(hp), outs["conv_p"].append(cp), outs["ssm_s"].append(hs), outs["conv_s"].append(cs)
            xp = _mm_res_ln(yp, xp, w_out, mg, mb, alpha)
            xs = _mm_res_ln(ys, xs, w_out, mg, mb, alpha)
        elif kind == 1:
            prm = (gm_w_in[j], gm_b_in[j], gm_ln_g[j], gm_ln_b[j], gm_w_s[j], gm_b_s[j], gm_w_out[j])
            xp, _ = _gmlp_mixer(xp, bp, seq, prm, mg, mb, alpha)
            xs, vs = _gmlp_mixer(xs, db, dseq, prm, mg, mb, alpha)
            outs["gmv_s"].append(vs.reshape(db, dseq, -1))
        else:
            prm = (nsa_w_in[j], nsa_cmp_pos[j], nsa_cmp_w1[j], nsa_cmp_b1[j], nsa_cmp_w2[j], nsa_cmp_b2[j])
            w_out = nsa_w_out[j].astype(BF16)
            op, kvp, wp = _nsa_prompt_mixer(xp, bp, seq, prm)
            os_, kvs, wsn = _nsa_sample_mixer(xs, db, dseq, cache_kv[j], page_table, state_win[j], prm)
            outs["kv_p"].append(kvp), outs["win_p"].append(wp[:, seq - min(WINDOW, seq):])
            outs["kv_s"].append(kvs), outs["win_s"].append(wsn)
            xp = _mm_res_ln(op, xp, w_out, mg, mb, alpha)
            xs = _mm_res_ln(os_, xs, w_out, mg, mb, alpha)
        f = l // 2
        fg, fb = row(ln_ffn_g, l), row(ln_ffn_b, l)
        if l % 2 == 0:
            wg, wu, wd = ffn_w_gate[f].astype(BF16), ffn_w_up[f].astype(BF16), ffn_w_down[f].astype(BF16)
            xp = _swiglu_res_ln(xp, wg, wu, wd, fg, fb, alpha)
            xs = _swiglu_res_ln(xs, wg, wu, wd, fg, fb, alpha)
        else:
            wg, wu, wd = moe_w_gate[f].astype(BF16), moe_w_up[f].astype(BF16), moe_w_down[f].astype(BF16)
            xp = _moe_res_ln(xp, moe_w_router[f], moe_b_router[f], wg, wu, wd, fg, fb, alpha)
            xs = _moe_res_ln(xs, moe_w_router[f], moe_b_router[f], wg, wu, wd, fg, fb, alpha)
    st = lambda k: jnp.stack(outs[k])
    return (xp.reshape(bp, seq, d), xs.reshape(db, dseq, d), st("ssm_p"), st("conv_p"), st("ssm_s"), st("conv_s"),
            st("gmv_s"), st("kv_p"), st("win_p"), st("kv_s"), st("win_s"))
```

```python
import functools

import jax
import jax.numpy as jnp
from jax import lax
from jax.experimental import pallas as pl
from jax.experimental.pallas import tpu as pltpu

F32 = jnp.float32
BF16 = jnp.bfloat16
I32 = jnp.int32

LN_EPS = 1e-5
NEG = -1e30
N_MIXERS = 3
SSD_P = 64
SSD_N = 128
SSD_G = 4
SSD_CONV = 4
SSD_CHUNK = 128
GM_CHUNK = 128
GM_G = 8
NSA_KV = 4
NSA_DH = 64
CMP_BLK = 32
CMP_STRIDE = 16
SEL_BLK = 64
N_SEL = 16
WINDOW = 512
FORCE = 1e4
TOP_K = 2

LANES = 128
SUBLANES = 8
MIB = 1 << 20


def _cparams(sem, vmem_mib):
    return pltpu.CompilerParams(dimension_semantics=sem, vmem_limit_bytes=vmem_mib * MIB)


def _row_tile(m, cands=(1024, 512, 256, 128, 64, 32, 16, 8)):
    for c in cands:
        if m % c == 0:
            return c
    raise ValueError(f"no row tile for {m}")


def _ln(v, g, b):
    mu = jnp.mean(v, -1, keepdims=True)
    d = v - mu
    var = jnp.mean(d * d, -1, keepdims=True)
    return d * lax.rsqrt(var + LN_EPS) * g + b


def _split(x, n):
    out = []
    r = x
    for k in range(n):
        h = r.astype(BF16)
        out.append(h)
        if k + 1 < n:
            r = r - h.astype(F32)
    return out


def _dot(a, b):
    return jnp.dot(a, b, preferred_element_type=F32)


def _dot_nt(a, b):
    return lax.dot_general(a, b, (((1,), (1,)), ((), ())), preferred_element_type=F32)


def _dot_tn(a, b):
    return lax.dot_general(a, b, (((0,), (0,)), ((), ())), preferred_element_type=F32)


def _silu(x):
    return x * jax.nn.sigmoid(x)


def _gelu_tanh(x):
    return 0.5 * x * (1.0 + jnp.tanh(0.7978845608028654 * (x + 0.044715 * (x * x * x))))


def _mm_kernel(x_ref, w_ref, b_ref, o_ref, *, act):
    acc = _dot(x_ref[...].astype(BF16), w_ref[...]) + b_ref[...]
    if act == "sigmoid":
        acc = jax.nn.sigmoid(acc)
    o_ref[...] = acc.astype(o_ref.dtype)


def _matmul(x, w, b=None, act=None, tn=None):
    m, k = x.shape
    n = w.shape[1]
    tm = _row_tile(m)
    tn = n if tn is None else tn
    if b is None:
        b = jnp.zeros((1, n), F32)
    return pl.pallas_call(
        functools.partial(_mm_kernel, act=act),
        grid=(m // tm, n // tn),
        in_specs=[pl.BlockSpec((tm, k), lambda i, j: (i, 0)),
                  pl.BlockSpec((k, tn), lambda i, j: (0, j)),
                  pl.BlockSpec((1, tn), lambda i, j: (0, j))],
        out_specs=pl.BlockSpec((tm, tn), lambda i, j: (i, j)),
        out_shape=jax.ShapeDtypeStruct((m, n), F32),
        compiler_params=_cparams(("parallel", "arbitrary"), 48),
        name="matmul",
    )(x, w, b)


def _mm_res_ln_kernel(y_ref, x_ref, w_ref, g_ref, b_ref, o_ref, *, alpha):
    f = _dot(y_ref[...].astype(BF16), w_ref[...])
    o_ref[...] = _ln(alpha * x_ref[...] + f, g_ref[...], b_ref[...])


def _mm_res_ln(y, x, w, g, b, alpha):
    m, k = y.shape
    d = x.shape[1]
    tm = _row_tile(m, (512, 256, 128))
    return pl.pallas_call(
        functools.partial(_mm_res_ln_kernel, alpha=alpha),
        grid=(m // tm,),
        in_specs=[pl.BlockSpec((tm, k), lambda i: (i, 0)),
                  pl.BlockSpec((tm, d), lambda i: (i, 0)),
                  pl.BlockSpec((k, d), lambda i: (0, 0)),
                  pl.BlockSpec((1, d), lambda i: (0, 0)),
                  pl.BlockSpec((1, d), lambda i: (0, 0))],
        out_specs=pl.BlockSpec((tm, d), lambda i: (i, 0)),
        out_shape=jax.ShapeDtypeStruct((m, d), F32),
        compiler_params=_cparams(("parallel",), 48),
        name="mm_res_ln",
    )(y, x, w, g, b)


def _swiglu_kernel(x_ref, wg_ref, wu_ref, wd_ref, g_ref, b_ref, o_ref, acc_ref, xb_ref, *, alpha):
    f = pl.program_id(1)

    @pl.when(f == 0)
    def _():
        acc_ref[...] = jnp.zeros_like(acc_ref)
        xb_ref[...] = x_ref[...].astype(BF16)

    xb = xb_ref[...]
    h = _silu(_dot(xb, wg_ref[...])) * _dot(xb, wu_ref[...])
    acc_ref[...] += _dot(h.astype(BF16), wd_ref[...])

    @pl.when(f == pl.num_programs(1) - 1)
    def _():
        o_ref[...] = _ln(alpha * x_ref[...] + acc_ref[...], g_ref[...], b_ref[...])


def _ff_tile(dff):
    for c in (512, 256, 128):
        if dff % c == 0:
            return c
    return dff


def _swiglu_res_ln(x, wg, wu, wd, g, b, alpha):
    m, d = x.shape
    dff = wg.shape[1]
    tm = _row_tile(m)
    tf = _ff_tile(dff)
    return pl.pallas_call(
        functools.partial(_swiglu_kernel, alpha=alpha),
        grid=(m // tm, dff // tf),
        in_specs=[pl.BlockSpec((tm, d), lambda i, f: (i, 0)),
                  pl.BlockSpec((d, tf), lambda i, f: (0, f)),
                  pl.BlockSpec((d, tf), lambda i, f: (0, f)),
                  pl.BlockSpec((tf, d), lambda i, f: (f, 0)),
                  pl.BlockSpec((1, d), lambda i, f: (0, 0)),
                  pl.BlockSpec((1, d), lambda i, f: (0, 0))],
        out_specs=pl.BlockSpec((tm, d), lambda i, f: (i, 0)),
        out_shape=jax.ShapeDtypeStruct((m, d), F32),
        scratch_shapes=[pltpu.VMEM((tm, d), F32), pltpu.VMEM((tm, d), BF16)],
        compiler_params=_cparams(("parallel", "arbitrary"), 48),
        name="swiglu_res_ln",
    )(x, wg, wu, wd, g, b)


def _ssd_kernel(xbc_ref, z_ref, dt_ref, conv0_ref, h0_ref, cw_ref, cb_ref, dtb_ref, alog_ref,
                dexp_ref, nw_ref, e_ref, ltri_ref, y_ref, h_ref, xp_scr, dtp_scr, yacc_scr,
                *, lv, n_heads):
    q = SSD_CHUNK
    di = n_heads * SSD_P
    gn = SSD_G * SSD_N
    hpg = n_heads // SSD_G
    gw = hpg * SSD_P
    ci = pl.program_id(1)

    @pl.when(ci == 0)
    def _():
        xp_scr[0:8, :] = conv0_ref[0]
        h_ref[0] = h0_ref[0]

    @pl.when(ci > 0)
    def _():
        xp_scr[0:8, :] = xp_scr[q:q + 8, :]

    if lv < q:
        xp_scr[8 + lv:, :] = jnp.zeros((q - lv, xp_scr.shape[1]), F32)
        dtp_scr[...] = jnp.zeros_like(dtp_scr)
        dtp_scr[0:lv, :] = dt_ref[...]
        dt_raw = dtp_scr[...]
    else:
        dt_raw = dt_ref[...]
    xp_scr[8:8 + lv, :] = xbc_ref[...]

    conv = cb_ref[...]
    for k in range(SSD_CONV):
        conv = conv + cw_ref[k:k + 1, :] * xp_scr[5 + k:5 + k + q, :]
    xc = _silu(conv)
    row = lax.broadcasted_iota(I32, (q, LANES), 0)
    lane = lax.broadcasted_iota(I32, (q, LANES), 1)
    if lv < q:
        xc = jnp.where(lax.broadcasted_iota(I32, xc.shape, 0) < lv, xc, 0.0)

    v = dt_raw + dtb_ref[...]
    dt = jnp.maximum(v, 0.0) + jnp.log1p(jnp.exp(-jnp.abs(v)))
    dt = jnp.where((row < lv) & (lane < n_heads), dt, 0.0)
    adt = dt * (-jnp.exp(alog_ref[...]))
    acum = sum(_dot(ltri_ref[...], t) for t in _split(adt, 3))
    acum_t = acum.T
    dt_t = dt.T
    ea = jnp.exp(acum)
    w = dt * jnp.exp(acum[q - 1:q, :] - acum)
    w_exp = sum(_dot(t, e_ref[...]) for t in _split(w, 2))
    ea_exp = sum(_dot(t, e_ref[...]) for t in _split(ea[:lv], 2))
    cdm = jnp.exp(jnp.broadcast_to(acum_t[:, q - 1:q], (LANES, LANES)))

    xs = xc[:, :di]
    xd = (xs * w_exp).astype(BF16)
    causal = (lax.broadcasted_iota(I32, (lv, q), 0) >= lax.broadcasted_iota(I32, (lv, q), 1))
    lo = lane < SSD_P

    for g in range(SSD_G):
        bg = xc[:, di + g * SSD_N:di + (g + 1) * SSD_N].astype(BF16)
        cg = xc[:lv, di + gn + g * SSD_N:di + gn + (g + 1) * SSD_N].astype(BF16)
        cbm = _dot_nt(cg, bg)
        hg = h_ref[0, g * gw:(g + 1) * gw, :].astype(BF16)
        yoff = _dot_nt(cg, hg)
        st = _dot_tn(xd[:, g * gw:(g + 1) * gw], bg)
        for pr in range(hpg // 2):
            col = g * gw + pr * LANES
            ms = []
            for hh in (2 * pr, 2 * pr + 1):
                h = g * hpg + hh
                seg = acum[:lv, h:h + 1] - acum_t[h:h + 1, :]
                dec = jnp.exp(jnp.where(causal, seg, -jnp.inf))
                ms.append((cbm * dec * dt_t[h:h + 1, :]).astype(BF16))
            xpair = xs[:, col:col + LANES]
            rhs = jnp.concatenate([jnp.where(lo, xpair, 0.0), jnp.where(lo, 0.0, xpair)], axis=0).astype(BF16)
            yd = _dot(jnp.concatenate(ms, axis=1), rhs)
            yacc_scr[:, col:col + LANES] = (yd + yoff[:, pr * LANES:(pr + 1) * LANES] * ea_exp[:, col:col + LANES]
                                            + dexp_ref[:, col:col + LANES] * xs[:lv, col:col + LANES])
        for hh in range(hpg):
            h = g * hpg + hh
            r0 = h * SSD_P
            h_ref[0, r0:r0 + SSD_P, :] = (h_ref[0, r0:r0 + SSD_P, :] * cdm[h:h + 1, :]
                                          + st[hh * SSD_P:(hh + 1) * SSD_P, :])

    y = yacc_scr[...] * _silu(z_ref[...])
    ng = di // SSD_G
    for g in range(SSD_G):
        yg = y[:, g * ng:(g + 1) * ng]
        ms_ = jnp.mean(yg * yg, -1, keepdims=True)
        yn = yg * lax.rsqrt(ms_ + LN_EPS) * nw_ref[:, g * ng:(g + 1) * ng]
        y_ref[:, g * ng:(g + 1) * ng] = yn.astype(y_ref.dtype)


def _ssd_scan(xbc, z, dt, conv0, h0, cw, cb, dtb, alog, dexp, nw, e, ltri, nb, lv):
    m, cdim = xbc.shape
    di = z.shape[1]
    n_heads = di // SSD_P
    nc = m // (nb * lv)
    y_dtype = BF16 if lv % (2 * SUBLANES) == 0 else F32
    return pl.pallas_call(
        functools.partial(_ssd_kernel, lv=lv, n_heads=n_heads),
        grid=(nb, nc),
        in_specs=[pl.BlockSpec((lv, cdim), lambda b, c: (b * nc + c, 0)),
                  pl.BlockSpec((lv, di), lambda b, c: (b * nc + c, 0)),
                  pl.BlockSpec((lv, LANES), lambda b, c: (b * nc + c, 0)),
                  pl.BlockSpec((1, 8, cdim), lambda b, c: (b, 0, 0)),
                  pl.BlockSpec((1, di, SSD_N), lambda b, c: (b, 0, 0)),
                  pl.BlockSpec((SSD_CONV, cdim), lambda b, c: (0, 0)),
                  pl.BlockSpec((1, cdim), lambda b, c: (0, 0)),
                  pl.BlockSpec((1, LANES), lambda b, c: (0, 0)),
                  pl.BlockSpec((1, LANES), lambda b, c: (0, 0)),
                  pl.BlockSpec((1, di), lambda b, c: (0, 0)),
                  pl.BlockSpec((1, di), lambda b, c: (0, 0)),
                  pl.BlockSpec((LANES, di), lambda b, c: (0, 0)),
                  pl.BlockSpec((SSD_CHUNK, SSD_CHUNK), lambda b, c: (0, 0))],
        out_specs=[pl.BlockSpec((lv, di), lambda b, c: (b * nc + c, 0)),
                   pl.BlockSpec((1, di, SSD_N), lambda b, c: (b, 0, 0))],
        out_shape=[jax.ShapeDtypeStruct((m, di), y_dtype),
                   jax.ShapeDtypeStruct((nb, di, SSD_N), F32)],
        scratch_shapes=[pltpu.VMEM((SSD_CHUNK + 8, cdim), F32),
                        pltpu.VMEM((SSD_CHUNK, LANES), F32),
                        pltpu.VMEM((lv, di), F32)],
        compiler_params=_cparams(("parallel", "arbitrary"), 56),
        name="ssd_scan",
    )(xbc, z, dt, conv0, h0, cw, cb, dtb, alog, dexp, nw, e, ltri)


def _ssd_mixer(x, nb, lv, conv0, h0, prm):
    w_in, conv_w, conv_b, dt_bias, a_log, d_skip, norm_w = prm
    di = norm_w.shape[0]
    cdim = conv_w.shape[1]
    n_heads = di // SSD_P
    seq = x.shape[0] // nb
    w_z = w_in[:, :di].astype(BF16)
    w_x = w_in[:, di:di + cdim].astype(BF16)
    w_dt = jnp.pad(w_in[:, di + cdim:], ((0, 0), (0, LANES - n_heads))).astype(BF16)
    z = _matmul(x, w_z, tn=1024)
    xbc = _matmul(x, w_x, tn=1024)
    dt = _matmul(x, w_dt)
    pad1 = lambda a: jnp.pad(a.astype(F32), (0, LANES - n_heads))[None, :]
    e = (jnp.arange(di)[None, :] // SSD_P == jnp.arange(LANES)[:, None]).astype(BF16)
    ltri = (jnp.arange(SSD_CHUNK)[:, None] >= jnp.arange(SSD_CHUNK)[None, :]).astype(BF16)
    y, h_new = _ssd_scan(xbc, z, dt, conv0, h0.reshape(nb, di, SSD_N), conv_w, conv_b[None, :],
                         pad1(dt_bias), pad1(a_log), jnp.repeat(d_skip, SSD_P)[None, :], norm_w[None, :],
                         e, ltri, nb, lv)
    conv_new = xbc.reshape(nb, seq, cdim)[:, seq - (SSD_CONV - 1):]
    return y, conv_new, h_new.reshape(nb, n_heads, SSD_P, SSD_N)


def _gm_in_kernel(x_ref, w_ref, b_ref, g_ref, bb_ref, u_ref, v_ref):
    h = _gelu_tanh(_dot(x_ref[...].astype(BF16), w_ref[...]) + b_ref[...])
    d = u_ref.shape[1]
    u_ref[...] = h[:, :d]
    v_ref[...] = _ln(h[:, d:], g_ref[...], bb_ref[...])


def _gm_in(x, w, b, g, bb):
    m, k = x.shape
    d = w.shape[1] // 2
    tm = _row_tile(m, (512, 256))
    return pl.pallas_call(
        _gm_in_kernel,
        grid=(m // tm,),
        in_specs=[pl.BlockSpec((tm, k), lambda i: (i, 0)),
                  pl.BlockSpec((k, 2 * d), lambda i: (0, 0)),
                  pl.BlockSpec((1, 2 * d), lambda i: (0, 0)),
                  pl.BlockSpec((1, d), lambda i: (0, 0)),
                  pl.BlockSpec((1, d), lambda i: (0, 0))],
        out_specs=[pl.BlockSpec((tm, d), lambda i: (i, 0)), pl.BlockSpec((tm, d), lambda i: (i, 0))],
        out_shape=[jax.ShapeDtypeStruct((m, d), F32), jax.ShapeDtypeStruct((m, d), F32)],
        compiler_params=_cparams(("parallel",), 48),
        name="gm_in",
    )(x, w, b, g, bb)


def _gm_out_kernel(u_ref, v_ref, x_ref, s_ref, sb_ref, w_ref, g_ref, b_ref, o_ref, gated_scr, *, alpha, r):
    tb = u_ref.shape[0]
    gd = u_ref.shape[1] // GM_G
    for s in range(tb // r):
        rows = slice(s * r, (s + 1) * r)
        for g in range(GM_G):
            cols = slice(g * gd, (g + 1) * gd)
            mixed = _dot(s_ref[g], v_ref[rows, cols].astype(BF16)) + sb_ref[rows, cols]
            gated_scr[rows, cols] = (u_ref[rows, cols] * mixed).astype(BF16)
    f = _dot(gated_scr[...], w_ref[...])
    o_ref[...] = _ln(alpha * x_ref[...] + f, g_ref[...], b_ref[...])


def _gm_out(u, v, x, smat, sbias, w, g, b, alpha, tb):
    m, d = u.shape
    r = smat.shape[1]
    return pl.pallas_call(
        functools.partial(_gm_out_kernel, alpha=alpha, r=r),
        grid=(m // tb,),
        in_specs=[pl.BlockSpec((tb, d), lambda i: (i, 0)),
                  pl.BlockSpec((tb, d), lambda i: (i, 0)),
                  pl.BlockSpec((tb, d), lambda i: (i, 0)),
                  pl.BlockSpec((GM_G, r, r), lambda i: (0, 0, 0)),
                  pl.BlockSpec((tb, d), lambda i: (0, 0)),
                  pl.BlockSpec((d, d), lambda i: (0, 0)),
                  pl.BlockSpec((1, d), lambda i: (0, 0)),
                  pl.BlockSpec((1, d), lambda i: (0, 0))],
        out_specs=pl.BlockSpec((tb, d), lambda i: (i, 0)),
        out_shape=jax.ShapeDtypeStruct((m, d), F32),
        scratch_shapes=[pltpu.VMEM((tb, d), BF16)],
        compiler_params=_cparams(("parallel",), 48),
        name="gm_out",
    )(u, v, x, smat, sbias, w, g, b)


def _gmlp_mixer(x, nb, seq, prm, ln_g, ln_b, alpha):
    w_in, b_in, g_in, bb_in, w_s, b_s, w_out = prm
    d = w_out.shape[0]
    gd = d // GM_G
    u, v = _gm_in(x, w_in.astype(BF16), b_in[None, :], g_in[None, :], bb_in[None, :])
    ws = jnp.tril(w_s)
    if seq % GM_CHUNK == 0:
        tb = 512
        smat = ws.astype(BF16)
        bias_rows = jnp.repeat(b_s.T, gd, axis=1)
        sbias = jnp.tile(bias_rows, (tb // GM_CHUNK, 1))
    else:
        tb = nb * seq
        smat = jnp.einsum("ab,gts->gatbs", jnp.eye(nb, dtype=F32), ws[:, :seq, :seq]).reshape(GM_G, tb, tb).astype(BF16)
        sbias = jnp.tile(jnp.repeat(b_s.T[:seq], gd, axis=1), (nb, 1))
    out = _gm_out(u, v, x, smat, sbias, w_out.astype(BF16), ln_g, ln_b, alpha, tb)
    return out, v


def _router_kernel(x_ref, wh_ref, wl_ref, b_ref, eid_ref, gate_ref):
    x = x_ref[...]
    xh, xl = _split(x, 2)
    logits = _dot(xh, wh_ref[...]) + _dot(xl, wh_ref[...]) + _dot(xh, wl_ref[...]) + b_ref[...]
    lane = lax.broadcasted_iota(I32, logits.shape, 1)
    m1 = jnp.max(logits, -1, keepdims=True)
    i1 = jnp.min(jnp.where(logits == m1, lane, LANES), -1, keepdims=True)
    rest = jnp.where(lane == i1, NEG * 2, logits)
    m2 = jnp.max(rest, -1, keepdims=True)
    i2 = jnp.min(jnp.where(rest == m2, lane, LANES), -1, keepdims=True)
    e = jnp.exp(m2 - m1)
    g1 = 1.0 / (1.0 + e)
    eid_ref[...] = jnp.where(lane == 0, i1, jnp.where(lane == 1, i2, 0))
    gate_ref[...] = jnp.where(lane == 0, g1, jnp.where(lane == 1, e * g1, 0.0))


def _router(x, w_router, b_router):
    m, d = x.shape
    ne = w_router.shape[1]
    w = jnp.pad(w_router, ((0, 0), (0, LANES - ne)))
    wh = w.astype(BF16)
    wl = (w - wh.astype(F32)).astype(BF16)
    b = jnp.pad(b_router.astype(F32), (0, LANES - ne), constant_values=NEG)[None, :]
    tm = _row_tile(m, (512, 256))
    return pl.pallas_call(
        _router_kernel,
        grid=(m // tm,),
        in_specs=[pl.BlockSpec((tm, d), lambda i: (i, 0)),
                  pl.BlockSpec((d, LANES), lambda i: (0, 0)),
                  pl.BlockSpec((d, LANES), lambda i: (0, 0)),
                  pl.BlockSpec((1, LANES), lambda i: (0, 0))],
        out_specs=[pl.BlockSpec((tm, LANES), lambda i: (i, 0)), pl.BlockSpec((tm, LANES), lambda i: (i, 0))],
        out_shape=[jax.ShapeDtypeStruct((m, LANES), I32), jax.ShapeDtypeStruct((m, LANES), F32)],
        compiler_params=_cparams(("parallel",), 32),
        name="router",
    )(x, wh, wl, b)


def _row_copy(src_hbm, dst, s_row, d_row, sem):
    return pltpu.make_async_copy(src_hbm.at[pl.ds(s_row, 1), :], dst.at[pl.ds(d_row, 1), :], sem)


GATHER_PRIORITY = 1


def _row_gather(src_hbm, dst, idx_ref, n, per_row, sem, start):
    if not start:
        for k in range(per_row):
            pltpu.make_async_copy(src_hbm.at[pl.ds(0, n), :], dst.at[k], sem).wait()
        return

    def body(j, c):
        for k in range(per_row):
            prio = GATHER_PRIORITY if per_row == 1 else k % 2
            _row_copy(src_hbm, dst.at[k], idx_ref[0, 0, per_row * j + k], j, sem).start(priority=prio)
        return c

    lax.fori_loop(0, n, body, 0, unroll=8)


def _moe_ffn_kernel(be_ref, nu_ref, idx_ref, idxn_ref, x_hbm, wg_ref, wu_ref, wd_ref, o_ref,
                    acc_ref, xg_ref, xb_ref, sem, *, tb, rps):
    i = pl.program_id(0)
    f = pl.program_id(1)
    last = pl.num_programs(1) - 1
    used = i < nu_ref[0]
    slot = i % 2
    tbp = xg_ref.shape[2]

    def start_next_rows():
        base = f * rps
        for j in range(rps):
            _row_copy(x_hbm, xg_ref.at[1 - slot, 0], idxn_ref[0, 0, base + j], base + j,
                      sem.at[1 - slot]).start(priority=GATHER_PRIORITY)

    @pl.when((f == 0) & (i == 0))
    def _():
        _row_gather(x_hbm, xg_ref.at[0], idx_ref, tbp, 1, sem.at[0], True)

    @pl.when(f == 0)
    def _():
        _row_gather(x_hbm, xg_ref.at[slot], idx_ref, tbp, 1, sem.at[slot], False)

    @pl.when(used & (f == 0))
    def _():
        acc_ref[...] = jnp.zeros_like(acc_ref)
        xb_ref[...] = xg_ref[slot, 0, 0:tb, :].astype(BF16)

    @pl.when(used)
    def _():
        start_next_rows()
        xb = xb_ref[...]
        h = _silu(_dot(xb, wg_ref[0])) * _dot(xb, wu_ref[0])
        acc_ref[...] += _dot(h.astype(BF16), wd_ref[0])

    @pl.when(jnp.logical_not(used))
    def _():
        start_next_rows()

    @pl.when(used & (f == last))
    def _():
        o_ref[...] = acc_ref[...]

    @pl.when(jnp.logical_not(used) & (f == last))
    def _():
        o_ref[...] = jnp.zeros_like(o_ref)

    @pl.when((i == pl.num_programs(0) - 1) & (f == last))
    def _():
        _row_gather(x_hbm, xg_ref.at[1 - slot], idx_ref, tbp, 1, sem.at[1 - slot], False)


def _moe_ffn(x, buf_tok, blk_e, n_used, wg, wu, wd, tb):
    d = x.shape[1]
    nblk = buf_tok.shape[0] // tb
    dff = wg.shape[2]
    tf = _ff_tile(dff)
    nf = dff // tf
    rps = -(-tb // (nf * SUBLANES)) * SUBLANES
    tbp = rps * nf

    def fe(i, f, be, nu):
        return jnp.where(i < nu[0], f, nf - 1)

    grid_spec = pltpu.PrefetchScalarGridSpec(
        num_scalar_prefetch=2,
        grid=(nblk, nf),
        in_specs=[pl.BlockSpec((1, 1, tbp), lambda i, f, be, nu: (i, 0, 0), memory_space=pltpu.SMEM),
                  pl.BlockSpec((1, 1, tbp), lambda i, f, be, nu: (jnp.minimum(i + 1, nblk - 1), 0, 0),
                               memory_space=pltpu.SMEM),
                  pl.BlockSpec(memory_space=pl.ANY),
                  pl.BlockSpec((1, d, tf), lambda i, f, be, nu: (be[i], 0, fe(i, f, be, nu))),
                  pl.BlockSpec((1, d, tf), lambda i, f, be, nu: (be[i], 0, fe(i, f, be, nu))),
                  pl.BlockSpec((1, tf, d), lambda i, f, be, nu: (be[i], fe(i, f, be, nu), 0))],
        out_specs=pl.BlockSpec((tb, d), lambda i, f, be, nu: (i, 0)),
        scratch_shapes=[pltpu.VMEM((tb, d), F32), pltpu.VMEM((2, 1, tbp, d), F32), pltpu.VMEM((tb, d), BF16),
                        pltpu.SemaphoreType.DMA((2,))])
    idx = jnp.pad(buf_tok.reshape(nblk, 1, tb), ((0, 0), (0, 0), (0, tbp - tb)))
    return pl.pallas_call(
        functools.partial(_moe_ffn_kernel, tb=tb, rps=rps),
        grid_spec=grid_spec,
        out_shape=jax.ShapeDtypeStruct((nblk * tb, d), F32),
        compiler_params=_cparams(("arbitrary", "arbitrary"), 56),
        name="moe_ffn",
    )(blk_e, n_used, idx, idx, x, wg, wu, wd)


def _combine_kernel(idx_ref, idxn_ref, yb_hbm, x_ref, gate_ref, g_ref, b_ref, o_ref, ybuf, sem, *, alpha, tm):
    i = pl.program_id(0)
    slot = i % 2

    @pl.when(i == 0)
    def _():
        _row_gather(yb_hbm, ybuf.at[0], idx_ref, tm, TOP_K, sem.at[0], True)

    @pl.when(i + 1 < pl.num_programs(0))
    def _():
        _row_gather(yb_hbm, ybuf.at[1 - slot], idxn_ref, tm, TOP_K, sem.at[1 - slot], True)

    _row_gather(yb_hbm, ybuf.at[slot], idx_ref, tm, TOP_K, sem.at[slot], False)
    gate = gate_ref[...]
    y = gate[:, 0:1] * ybuf[slot, 0] + gate[:, 1:2] * ybuf[slot, 1]
    o_ref[...] = _ln(alpha * x_ref[...] + y, g_ref[...], b_ref[...])


def _moe_combine(yb, dest, x, gate, g, b, alpha):
    m, d = x.shape
    tm = _row_tile(m, (256, 128))
    nblk = m // tm
    idx = dest.reshape(nblk, 1, TOP_K * tm)
    return pl.pallas_call(
        functools.partial(_combine_kernel, alpha=alpha, tm=tm),
        grid=(nblk,),
        in_specs=[pl.BlockSpec((1, 1, TOP_K * tm), lambda i: (i, 0, 0), memory_space=pltpu.SMEM),
                  pl.BlockSpec((1, 1, TOP_K * tm), lambda i: (jnp.minimum(i + 1, nblk - 1), 0, 0),
                               memory_space=pltpu.SMEM),
                  pl.BlockSpec(memory_space=pl.ANY),
                  pl.BlockSpec((tm, d), lambda i: (i, 0)),
                  pl.BlockSpec((tm, LANES), lambda i: (i, 0)),
                  pl.BlockSpec((1, d), lambda i: (0, 0)),
                  pl.BlockSpec((1, d), lambda i: (0, 0))],
        out_specs=pl.BlockSpec((tm, d), lambda i: (i, 0)),
        out_shape=jax.ShapeDtypeStruct((m, d), F32),
        scratch_shapes=[pltpu.VMEM((2, TOP_K, tm, d), F32), pltpu.SemaphoreType.DMA((2,))],
        compiler_params=_cparams(("arbitrary",), 32),
        name="moe_combine",
    )(idx, idx, yb, x, gate, g, b)


def _moe_res_ln(x, w_router, b_router, wg, wu, wd, g, b, alpha):
    m, d = x.shape
    ne = w_router.shape[1]
    eid, gate = _router(x, w_router, b_router)
    tk = m * TOP_K
    tb = 1024 if tk >= 16384 else 256
    ef = eid[:, :TOP_K].reshape(-1)
    onehot = (ef[:, None] == jnp.arange(ne, dtype=I32)[None, :]).astype(I32)
    csum = jnp.cumsum(onehot, axis=0)
    rank = jnp.sum(csum * onehot, axis=1) - 1
    counts = csum[-1]
    padded = (counts + tb - 1) // tb * tb
    pend = jnp.cumsum(padded)
    dest = (pend - padded)[ef] + rank
    nblk = -(-tk // tb) + ne
    buf_tok = jnp.zeros((nblk * tb,), I32).at[dest].set(jnp.arange(tk, dtype=I32) // TOP_K)
    blk_start = jnp.arange(nblk, dtype=I32) * tb
    blk_e = jnp.minimum(jnp.sum((pend[None, :] <= blk_start[:, None]).astype(I32), axis=1), ne - 1).astype(I32)
    n_used = (pend[-1:] // tb).astype(I32)
    yb = _moe_ffn(x, buf_tok, blk_e, n_used, wg, wu, wd, tb)
    return _moe_combine(yb, dest.astype(I32), x, gate, g, b, alpha)


PAGE = 128


def _compress_kernel(pt_ref, pages_hbm, w1_ref, pos_ref, b1_ref, w2_ref, b2_ref, o_ref, scr, acc_scr, cv_scr, sem,
                     *stage, npg, col0):
    b = pl.program_id(0)
    pw = NSA_KV * NSA_DH
    nsub = npg * PAGE // CMP_STRIDE
    hid = b1_ref.shape[2]

    nslab = 2 * pw // LANES

    def position_term():
        for c in range(2):
            cv = jnp.zeros((SUBLANES, hid), F32) + b1_ref[c]
            for o in range(CMP_STRIDE):
                cv = cv + _dot(pos_ref[c, o].astype(BF16), w1_ref[c, o])[:, :hid]
                cv = cv + _dot(pos_ref[c, CMP_STRIDE + o].astype(BF16), w1_ref[c, o])[:, hid:]
            cv_scr[c] = cv

    if not stage:
        def start(p, c_):
            row = pl.multiple_of(p * PAGE, PAGE)
            for sl in range(nslab):
                pltpu.make_async_copy(pages_hbm.at[pt_ref[b, p], :, pl.ds(col0 + sl * LANES, LANES)],
                                      scr.at[sl, pl.ds(row, PAGE), :], sem.at[0]).start()
            return c_

        lax.fori_loop(0, npg, start, 0)
        pl.when(b == 0)(position_term)
        for sl in range(nslab):
            pltpu.make_async_copy(scr.at[sl], scr.at[sl], sem.at[0]).wait()
    else:
        stg = stage[0]
        cpg = stg.shape[2]
        nch = npg // cpg

        def start_chunk(bb, k, slot):
            def body(p, c_):
                for c in range(2):
                    pltpu.make_async_copy(pages_hbm.at[pt_ref[bb, k * cpg + p], c], stg.at[slot, c, p],
                                          sem.at[slot]).start()
                return c_

            lax.fori_loop(0, cpg, body, 0)

        @pl.when(b == 0)
        def _():
            start_chunk(b, 0, 0)
            position_term()

        for k in range(nch):
            slot = k % 2
            if k + 1 < nch:
                start_chunk(b, k + 1, 1 - slot)
            else:
                pl.when(b + 1 < pl.num_programs(0))(functools.partial(start_chunk, b + 1, 0, (k + 1) % 2))
            for c in range(2):
                pltpu.make_async_copy(stg.at[slot, c], stg.at[slot, c], sem.at[slot]).wait()

            def to_token_major(p, c_):
                row = pl.multiple_of((k * cpg + p) * PAGE, PAGE)
                for sl in range(nslab):
                    c, half = divmod(sl, nslab // 2)
                    scr[sl, pl.ds(row, PAGE), :] = stg[slot, c, p, half * LANES:(half + 1) * LANES, :].T
                return c_

            lax.fori_loop(0, cpg, to_token_major, 0)

    gps = LANES // NSA_DH
    for c in range(2):
        acc_scr[...] = jnp.zeros_like(acc_scr)
        for o in range(CMP_STRIDE):
            for sl in range(nslab // 2):
                xo = scr[c * (nslab // 2) + sl, pl.ds(o, nsub, stride=CMP_STRIDE), :].astype(BF16)
                for gg in range(gps):
                    acc_scr[sl * gps + gg] += _dot(xo[:, gg * NSA_DH:(gg + 1) * NSA_DH], w1_ref[c, o])
        for g in range(NSA_KV):
            acc = acc_scr[g]
            pre = acc[:, :hid] + pltpu.roll(acc[:, hid:], nsub - 1, 0) + cv_scr[c, 0:1, :]
            o_ref[0, c, g] = _dot(_gelu_tanh(pre).astype(BF16), w2_ref[c]) + b2_ref[c]


def _compress(pages, page_table, pos, w1, b1, w2, b2, col0=0, token_minor=False):
    nb, npg = page_table.shape
    nsub = npg * PAGE // CMP_STRIDE
    hid = w1.shape[-1]
    pw = NSA_KV * NSA_DH
    w1cat = jnp.concatenate([w1[:, :CMP_STRIDE], w1[:, CMP_STRIDE:]], axis=-1).astype(BF16)
    posb = jnp.broadcast_to(pos[:, :, None, :], (2, CMP_BLK, SUBLANES, NSA_DH))
    chunk = 16
    assert not token_minor or (npg % chunk == 0 and (npg // chunk) % 2 == 0)
    grid_spec = pltpu.PrefetchScalarGridSpec(
        num_scalar_prefetch=1,
        grid=(nb,),
        in_specs=[pl.BlockSpec(memory_space=pl.ANY),
                  pl.BlockSpec((2, CMP_STRIDE, NSA_DH, 2 * hid), lambda b, pt: (0, 0, 0, 0)),
                  pl.BlockSpec((2, CMP_BLK, SUBLANES, NSA_DH), lambda b, pt: (0, 0, 0, 0)),
                  pl.BlockSpec((2, 1, hid), lambda b, pt: (0, 0, 0)),
                  pl.BlockSpec((2, hid, NSA_DH), lambda b, pt: (0, 0, 0)),
                  pl.BlockSpec((2, 1, NSA_DH), lambda b, pt: (0, 0, 0))],
        out_specs=pl.BlockSpec((1, 2, NSA_KV, nsub, NSA_DH), lambda b, pt: (b, 0, 0, 0, 0)),
        scratch_shapes=[pltpu.VMEM((2 * pw // LANES, npg * PAGE, LANES), F32),
                        pltpu.VMEM((NSA_KV, nsub, 2 * hid), F32),
                        pltpu.VMEM((2, SUBLANES, hid), F32),
                        pltpu.SemaphoreType.DMA((2,))]
                       + ([pltpu.VMEM((2, 2, chunk, pw, PAGE), F32)] if token_minor else []))
    return pl.pallas_call(
        functools.partial(_compress_kernel, npg=npg, col0=col0),
        grid_spec=grid_spec,
        out_shape=jax.ShapeDtypeStruct((nb, 2, NSA_KV, nsub, NSA_DH), F32),
        compiler_params=_cparams(("arbitrary",), 56),
        name="nsa_compress",
    )(page_table, pages, w1cat, posb, b1[:, None, :], w2.astype(BF16), b2[:, None, :])


def _masked_softmax(s, valid):
    s = jnp.where(valid, s, NEG)
    p = jnp.exp(s - jnp.max(s, -1, keepdims=True))
    return p / jnp.sum(p, -1, keepdims=True)


def _select_blocks(imp, ovl, tq, nbs):
    score = sum(_dot(t, ovl) for t in _split(imp, 3))
    jb = lax.broadcasted_iota(I32, score.shape, 1)
    ok = jb * SEL_BLK <= tq
    cur = tq // SEL_BLK
    forced = (jb == 0) | (jb == cur) | (jb == cur - 1)
    score = jnp.where(ok, score + jnp.where(forced, FORCE, 0.0), NEG)
    rank = jnp.zeros(score.shape, F32)
    for j2 in range(nbs):
        cj = score[:, j2:j2 + 1]
        beats = (cj > score) | ((cj == score) & (j2 < jb))
        rank = rank + jnp.where(beats, 1.0, 0.0)
    return jnp.where(ok & (rank < N_SEL), 1.0, 0.0)


def _expand_sel(sel, first_blk, nkeys):
    jj = lax.broadcasted_iota(I32, (sel.shape[1], nkeys), 0)
    kk = lax.broadcasted_iota(I32, (sel.shape[1], nkeys), 1)
    expand = jnp.where(kk // SEL_BLK + first_blk == jj, 1.0, 0.0).astype(BF16)
    return _dot(sel.astype(BF16), expand)


POS_SPLIT = 64
Q_EXTRA = 16


PROMPT_KT = 512


def _nsa_prompt_kernel(q_ref, gate_ref, slope_ref, kc_ref, vc_ref, ks_ref, vs_ref, kw_ref, vw_ref, ovl_ref, exp_ref,
                       o_ref, *, nbs, qblk, nrep):
    qb = pl.program_id(2)
    cols = qblk * nrep
    q = q_ref[0, 0, 0]
    q16 = q.astype(BF16)
    slope = slope_ref[0][0:1, :]
    tq = qb * qblk + lax.broadcasted_iota(I32, (1, cols), 1) % qblk
    tq_row = tq[:, 0:qblk]

    kc = kc_ref[0, 0]
    qh, ql = _split(q[:NSA_DH, :], 2)
    kh, kl = _split(kc, 2)
    s = _dot(kh, qh) + _dot(kh, ql) + _dot(kl, qh)
    e = lax.broadcasted_iota(I32, (kc.shape[0], 1), 0) * CMP_STRIDE + (CMP_BLK - 1)
    valid = e <= tq
    s = jnp.where(valid, s + slope * e.astype(F32), NEG)
    p = jnp.exp(s - jnp.max(s, 0, keepdims=True))
    p = jnp.where(valid, p / jnp.sum(p, 0, keepdims=True), 0.0)
    o_cmp = _dot(vc_ref[0, 0].astype(BF16), p.astype(BF16))
    imp = p[:, 0:qblk]
    for r in range(1, nrep):
        imp = imp + p[:, r * qblk:(r + 1) * qblk]

    score = sum(_dot(ovl_ref[...], t) for t in _split(imp, 3))
    jb = lax.broadcasted_iota(I32, score.shape, 0)
    ok = jb * SEL_BLK <= tq_row
    cur = tq_row // SEL_BLK
    forced = (jb == 0) | (jb == cur) | (jb == cur - 1)
    score = jnp.where(ok, score + jnp.where(forced, FORCE, 0.0), NEG)
    rank = jnp.zeros(score.shape, F32)
    for j2 in range(nbs):
        cj = score[j2:j2 + 1, :]
        beats = (cj > score) | ((cj == score) & (j2 < jb))
        rank = rank + jnp.where(beats, 1.0, 0.0)
    selneg = jnp.where(ok & (rank < N_SEL), 0.0, NEG).astype(BF16)

    def tile(k_ref, v_ref, bias_fn, kt, carry):
        m, l, acc = carry
        off = pl.multiple_of(kt * PROMPT_KT, PROMPT_KT)
        kpos = off + lax.broadcasted_iota(I32, (PROMPT_KT, 1), 0)
        s_ = _dot(k_ref[0, 0, pl.ds(off, PROMPT_KT), :], q16)
        s_ = s_ + jnp.concatenate([bias_fn(off, kpos)] * nrep, axis=1)
        m_new = jnp.maximum(m, jnp.max(s_, 0, keepdims=True))
        a = jnp.exp(m - m_new)
        p_ = jnp.exp(s_ - m_new)
        v = v_ref[0, 0, :, pl.ds(off, PROMPT_KT)].astype(BF16)
        return m_new, l * a + jnp.sum(p_, 0, keepdims=True), acc * a + _dot(v, p_.astype(BF16))

    def sel_bias(off, kpos):
        return jnp.where(kpos <= tq_row, _dot(exp_ref[pl.ds(off, PROMPT_KT), :], selneg), NEG)

    def win_bias(off, kpos):
        return jnp.where((kpos <= tq_row) & (kpos > tq_row - WINDOW), 0.0, NEG)

    kt_hi = (qb * qblk + qblk - 1) // PROMPT_KT + 1
    kt_win = jnp.maximum(qb * qblk - (WINDOW - 1), 0) // PROMPT_KT
    init = (jnp.full((1, cols), -jnp.inf, F32), jnp.zeros((1, cols), F32), jnp.zeros((NSA_DH, cols), F32))
    c_sel = lax.fori_loop(0, kt_win, lambda kt, c: tile(ks_ref, vs_ref, sel_bias, kt, c), init)
    c_sel, c_win = lax.fori_loop(
        kt_win, kt_hi,
        lambda kt, c: (tile(ks_ref, vs_ref, sel_bias, kt, c[0]), tile(kw_ref, vw_ref, win_bias, kt, c[1])),
        (c_sel, init))
    gate = gate_ref[0, 0, 0]
    o_ref[0, 0, 0] = (gate[0:1, :] * o_cmp + gate[1:2, :] * (c_sel[2] / c_sel[1])
                      + gate[2:3, :] * (c_win[2] / c_win[1]))


def _nsa_prompt(qt, gates, slopes, kc, vct, ksa, kwa, kvt, ovl_t, nbs):
    nb, ng, nqb, qw, cols = qt.shape
    dh = qw - Q_EXTRA
    seq = kvt.shape[3]
    nbc = kc.shape[2]
    qblk = seq // nqb
    nrep = cols // qblk
    qspec = lambda r: pl.BlockSpec((1, 1, 1, r, cols), lambda b, g, i: (b, g, i, 0, 0))
    per_g = lambda r, c: pl.BlockSpec((1, 1, r, c), lambda b, g, i: (b, g, 0, 0))
    tspec = lambda c: pl.BlockSpec((1, 1, dh, seq), lambda b, g, i: (b, c * ng + g, 0, 0))
    expand = (jnp.arange(seq)[:, None] // SEL_BLK == jnp.arange(ovl_t.shape[0])[None, :]).astype(BF16)
    return pl.pallas_call(
        functools.partial(_nsa_prompt_kernel, nbs=nbs, qblk=qblk, nrep=nrep),
        grid=(nb, ng, nqb),
        in_specs=[qspec(qw), qspec(SUBLANES),
                  pl.BlockSpec((1, SUBLANES, cols), lambda b, g, i: (g, 0, 0)),
                  per_g(nbc, dh), per_g(dh, nbc), per_g(seq, qw), tspec(3), per_g(seq, qw), tspec(5),
                  pl.BlockSpec(ovl_t.shape, lambda b, g, i: (0, 0)),
                  pl.BlockSpec(expand.shape, lambda b, g, i: (0, 0))],
        out_specs=qspec(dh),
        out_shape=jax.ShapeDtypeStruct((nb, ng, nqb, dh, cols), F32),
        compiler_params=_cparams(("parallel", "parallel", "arbitrary"), 48),
        name="nsa_prompt",
    )(qt, gates, slopes, kc, vct, ksa, kvt, kwa, kvt, ovl_t, expand)


def _nsa_sample_kernel(pt_ref, q_ref, gate_ref, slope_ref, kc_ref, vc_ref, *refs, nbs, past, nq, nrep, ppt):
    ks_pages, vs_pages = refs[:ppt], refs[ppt:2 * ppt]
    (ksn_ref, vsn_ref, kwp_ref, vwp_ref, kwn_ref, vwn_ref, ovl_ref, o_ref,
     sel_scr, m_scr, l_scr, acc_scr, part_scr) = refs[2 * ppt:]
    keys = ppt * PAGE
    t = pl.program_id(1)
    rows = q_ref.shape[1]
    pw = q_ref.shape[2]
    grows = rows // NSA_KV
    scale = NSA_DH ** -0.5
    q = q_ref[0]
    q16 = q.astype(BF16)
    slope = slope_ref[:, 0:1]
    tq = past + lax.broadcasted_iota(I32, (rows, 1), 0) % nq
    tqf = tq.astype(F32)
    gate = gate_ref[0]

    def logits(qk, kpos):
        return qk * scale - slope * (tqf - kpos.astype(F32))

    @pl.when(t == 0)
    def _():
        kc = kc_ref[0]
        qh, ql = _split(q, 2)
        kh, kl = _split(kc, 2)
        s = (_dot_nt(qh, kh) + _dot_nt(ql, kh) + _dot_nt(qh, kl)) * scale
        e = lax.broadcasted_iota(I32, (1, kc.shape[0]), 1) * CMP_STRIDE + (CMP_BLK - 1)
        valid = e <= tq
        p = jnp.where(valid, _masked_softmax(s - slope * (tqf - e.astype(F32)), valid), 0.0)
        o_cmp = _dot(p.astype(BF16), vc_ref[0].astype(BF16))
        imps = []
        for g in range(NSA_KV):
            a = p[g * grows:g * grows + nq]
            for r in range(1, nrep):
                a = a + p[g * grows + r * nq:g * grows + (r + 1) * nq]
            imps.append(a)
        imp = jnp.concatenate(imps, axis=0)
        tq_s = past + lax.broadcasted_iota(I32, (NSA_KV * nq, 1), 0) % nq
        sel_s = _select_blocks(imp, ovl_ref[...], tq_s, nbs)
        sel = jnp.concatenate([sel_s[g * nq:(g + 1) * nq] for g in range(NSA_KV) for _ in range(nrep)], axis=0)
        sel_scr[...] = sel

        wb = kwp_ref.shape[3]
        npad = LANES - nq
        zpad = jnp.zeros((npad, pw), F32)
        kwn = jnp.concatenate([kwn_ref[0], zpad], axis=0).astype(BF16)
        vwn = jnp.concatenate([vwn_ref[0], zpad], axis=0).astype(BF16)
        qk = jnp.concatenate([_dot(q16, kwp_ref[0, 0].astype(BF16)), _dot_nt(q16, kwn)], axis=1)
        idx = lax.broadcasted_iota(I32, (1, wb + LANES), 1)
        kpos = past - wb + idx
        valid = (idx < wb + nq) & (kpos <= tq) & (kpos > tq - WINDOW)
        pw_ = _masked_softmax(logits(qk, kpos), valid).astype(BF16)
        o_win = _dot_nt(pw_[:, :wb], vwp_ref[0, 0].astype(BF16)) + _dot(pw_[:, wb:], vwn)
        part_scr[...] = gate[:, 0:1] * o_cmp + gate[:, 2:3] * o_win

        kn = jnp.concatenate([ksn_ref[0], zpad], axis=0).astype(BF16)
        vn = jnp.concatenate([vsn_ref[0], zpad], axis=0).astype(BF16)
        idx = lax.broadcasted_iota(I32, (1, LANES), 1)
        kpos = past + idx
        blk = past // SEL_BLK
        valid = (idx < nq) & (kpos <= tq) & (sel[:, blk:blk + 1] > 0.5)
        s = jnp.where(valid, logits(_dot_nt(q16, kn), kpos), NEG)
        m = jnp.max(s, -1, keepdims=True)
        p = jnp.exp(s - m)
        m_scr[...] = m
        l_scr[...] = jnp.sum(p, -1, keepdims=True)
        acc_scr[...] = _dot(p.astype(BF16), vn)

    k = jnp.concatenate([r[0, 0] for r in ks_pages], axis=1).astype(BF16)
    v = jnp.concatenate([r[0, 0] for r in vs_pages], axis=1).astype(BF16)
    kpos = t * keys + lax.broadcasted_iota(I32, (1, keys), 1)
    selk = _expand_sel(sel_scr[...], t * (keys // SEL_BLK), keys)
    s = jnp.where(selk > 0.5, logits(_dot(q16, k), kpos), NEG)
    m = m_scr[...]
    m_new = jnp.maximum(m, jnp.max(s, -1, keepdims=True))
    a = jnp.exp(m - m_new)
    p = jnp.exp(s - m_new)
    m_scr[...] = m_new
    l_scr[...] = l_scr[...] * a + jnp.sum(p, -1, keepdims=True)
    acc_scr[...] = acc_scr[...] * a + _dot_nt(p.astype(BF16), v)

    @pl.when(t == pl.num_programs(1) - 1)
    def _():
        tot = part_scr[...] + gate[:, 1:2] * (acc_scr[...] / l_scr[...])
        o_ref[0] = jnp.concatenate([tot[g * grows:(g + 1) * grows, g * NSA_DH:(g + 1) * NSA_DH]
                                    for g in range(NSA_KV)], axis=0)


def _nsa_sample(qbd, gates, slopes, kc, vc, cache_t, page_table, kv_new, win_t, win_new, ovl, nbs, nq, nrep):
    nb, rows, pw = qbd.shape
    npg = page_table.shape[1]
    past = npg * PAGE
    ppt = max(p for p in (8, 4, 2, 1) if npg % p == 0)
    nt = npg // ppt
    nbc = kc.shape[1]
    wb = win_t.shape[3]
    const = lambda shape: pl.BlockSpec(shape, lambda b, t, pt: (0,) * len(shape))
    per_b = lambda n, w, col=0: pl.BlockSpec((1, n, w), lambda b, t, pt: (b, 0, col))
    page = lambda c, i: pl.BlockSpec((1, 1, pw, PAGE), lambda b, t, pt: (pt[b, ppt * t + i], c, 0, 0))
    wspec = lambda c: pl.BlockSpec((1, 1, pw, wb), lambda b, t, pt: (b, c, 0, 0))
    grid_spec = pltpu.PrefetchScalarGridSpec(
        num_scalar_prefetch=1,
        grid=(nb, nt),
        in_specs=[per_b(rows, pw), per_b(rows, SUBLANES), const((rows, SUBLANES)), per_b(nbc, pw), per_b(nbc, pw)]
                 + [page(2, i) for i in range(ppt)] + [page(3, i) for i in range(ppt)]
                 + [per_b(nq, pw, 2), per_b(nq, pw, 3), wspec(0), wspec(1),
                    per_b(nq, pw, 0), per_b(nq, pw, 1), const(ovl.shape)],
        out_specs=pl.BlockSpec((1, rows, NSA_DH), lambda b, t, pt: (b, 0, 0)),
        scratch_shapes=[pltpu.VMEM((rows, ovl.shape[1]), F32), pltpu.VMEM((rows, 1), F32), pltpu.VMEM((rows, 1), F32),
                        pltpu.VMEM((rows, pw), F32), pltpu.VMEM((rows, pw), F32)])
    return pl.pallas_call(
        functools.partial(_nsa_sample_kernel, nbs=nbs, past=past, nq=nq, nrep=nrep, ppt=ppt),
        grid_spec=grid_spec,
        out_shape=jax.ShapeDtypeStruct((nb, rows, NSA_DH), F32),
        compiler_params=_cparams(("parallel", "arbitrary"), 48),
        name="nsa_sample",
    )(page_table, qbd, gates, slopes, kc, vc, *([cache_t] * (2 * ppt)), kv_new, kv_new, win_t, win_t,
      win_new, win_new, ovl)


def _overlap(nbc, nbs_pad):
    ci = jnp.arange(nbc)[:, None] * CMP_STRIDE
    sj = jnp.arange(nbs_pad)[None, :] * SEL_BLK
    return ((ci < sj + SEL_BLK) & (ci + CMP_BLK > sj)).astype(BF16)


def _alibi_slopes(nh):
    return jnp.exp2(-8.0 * (jnp.arange(nh, dtype=F32) + 1.0) / nh)


def _nsa_in(x, w_in, nh):
    qw = nh * NSA_DH
    kvw = 6 * NSA_KV * NSA_DH
    proj = _matmul(x, w_in[:, :qw + kvw].astype(BF16), tn=512)
    ng = 3 * nh
    gates = _matmul(x, jnp.pad(w_in[:, qw + kvw:], ((0, 0), (0, LANES - ng))).astype(BF16), act="sigmoid")
    kv4w = 4 * NSA_KV * NSA_DH
    return proj[:, :qw], proj[:, qw:qw + kv4w], proj[:, qw + kv4w:], gates[:, :ng]


def _mm_t_kernel(w_ref, x_ref, o_ref):
    o_ref[0] = _dot_nt(w_ref[...], x_ref[...].astype(BF16))


def _matmul_t(x, wt, nb):
    m, k = x.shape
    n = wt.shape[0]
    seq = m // nb
    tl = _row_tile(seq, (1024, 512, 256, 128))
    tn = _row_tile(n, (512, 256, 128))
    return pl.pallas_call(
        _mm_t_kernel,
        grid=(nb, seq // tl, n // tn),
        in_specs=[pl.BlockSpec((tn, k), lambda b, l, j: (j, 0)),
                  pl.BlockSpec((tl, k), lambda b, l, j: (b * (seq // tl) + l, 0))],
        out_specs=pl.BlockSpec((1, tn, tl), lambda b, l, j: (b, j, l)),
        out_shape=jax.ShapeDtypeStruct((nb, n, seq), F32),
        compiler_params=_cparams(("parallel", "parallel", "arbitrary"), 48),
        name="matmul_t",
    )(wt, x)


def _nsa_prompt_mixer(x, nb, seq, prm):
    w_in, pos, w1, b1, w2, b2 = prm
    nh = (w_in.shape[1] - 6 * NSA_KV * NSA_DH) // (NSA_DH + 3)
    nrep = nh // NSA_KV
    pw = NSA_KV * NSA_DH
    qw = nh * NSA_DH
    qblk = 128
    nqb = seq // qblk
    rows = qblk * nrep
    w_tok = jnp.concatenate([w_in[:, :qw + 3 * pw], w_in[:, qw + 4 * pw:qw + 5 * pw]], axis=1)
    proj = _matmul(x, w_tok.astype(BF16), tn=512)
    kvt = _matmul_t(x, w_in[:, qw:qw + 6 * pw].T.astype(BF16), nb)
    ng = 3 * nh
    gates = _matmul(x, jnp.pad(w_in[:, qw + 6 * pw:], ((0, 0), (0, LANES - ng))).astype(BF16), act="sigmoid")[:, :ng]
    npg = seq // PAGE
    cmp_out = _compress(proj.reshape(nb * npg, PAGE, proj.shape[1]), jnp.arange(nb * npg, dtype=I32).reshape(nb, npg),
                        pos, w1, b1, w2, b2, col0=qw)
    q = proj[:, :qw] * (NSA_DH ** -0.5)
    qt = q.reshape(nb, nqb, qblk, NSA_KV, nrep, NSA_DH).transpose(0, 3, 1, 5, 4, 2).reshape(nb, NSA_KV, nqb, NSA_DH, rows)
    slf = jnp.repeat(_alibi_slopes(nh).reshape(NSA_KV, nrep), qblk, axis=1)
    s_hi = slf.astype(BF16).astype(F32)
    s_lo = (slf - s_hi).astype(BF16).astype(F32)
    extra = jnp.stack([POS_SPLIT * s_hi, s_hi, POS_SPLIT * s_lo, s_lo] + [jnp.zeros_like(slf)] * (Q_EXTRA - 4), axis=1)
    qt = jnp.concatenate([qt, jnp.broadcast_to(extra[None, :, None], (nb, NSA_KV, nqb, Q_EXTRA, rows))], axis=3)
    gs = gates.reshape(nb, nqb, qblk, 3, NSA_KV, nrep).transpose(0, 4, 1, 3, 5, 2).reshape(nb, NSA_KV, nqb, 3, rows)
    gs = jnp.pad(gs, ((0, 0),) * 3 + ((0, SUBLANES - 3), (0, 0)))
    sl = jnp.pad(slf[:, None, :], ((0, 0), (0, SUBLANES - 1), (0, 0)))
    t = jnp.arange(seq, dtype=I32)
    pos_cols = jnp.stack([t // POS_SPLIT, t % POS_SPLIT] * 2 + [jnp.zeros_like(t)] * (Q_EXTRA - 4), axis=1).astype(F32)
    pos_cols = jnp.broadcast_to(pos_cols[None, None], (nb, NSA_KV, seq, Q_EXTRA))
    with_pos = lambda cols: jnp.concatenate(
        [cols.reshape(nb, seq, NSA_KV, NSA_DH).transpose(0, 2, 1, 3), pos_cols], axis=-1).astype(BF16)
    ksa = with_pos(proj[:, qw + 2 * pw:qw + 3 * pw])
    kwa = with_pos(proj[:, qw + 3 * pw:qw + 4 * pw])
    nbs = -(-seq // SEL_BLK)
    nbs_r = -(-nbs // SUBLANES) * SUBLANES
    o = _nsa_prompt(qt, gs, sl, cmp_out[:, 0], cmp_out[:, 1].transpose(0, 1, 3, 2), ksa, kwa,
                    kvt.reshape(nb, 6 * NSA_KV, NSA_DH, seq), _overlap(seq // CMP_STRIDE, nbs_r).T, nbs)
    o = o.reshape(nb, NSA_KV, nqb, NSA_DH, nrep, qblk).transpose(0, 2, 5, 1, 4, 3).reshape(nb * seq, nh * NSA_DH)
    kv6 = kvt.reshape(nb, 6, NSA_KV, NSA_DH, seq).transpose(0, 4, 1, 2, 3)
    return o, kv6[:, :, :4], kv6[:, :, 4:]


def _nsa_sample_mixer(x, nb, nq, cache, page_table, win_buf, prm):
    w_in, pos, w1, b1, w2, b2 = prm
    nh = (w_in.shape[1] - 6 * NSA_KV * NSA_DH) // (NSA_DH + 3)
    nrep = nh // NSA_KV
    pw = NSA_KV * NSA_DH
    npg = page_table.shape[1]
    past = npg * PAGE
    q, kv4, win, gates = _nsa_in(x, w_in, nh)
    n_pool = cache.shape[0]
    cache_t = cache.transpose(0, 2, 3, 4, 1).reshape(n_pool, 4, pw, PAGE)
    win_t = win_buf.transpose(0, 2, 3, 4, 1).reshape(nb, 2, pw, win_buf.shape[1])
    cmp_out = _compress(cache_t, page_table, pos, w1, b1, w2, b2, token_minor=True)
    nsub = cmp_out.shape[3]
    kc = cmp_out[:, 0].transpose(0, 2, 1, 3).reshape(nb, nsub, pw)
    vc = cmp_out[:, 1].transpose(0, 2, 1, 3).reshape(nb, nsub, pw)
    rows = NSA_KV * nrep * nq
    q5 = q.reshape(nb, nq, NSA_KV, nrep, NSA_DH).transpose(0, 2, 3, 1, 4)
    qbd = jnp.einsum("bgrqd,gh->bgrqhd", q5, jnp.eye(NSA_KV, dtype=F32)).reshape(nb, rows, pw)
    gs = gates.reshape(nb, nq, 3, NSA_KV, nrep).transpose(0, 3, 4, 1, 2).reshape(nb, rows, 3)
    gs = jnp.pad(gs, ((0, 0), (0, 0), (0, SUBLANES - 3)))
    sl = jnp.repeat(_alibi_slopes(nh), nq)[:, None]
    sl = jnp.pad(sl, ((0, 0), (0, SUBLANES - 1)))
    nbs = -(-(past + nq) // SEL_BLK)
    nbs_pad = -(-nbs // LANES) * LANES
    o = _nsa_sample(qbd, gs, sl, kc, vc, cache_t, page_table, kv4.reshape(nb, nq, 4 * pw),
                    win_t, win.reshape(nb, nq, 2 * pw), _overlap(nsub, nbs_pad), nbs, nq, nrep)
    o = o.reshape(nb, NSA_KV, nrep, nq, NSA_DH).transpose(0, 3, 1, 2, 4).reshape(nb * nq, nh * NSA_DH)
    return o, kv4.reshape(nb, nq, 4, NSA_KV, NSA_DH), win.reshape(nb, nq, 2, NSA_KV, NSA_DH)


def kernel(x_prompt, x_sample, state_ssm, state_conv, cache_kv, state_win, page_table, ln_mix_g, ln_mix_b, ln_ffn_g, ln_ffn_b, ssd_w_in, ssd_conv_w, ssd_conv_b, ssd_dt_bias, ssd_a_log, ssd_d, ssd_norm_w, ssd_w_out, gm_w_in, gm_b_in, gm_ln_g, gm_ln_b, gm_w_s, gm_b_s, gm_w_out, nsa_w_in, nsa_cmp_pos, nsa_cmp_w1, nsa_cmp_b1, nsa_cmp_w2, nsa_cmp_b2, nsa_w_out, ffn_w_gate, ffn_w_up, ffn_w_down, moe_w_router, moe_b_router, moe_w_gate, moe_w_up, moe_w_down):
    depth = ln_mix_g.shape[0]
    alpha = (2 * depth) ** 0.25
    bp, seq, d = x_prompt.shape
    db, dseq, _ = x_sample.shape
    xp = x_prompt.reshape(bp * seq, d)
    xs = x_sample.reshape(db * dseq, d)
    outs = {k: [] for k in ("ssm_p", "conv_p", "ssm_s", "conv_s", "gmv_s", "kv_p", "win_p", "kv_s", "win_s")}
    row = lambda a, i: a[i][None, :]
    for l in range(depth):
        kind, j = l % N_MIXERS, l // N_MIXERS
        mg, mb = row(ln_mix_g, l), row(ln_mix_b, l)
        if kind == 0:
            prm = (ssd_w_in[j], ssd_conv_w[j], ssd_conv_b[j], ssd_dt_bias[j], ssd_a_log[j], ssd_d[j], ssd_norm_w[j])
            cdim = ssd_conv_w.shape[2]
            w_out = ssd_w_out[j].astype(BF16)
            yp, cp, hp = _ssd_mixer(xp, bp, SSD_CHUNK, jnp.zeros((bp, 8, cdim), F32),
                                    jnp.zeros((bp,) + state_ssm.shape[2:], F32), prm)
            ys, cs, hs = _ssd_mixer(xs, db, dseq, jnp.pad(state_conv[j], ((0, 0), (8 - (SSD_CONV - 1), 0), (0, 0))),
                                    state_ssm[j], prm)
            outs["ssm_p"].append(hp), outs["conv_p"].append(cp), outs["ssm_s"].append(hs), outs["conv_s"].append(cs)
            xp = _mm_res_ln(yp, xp, w_out, mg, mb, alpha)
            xs = _mm_res_ln(ys, xs, w_out, mg, mb, alpha)
        elif kind == 1:
            prm = (gm_w_in[j], gm_b_in[j], gm_ln_g[j], gm_ln_b[j], gm_w_s[j], gm_b_s[j], gm_w_out[j])
            xp, _ = _gmlp_mixer(xp, bp, seq, prm, mg, mb, alpha)
            xs, vs = _gmlp_mixer(xs, db, dseq, prm, mg, mb, alpha)
            outs["gmv_s"].append(vs.reshape(db, dseq, -1))
        else:
            prm = (nsa_w_in[j], nsa_cmp_pos[j], nsa_cmp_w1[j], nsa_cmp_b1[j], nsa_cmp_w2[j], nsa_cmp_b2[j])
            w_out = nsa_w_out[j].astype(BF16)
            op, kvp, wp = _nsa_prompt_mixer(xp, bp, seq, prm)
            os_, kvs, wsn = _nsa_sample_mixer(xs, db, dseq, cache_kv[j], page_table, state_win[j], prm)
            outs["kv_p"].append(kvp), outs["win_p"].append(wp[:, seq - min(WINDOW, seq):])
            outs["kv_s"].append(kvs), outs["win_s"].append(wsn)
            xp = _mm_res_ln(op, xp, w_out, mg, mb, alpha)
            xs = _mm_res_ln(os_, xs, w_out, mg, mb, alpha)
        f = l // 2
        fg, fb = row(ln_ffn_g, l), row(ln_ffn_b, l)
        if l % 2 == 0:
            wg, wu, wd = ffn_w_gate[f].astype(BF16), ffn_w_up[f].astype(BF16), ffn_w_down[f].astype(BF16)
            xp = _swiglu_res_ln(xp, wg, wu, wd, fg, fb, alpha)
            xs = _swiglu_res_ln(xs, wg, wu, wd, fg, fb, alpha)
        else:
            wg, wu, wd = moe_w_gate[f].astype(BF16), moe_w_up[f].astype(BF16), moe_w_down[f].astype(BF16)
            xp = _moe_res_ln(xp, moe_w_router[f], moe_b_router[f], wg, wu, wd, fg, fb, alpha)
            xs = _moe_res_ln(xs, moe_w_router[f], moe_b_router[f], wg, wu, wd, fg, fb, alpha)
    st = lambda k: jnp.stack(outs[k])
    return (xp.reshape(bp, seq, d), xs.reshape(db, dseq, d), st("ssm_p"), st("conv_p"), st("ssm_s"), st("conv_s"),
            st("gmv_s"), st("kv_p"), st("win_p"), st("kv_s"), st("win_s"))
```

```python
import functools

import jax
import jax.numpy as jnp
from jax import lax
from jax.experimental import pallas as pl
from jax.experimental.pallas import tpu as pltpu

F32 = jnp.float32
BF16 = jnp.bfloat16
I32 = jnp.int32

LN_EPS = 1e-5
NEG = -1e30
N_MIXERS = 3
SSD_P = 64
SSD_N = 128
SSD_G = 4
SSD_CONV = 4
SSD_CHUNK = 128
GM_CHUNK = 128
GM_G = 8
NSA_KV = 4
NSA_DH = 64
CMP_BLK = 32
CMP_STRIDE = 16
SEL_BLK = 64
N_SEL = 16
WINDOW = 512
FORCE = 1e4
TOP_K = 2

LANES = 128
SUBLANES = 8
MIB = 1 << 20


def _cparams(sem, vmem_mib):
    return pltpu.CompilerParams(dimension_semantics=sem, vmem_limit_bytes=vmem_mib * MIB)


def _row_tile(m, cands=(1024, 512, 256, 128, 64, 32, 16, 8)):
    for c in cands:
        if m % c == 0:
            return c
    raise ValueError(f"no row tile for {m}")


def _ln(v, g, b):
    mu = jnp.mean(v, -1, keepdims=True)
    d = v - mu
    var = jnp.mean(d * d, -1, keepdims=True)
    return d * lax.rsqrt(var + LN_EPS) * g + b


def _split(x, n):
    out = []
    r = x
    for k in range(n):
        h = r.astype(BF16)
        out.append(h)
        if k + 1 < n:
            r = r - h.astype(F32)
    return out


def _dot(a, b):
    return jnp.dot(a, b, preferred_element_type=F32)


def _dot_nt(a, b):
    return lax.dot_general(a, b, (((1,), (1,)), ((), ())), preferred_element_type=F32)


def _dot_tn(a, b):
    return lax.dot_general(a, b, (((0,), (0,)), ((), ())), preferred_element_type=F32)


def _silu(x):
    return x * jax.nn.sigmoid(x)


def _gelu_tanh(x):
    return 0.5 * x * (1.0 + jnp.tanh(0.7978845608028654 * (x + 0.044715 * (x * x * x))))


def _mm_kernel(x_ref, w_ref, b_ref, o_ref, *, act):
    acc = _dot(x_ref[...].astype(BF16), w_ref[...]) + b_ref[...]
    if act == "sigmoid":
        acc = jax.nn.sigmoid(acc)
    o_ref[...] = acc.astype(o_ref.dtype)


def _matmul(x, w, b=None, act=None, tn=None):
    m, k = x.shape
    n = w.shape[1]
    tm = _row_tile(m)
    tn = n if tn is None else tn
    if b is None:
        b = jnp.zeros((1, n), F32)
    return pl.pallas_call(
        functools.partial(_mm_kernel, act=act),
        grid=(m // tm, n // tn),
        in_specs=[pl.BlockSpec((tm, k), lambda i, j: (i, 0)),
                  pl.BlockSpec((k, tn), lambda i, j: (0, j)),
                  pl.BlockSpec((1, tn), lambda i, j: (0, j))],
        out_specs=pl.BlockSpec((tm, tn), lambda i, j: (i, j)),
        out_shape=jax.ShapeDtypeStruct((m, n), F32),
        compiler_params=_cparams(("parallel", "arbitrary"), 48),
        name="matmul",
    )(x, w, b)


def _mm_res_ln_kernel(y_ref, x_ref, w_ref, g_ref, b_ref, o_ref, *, alpha):
    f = _dot(y_ref[...].astype(BF16), w_ref[...])
    o_ref[...] = _ln(alpha * x_ref[...] + f, g_ref[...], b_ref[...])


def _mm_res_ln(y, x, w, g, b, alpha):
    m, k = y.shape
    d = x.shape[1]
    tm = _row_tile(m, (512, 256, 128))
    return pl.pallas_call(
        functools.partial(_mm_res_ln_kernel, alpha=alpha),
        grid=(m // tm,),
        in_specs=[pl.BlockSpec((tm, k), lambda i: (i, 0)),
                  pl.BlockSpec((tm, d), lambda i: (i, 0)),
                  pl.BlockSpec((k, d), lambda i: (0, 0)),
                  pl.BlockSpec((1, d), lambda i: (0, 0)),
                  pl.BlockSpec((1, d), lambda i: (0, 0))],
        out_specs=pl.BlockSpec((tm, d), lambda i: (i, 0)),
        out_shape=jax.ShapeDtypeStruct((m, d), F32),
        compiler_params=_cparams(("parallel",), 48),
        name="mm_res_ln",
    )(y, x, w, g, b)


def _swiglu_kernel(x_ref, wg_ref, wu_ref, wd_ref, g_ref, b_ref, o_ref, acc_ref, xb_ref, *, alpha):
    f = pl.program_id(1)

    @pl.when(f == 0)
    def _():
        acc_ref[...] = jnp.zeros_like(acc_ref)
        xb_ref[...] = x_ref[...].astype(BF16)

    xb = xb_ref[...]
    h = _silu(_dot(xb, wg_ref[...])) * _dot(xb, wu_ref[...])
    acc_ref[...] += _dot(h.astype(BF16), wd_ref[...])

    @pl.when(f == pl.num_programs(1) - 1)
    def _():
        o_ref[...] = _ln(alpha * x_ref[...] + acc_ref[...], g_ref[...], b_ref[...])


def _ff_tile(dff):
    for c in (512, 256, 128):
        if dff % c == 0:
            return c
    return dff


def _swiglu_res_ln(x, wg, wu, wd, g, b, alpha):
    m, d = x.shape
    dff = wg.shape[1]
    tm = _row_tile(m)
    tf = _ff_tile(dff)
    return pl.pallas_call(
        functools.partial(_swiglu_kernel, alpha=alpha),
        grid=(m // tm, dff // tf),
        in_specs=[pl.BlockSpec((tm, d), lambda i, f: (i, 0)),
                  pl.BlockSpec((d, tf), lambda i, f: (0, f)),
                  pl.BlockSpec((d, tf), lambda i, f: (0, f)),
                  pl.BlockSpec((tf, d), lambda i, f: (f, 0)),
                  pl.BlockSpec((1, d), lambda i, f: (0, 0)),
                  pl.BlockSpec((1, d), lambda i, f: (0, 0))],
        out_specs=pl.BlockSpec((tm, d), lambda i, f: (i, 0)),
        out_shape=jax.ShapeDtypeStruct((m, d), F32),
        scratch_shapes=[pltpu.VMEM((tm, d), F32), pltpu.VMEM((tm, d), BF16)],
        compiler_params=_cparams(("parallel", "arbitrary"), 48),
        name="swiglu_res_ln",
    )(x, wg, wu, wd, g, b)


def _ssd_kernel(xbc_ref, z_ref, dt_ref, conv0_ref, h0_ref, cw_ref, cb_ref, dtb_ref, alog_ref,
                dexp_ref, nw_ref, e_ref, ltri_ref, y_ref, h_ref, xp_scr, dtp_scr, yacc_scr,
                *, lv, n_heads):
    q = SSD_CHUNK
    di = n_heads * SSD_P
    gn = SSD_G * SSD_N
    hpg = n_heads // SSD_G
    gw = hpg * SSD_P
    ci = pl.program_id(1)

    @pl.when(ci == 0)
    def _():
        xp_scr[0:8, :] = conv0_ref[0]
        h_ref[0] = h0_ref[0]

    @pl.when(ci > 0)
    def _():
        xp_scr[0:8, :] = xp_scr[q:q + 8, :]

    if lv < q:
        xp_scr[8 + lv:, :] = jnp.zeros((q - lv, xp_scr.shape[1]), F32)
        dtp_scr[...] = jnp.zeros_like(dtp_scr)
        dtp_scr[0:lv, :] = dt_ref[...]
        dt_raw = dtp_scr[...]
    else:
        dt_raw = dt_ref[...]
    xp_scr[8:8 + lv, :] = xbc_ref[...]

    conv = cb_ref[...]
    for k in range(SSD_CONV):
        conv = conv + cw_ref[k:k + 1, :] * xp_scr[5 + k:5 + k + q, :]
    xc = _silu(conv)
    row = lax.broadcasted_iota(I32, (q, LANES), 0)
    lane = lax.broadcasted_iota(I32, (q, LANES), 1)
    if lv < q:
        xc = jnp.where(lax.broadcasted_iota(I32, xc.shape, 0) < lv, xc, 0.0)

    v = dt_raw + dtb_ref[...]
    dt = jnp.maximum(v, 0.0) + jnp.log1p(jnp.exp(-jnp.abs(v)))
    dt = jnp.where((row < lv) & (lane < n_heads), dt, 0.0)
    adt = dt * (-jnp.exp(alog_ref[...]))
    acum = sum(_dot(ltri_ref[...], t) for t in _split(adt, 3))
    acum_t = acum.T
    dt_t = dt.T
    ea = jnp.exp(acum)
    w = dt * jnp.exp(acum[q - 1:q, :] - acum)
    w_exp = sum(_dot(t, e_ref[...]) for t in _split(w, 2))
    ea_exp = sum(_dot(t, e_ref[...]) for t in _split(ea[:lv], 2))
    cdm = jnp.exp(jnp.broadcast_to(acum_t[:, q - 1:q], (LANES, LANES)))

    xs = xc[:, :di]
    xd = (xs * w_exp).astype(BF16)
    causal = (lax.broadcasted_iota(I32, (lv, q), 0) >= lax.broadcasted_iota(I32, (lv, q), 1))
    lo = lane < SSD_P

    for g in range(SSD_G):
        bg = xc[:, di + g * SSD_N:di + (g + 1) * SSD_N].astype(BF16)
        cg = xc[:lv, di + gn + g * SSD_N:di + gn + (g + 1) * SSD_N].astype(BF16)
        cbm = _dot_nt(cg, bg)
        hg = h_ref[0, g * gw:(g + 1) * gw, :].astype(BF16)
        yoff = _dot_nt(cg, hg)
        st = _dot_tn(xd[:, g * gw:(g + 1) * gw], bg)
        for pr in range(hpg // 2):
            col = g * gw + pr * LANES
            ms = []
            for hh in (2 * pr, 2 * pr + 1):
                h = g * hpg + hh
                seg = acum[:lv, h:h + 1] - acum_t[h:h + 1, :]
                dec = jnp.exp(jnp.where(causal, seg, -jnp.inf))
                ms.append((cbm * dec * dt_t[h:h + 1, :]).astype(BF16))
            xpair = xs[:, col:col + LANES]
            rhs = jnp.concatenate([jnp.where(lo, xpair, 0.0), jnp.where(lo, 0.0, xpair)], axis=0).astype(BF16)
            yd = _dot(jnp.concatenate(ms, axis=1), rhs)
            yacc_scr[:, col:col + LANES] = (yd + yoff[:, pr * LANES:(pr + 1) * LANES] * ea_exp[:, col:col + LANES]
                                            + dexp_ref[:, col:col + LANES] * xs[:lv, col:col + LANES])
        for hh in range(hpg):
            h = g * hpg + hh
            r0 = h * SSD_P
            h_ref[0, r0:r0 + SSD_P, :] = (h_ref[0, r0:r0 + SSD_P, :] * cdm[h:h + 1, :]
                                          + st[hh * SSD_P:(hh + 1) * SSD_P, :])

    y = yacc_scr[...] * _silu(z_ref[...])
    ng = di // SSD_G
    for g in range(SSD_G):
        yg = y[:, g * ng:(g + 1) * ng]
        ms_ = jnp.mean(yg * yg, -1, keepdims=True)
        yn = yg * lax.rsqrt(ms_ + LN_EPS) * nw_ref[:, g * ng:(g + 1) * ng]
        y_ref[:, g * ng:(g + 1) * ng] = yn.astype(y_ref.dtype)


def _ssd_scan(xbc, z, dt, conv0, h0, cw, cb, dtb, alog, dexp, nw, e, ltri, nb, lv):
    m, cdim = xbc.shape
    di = z.shape[1]
    n_heads = di // SSD_P
    nc = m // (nb * lv)
    y_dtype = BF16 if lv % (2 * SUBLANES) == 0 else F32
    return pl.pallas_call(
        functools.partial(_ssd_kernel, lv=lv, n_heads=n_heads),
        grid=(nb, nc),
        in_specs=[pl.BlockSpec((lv, cdim), lambda b, c: (b * nc + c, 0)),
                  pl.BlockSpec((lv, di), lambda b, c: (b * nc + c, 0)),
                  pl.BlockSpec((lv, LANES), lambda b, c: (b * nc + c, 0)),
                  pl.BlockSpec((1, 8, cdim), lambda b, c: (b, 0, 0)),
                  pl.BlockSpec((1, di, SSD_N), lambda b, c: (b, 0, 0)),
                  pl.BlockSpec((SSD_CONV, cdim), lambda b, c: (0, 0)),
                  pl.BlockSpec((1, cdim), lambda b, c: (0, 0)),
                  pl.BlockSpec((1, LANES), lambda b, c: (0, 0)),
                  pl.BlockSpec((1, LANES), lambda b, c: (0, 0)),
                  pl.BlockSpec((1, di), lambda b, c: (0, 0)),
                  pl.BlockSpec((1, di), lambda b, c: (0, 0)),
                  pl.BlockSpec((LANES, di), lambda b, c: (0, 0)),
                  pl.BlockSpec((SSD_CHUNK, SSD_CHUNK), lambda b, c: (0, 0))],
        out_specs=[pl.BlockSpec((lv, di), lambda b, c: (b * nc + c, 0)),
                   pl.BlockSpec((1, di, SSD_N), lambda b, c: (b, 0, 0))],
        out_shape=[jax.ShapeDtypeStruct((m, di), y_dtype),
                   jax.ShapeDtypeStruct((nb, di, SSD_N), F32)],
        scratch_shapes=[pltpu.VMEM((SSD_CHUNK + 8, cdim), F32),
                        pltpu.VMEM((SSD_CHUNK, LANES), F32),
                        pltpu.VMEM((lv, di), F32)],
        compiler_params=_cparams(("parallel", "arbitrary"), 56),
        name="ssd_scan",
    )(xbc, z, dt, conv0, h0, cw, cb, dtb, alog, dexp, nw, e, ltri)


def _ssd_mixer(x, nb, lv, conv0, h0, prm):
    w_in, conv_w, conv_b, dt_bias, a_log, d_skip, norm_w = prm
    di = norm_w.shape[0]
    cdim = conv_w.shape[1]
    n_heads = di // SSD_P
    seq = x.shape[0] // nb
    w_z = w_in[:, :di].astype(BF16)
    w_x = w_in[:, di:di + cdim].astype(BF16)
    w_dt = jnp.pad(w_in[:, di + cdim:], ((0, 0), (0, LANES - n_heads))).astype(BF16)
    z = _matmul(x, w_z, tn=1024)
    xbc = _matmul(x, w_x, tn=1024)
    dt = _matmul(x, w_dt)
    pad1 = lambda a: jnp.pad(a.astype(F32), (0, LANES - n_heads))[None, :]
    e = (jnp.arange(di)[None, :] // SSD_P == jnp.arange(LANES)[:, None]).astype(BF16)
    ltri = (jnp.arange(SSD_CHUNK)[:, None] >= jnp.arange(SSD_CHUNK)[None, :]).astype(BF16)
    y, h_new = _ssd_scan(xbc, z, dt, conv0, h0.reshape(nb, di, SSD_N), conv_w, conv_b[None, :],
                         pad1(dt_bias), pad1(a_log), jnp.repeat(d_skip, SSD_P)[None, :], norm_w[None, :],
                         e, ltri, nb, lv)
    conv_new = xbc.reshape(nb, seq, cdim)[:, seq - (SSD_CONV - 1):]
    return y, conv_new, h_new.reshape(nb, n_heads, SSD_P, SSD_N)


def _gm_in_kernel(x_ref, w_ref, b_ref, g_ref, bb_ref, u_ref, v_ref):
    h = _gelu_tanh(_dot(x_ref[...].astype(BF16), w_ref[...]) + b_ref[...])
    d = u_ref.shape[1]
    u_ref[...] = h[:, :d]
    v_ref[...] = _ln(h[:, d:], g_ref[...], bb_ref[...])


def _gm_in(x, w, b, g, bb):
    m, k = x.shape
    d = w.shape[1] // 2
    tm = _row_tile(m, (512, 256))
    return pl.pallas_call(
        _gm_in_kernel,
        grid=(m // tm,),
        in_specs=[pl.BlockSpec((tm, k), lambda i: (i, 0)),
                  pl.BlockSpec((k, 2 * d), lambda i: (0, 0)),
                  pl.BlockSpec((1, 2 * d), lambda i: (0, 0)),
                  pl.BlockSpec((1, d), lambda i: (0, 0)),
                  pl.BlockSpec((1, d), lambda i: (0, 0))],
        out_specs=[pl.BlockSpec((tm, d), lambda i: (i, 0)), pl.BlockSpec((tm, d), lambda i: (i, 0))],
        out_shape=[jax.ShapeDtypeStruct((m, d), F32), jax.ShapeDtypeStruct((m, d), F32)],
        compiler_params=_cparams(("parallel",), 48),
        name="gm_in",
    )(x, w, b, g, bb)


def _gm_out_kernel(u_ref, v_ref, x_ref, s_ref, sb_ref, w_ref, g_ref, b_ref, o_ref, gated_scr, *, alpha, r):
    tb = u_ref.shape[0]
    gd = u_ref.shape[1] // GM_G
    for s in range(tb // r):
        rows = slice(s * r, (s + 1) * r)
        for g in range(GM_G):
            cols = slice(g * gd, (g + 1) * gd)
            mixed = _dot(s_ref[g], v_ref[rows, cols].astype(BF16)) + sb_ref[rows, cols]
            gated_scr[rows, cols] = (u_ref[rows, cols] * mixed).astype(BF16)
    f = _dot(gated_scr[...], w_ref[...])
    o_ref[...] = _ln(alpha * x_ref[...] + f, g_ref[...], b_ref[...])


def _gm_out(u, v, x, smat, sbias, w, g, b, alpha, tb):
    m, d = u.shape
    r = smat.shape[1]
    return pl.pallas_call(
        functools.partial(_gm_out_kernel, alpha=alpha, r=r),
        grid=(m // tb,),
        in_specs=[pl.BlockSpec((tb, d), lambda i: (i, 0)),
                  pl.BlockSpec((tb, d), lambda i: (i, 0)),
                  pl.BlockSpec((tb, d), lambda i: (i, 0)),
                  pl.BlockSpec((GM_G, r, r), lambda i: (0, 0, 0)),
                  pl.BlockSpec((tb, d), lambda i: (0, 0)),
                  pl.BlockSpec((d, d), lambda i: (0, 0)),
                  pl.BlockSpec((1, d), lambda i: (0, 0)),
                  pl.BlockSpec((1, d), lambda i: (0, 0))],
        out_specs=pl.BlockSpec((tb, d), lambda i: (i, 0)),
        out_shape=jax.ShapeDtypeStruct((m, d), F32),
        scratch_shapes=[pltpu.VMEM((tb, d), BF16)],
        compiler_params=_cparams(("parallel",), 48),
        name="gm_out",
    )(u, v, x, smat, sbias, w, g, b)


def _gmlp_mixer(x, nb, seq, prm, ln_g, ln_b, alpha):
    w_in, b_in, g_in, bb_in, w_s, b_s, w_out = prm
    d = w_out.shape[0]
    gd = d // GM_G
    u, v = _gm_in(x, w_in.astype(BF16), b_in[None, :], g_in[None, :], bb_in[None, :])
    ws = jnp.tril(w_s)
    if seq % GM_CHUNK == 0:
        tb = 512
        smat = ws.astype(BF16)
        bias_rows = jnp.repeat(b_s.T, gd, axis=1)
        sbias = jnp.tile(bias_rows, (tb // GM_CHUNK, 1))
    else:
        tb = nb * seq
        smat = jnp.einsum("ab,gts->gatbs", jnp.eye(nb, dtype=F32), ws[:, :seq, :seq]).reshape(GM_G, tb, tb).astype(BF16)
        sbias = jnp.tile(jnp.repeat(b_s.T[:seq], gd, axis=1), (nb, 1))
    out = _gm_out(u, v, x, smat, sbias, w_out.astype(BF16), ln_g, ln_b, alpha, tb)
    return out, v


def _router_kernel(x_ref, wh_ref, wl_ref, b_ref, eid_ref, gate_ref):
    x = x_ref[...]
    xh, xl = _split(x, 2)
    logits = _dot(xh, wh_ref[...]) + _dot(xl, wh_ref[...]) + _dot(xh, wl_ref[...]) + b_ref[...]
    lane = lax.broadcasted_iota(I32, logits.shape, 1)
    m1 = jnp.max(logits, -1, keepdims=True)
    i1 = jnp.min(jnp.where(logits == m1, lane, LANES), -1, keepdims=True)
    rest = jnp.where(lane == i1, NEG * 2, logits)
    m2 = jnp.max(rest, -1, keepdims=True)
    i2 = jnp.min(jnp.where(rest == m2, lane, LANES), -1, keepdims=True)
    e = jnp.exp(m2 - m1)
    g1 = 1.0 / (1.0 + e)
    eid_ref[...] = jnp.where(lane == 0, i1, jnp.where(lane == 1, i2, 0))
    gate_ref[...] = jnp.where(lane == 0, g1, jnp.where(lane == 1, e * g1, 0.0))


def _router(x, w_router, b_router):
    m, d = x.shape
    ne = w_router.shape[1]
    w = jnp.pad(w_router, ((0, 0), (0, LANES - ne)))
    wh = w.astype(BF16)
    wl = (w - wh.astype(F32)).astype(BF16)
    b = jnp.pad(b_router.astype(F32), (0, LANES - ne), constant_values=NEG)[None, :]
    tm = _row_tile(m, (512, 256))
    return pl.pallas_call(
        _router_kernel,
        grid=(m // tm,),
        in_specs=[pl.BlockSpec((tm, d), lambda i: (i, 0)),
                  pl.BlockSpec((d, LANES), lambda i: (0, 0)),
                  pl.BlockSpec((d, LANES), lambda i: (0, 0)),
                  pl.BlockSpec((1, LANES), lambda i: (0, 0))],
        out_specs=[pl.BlockSpec((tm, LANES), lambda i: (i, 0)), pl.BlockSpec((tm, LANES), lambda i: (i, 0))],
        out_shape=[jax.ShapeDtypeStruct((m, LANES), I32), jax.ShapeDtypeStruct((m, LANES), F32)],
        compiler_params=_cparams(("parallel",), 32),
        name="router",
    )(x, wh, wl, b)


def _row_copy(src_hbm, dst, s_row, d_row, sem):
    return pltpu.make_async_copy(src_hbm.at[pl.ds(s_row, 1), :], dst.at[pl.ds(d_row, 1), :], sem)


def _row_gather(src_hbm, dst, idx_ref, n, per_row, sem, start):
    if not start:
        for k in range(per_row):
            pltpu.make_async_copy(src_hbm.at[pl.ds(0, n), :], dst.at[k], sem).wait()
        return

    def body(j, c):
        for k in range(per_row):
            _row_copy(src_hbm, dst.at[k], idx_ref[0, 0, per_row * j + k], j, sem).start()
        return c

    lax.fori_loop(0, n, body, 0, unroll=8)


def _moe_ffn_kernel(be_ref, nu_ref, idx_ref, idxn_ref, x_hbm, wg_ref, wu_ref, wd_ref, o_ref,
                    acc_ref, xg_ref, xb_ref, sem, *, tb):
    i = pl.program_id(0)
    f = pl.program_id(1)
    last = pl.num_programs(1) - 1
    n_used = nu_ref[0]
    used = i < n_used
    slot = i % 2

    @pl.when((f == 0) & (i == 0))
    def _():
        _row_gather(x_hbm, xg_ref.at[0], idx_ref, tb, 1, sem.at[0], True)

    @pl.when((f == 0) & (i + 1 < n_used))
    def _():
        _row_gather(x_hbm, xg_ref.at[1 - slot], idxn_ref, tb, 1, sem.at[1 - slot], True)

    @pl.when(used & (f == 0))
    def _():
        _row_gather(x_hbm, xg_ref.at[slot], idx_ref, tb, 1, sem.at[slot], False)
        acc_ref[...] = jnp.zeros_like(acc_ref)
        xb_ref[...] = xg_ref[slot, 0].astype(BF16)

    @pl.when(used)
    def _():
        xb = xb_ref[...]
        h = _silu(_dot(xb, wg_ref[0])) * _dot(xb, wu_ref[0])
        acc_ref[...] += _dot(h.astype(BF16), wd_ref[0])

    @pl.when(used & (f == last))
    def _():
        o_ref[...] = acc_ref[...]

    @pl.when(jnp.logical_not(used) & (f == last))
    def _():
        o_ref[...] = jnp.zeros_like(o_ref)


def _moe_ffn(x, buf_tok, blk_e, n_used, wg, wu, wd, tb):
    d = x.shape[1]
    nblk = buf_tok.shape[0] // tb
    dff = wg.shape[2]
    tf = _ff_tile(dff)
    nf = dff // tf

    def fe(i, f, be, nu):
        return jnp.where(i < nu[0], f, nf - 1)

    grid_spec = pltpu.PrefetchScalarGridSpec(
        num_scalar_prefetch=2,
        grid=(nblk, nf),
        in_specs=[pl.BlockSpec((1, 1, tb), lambda i, f, be, nu: (i, 0, 0), memory_space=pltpu.SMEM),
                  pl.BlockSpec((1, 1, tb), lambda i, f, be, nu: (jnp.minimum(i + 1, nblk - 1), 0, 0),
                               memory_space=pltpu.SMEM),
                  pl.BlockSpec(memory_space=pl.ANY),
                  pl.BlockSpec((1, d, tf), lambda i, f, be, nu: (be[i], 0, fe(i, f, be, nu))),
                  pl.BlockSpec((1, d, tf), lambda i, f, be, nu: (be[i], 0, fe(i, f, be, nu))),
                  pl.BlockSpec((1, tf, d), lambda i, f, be, nu: (be[i], fe(i, f, be, nu), 0))],
        out_specs=pl.BlockSpec((tb, d), lambda i, f, be, nu: (i, 0)),
        scratch_shapes=[pltpu.VMEM((tb, d), F32), pltpu.VMEM((2, 1, tb, d), F32), pltpu.VMEM((tb, d), BF16),
                        pltpu.SemaphoreType.DMA((2,))])
    idx = buf_tok.reshape(nblk, 1, tb)
    return pl.pallas_call(
        functools.partial(_moe_ffn_kernel, tb=tb),
        grid_spec=grid_spec,
        out_shape=jax.ShapeDtypeStruct((nblk * tb, d), F32),
        compiler_params=_cparams(("arbitrary", "arbitrary"), 48),
        name="moe_ffn",
    )(blk_e, n_used, idx, idx, x, wg, wu, wd)


def _combine_kernel(idx_ref, idxn_ref, yb_hbm, x_ref, gate_ref, g_ref, b_ref, o_ref, ybuf, sem, *, alpha, tm):
    i = pl.program_id(0)
    slot = i % 2

    @pl.when(i == 0)
    def _():
        _row_gather(yb_hbm, ybuf.at[0], idx_ref, tm, TOP_K, sem.at[0], True)

    @pl.when(i + 1 < pl.num_programs(0))
    def _():
        _row_gather(yb_hbm, ybuf.at[1 - slot], idxn_ref, tm, TOP_K, sem.at[1 - slot], True)

    _row_gather(yb_hbm, ybuf.at[slot], idx_ref, tm, TOP_K, sem.at[slot], False)
    gate = gate_ref[...]
    y = gate[:, 0:1] * ybuf[slot, 0] + gate[:, 1:2] * ybuf[slot, 1]
    o_ref[...] = _ln(alpha * x_ref[...] + y, g_ref[...], b_ref[...])


def _moe_combine(yb, dest, x, gate, g, b, alpha):
    m, d = x.shape
    tm = _row_tile(m, (256, 128))
    nblk = m // tm
    idx = dest.reshape(nblk, 1, TOP_K * tm)
    return pl.pallas_call(
        functools.partial(_combine_kernel, alpha=alpha, tm=tm),
        grid=(nblk,),
        in_specs=[pl.BlockSpec((1, 1, TOP_K * tm), lambda i: (i, 0, 0), memory_space=pltpu.SMEM),
                  pl.BlockSpec((1, 1, TOP_K * tm), lambda i: (jnp.minimum(i + 1, nblk - 1), 0, 0),
                               memory_space=pltpu.SMEM),
                  pl.BlockSpec(memory_space=pl.ANY),
                  pl.BlockSpec((tm, d), lambda i: (i, 0)),
                  pl.BlockSpec((tm, LANES), lambda i: (i, 0)),
                  pl.BlockSpec((1, d), lambda i: (0, 0)),
                  pl.BlockSpec((1, d), lambda i: (0, 0))],
        out_specs=pl.BlockSpec((tm, d), lambda i: (i, 0)),
        out_shape=jax.ShapeDtypeStruct((m, d), F32),
        scratch_shapes=[pltpu.VMEM((2, TOP_K, tm, d), F32), pltpu.SemaphoreType.DMA((2,))],
        compiler_params=_cparams(("arbitrary",), 32),
        name="moe_combine",
    )(idx, idx, yb, x, gate, g, b)


def _moe_res_ln(x, w_router, b_router, wg, wu, wd, g, b, alpha):
    m, d = x.shape
    ne = w_router.shape[1]
    eid, gate = _router(x, w_router, b_router)
    tk = m * TOP_K
    tb = 1024 if tk >= 16384 else 256
    ef = eid[:, :TOP_K].reshape(-1)
    onehot = (ef[:, None] == jnp.arange(ne, dtype=I32)[None, :]).astype(I32)
    csum = jnp.cumsum(onehot, axis=0)
    rank = jnp.sum(csum * onehot, axis=1) - 1
    counts = csum[-1]
    padded = (counts + tb - 1) // tb * tb
    pend = jnp.cumsum(padded)
    dest = (pend - padded)[ef] + rank
    nblk = -(-tk // tb) + ne
    buf_tok = jnp.zeros((nblk * tb,), I32).at[dest].set(jnp.arange(tk, dtype=I32) // TOP_K)
    blk_start = jnp.arange(nblk, dtype=I32) * tb
    blk_e = jnp.minimum(jnp.sum((pend[None, :] <= blk_start[:, None]).astype(I32), axis=1), ne - 1).astype(I32)
    n_used = (pend[-1:] // tb).astype(I32)
    yb = _moe_ffn(x, buf_tok, blk_e, n_used, wg, wu, wd, tb)
    return _moe_combine(yb, dest.astype(I32), x, gate, g, b, alpha)


PAGE = 128


def _compress_kernel(pt_ref, pages_hbm, w1_ref, pos_ref, b1_ref, w2_ref, b2_ref, o_ref, scr, acc_scr, cv_scr, sem,
                     *stage, npg, col0):
    b = pl.program_id(0)
    pw = NSA_KV * NSA_DH
    nsub = npg * PAGE // CMP_STRIDE
    hid = b1_ref.shape[2]

    nslab = 2 * pw // LANES

    def position_term():
        for c in range(2):
            cv = jnp.zeros((SUBLANES, hid), F32) + b1_ref[c]
            for o in range(CMP_STRIDE):
                cv = cv + _dot(pos_ref[c, o].astype(BF16), w1_ref[c, o])[:, :hid]
                cv = cv + _dot(pos_ref[c, CMP_STRIDE + o].astype(BF16), w1_ref[c, o])[:, hid:]
            cv_scr[c] = cv

    if not stage:
        def start(p, c_):
            row = pl.multiple_of(p * PAGE, PAGE)
            for sl in range(nslab):
                pltpu.make_async_copy(pages_hbm.at[pt_ref[b, p], :, pl.ds(col0 + sl * LANES, LANES)],
                                      scr.at[sl, pl.ds(row, PAGE), :], sem.at[0]).start()
            return c_

        lax.fori_loop(0, npg, start, 0)
        pl.when(b == 0)(position_term)
        for sl in range(nslab):
            pltpu.make_async_copy(scr.at[sl], scr.at[sl], sem.at[0]).wait()
    else:
        stg = stage[0]
        cpg = stg.shape[2]
        nch = npg // cpg

        def start_chunk(bb, k, slot):
            def body(p, c_):
                for c in range(2):
                    pltpu.make_async_copy(pages_hbm.at[pt_ref[bb, k * cpg + p], c], stg.at[slot, c, p],
                                          sem.at[slot]).start()
                return c_

            lax.fori_loop(0, cpg, body, 0)

        @pl.when(b == 0)
        def _():
            start_chunk(b, 0, 0)
            position_term()

        for k in range(nch):
            slot = k % 2
            if k + 1 < nch:
                start_chunk(b, k + 1, 1 - slot)
            else:
                pl.when(b + 1 < pl.num_programs(0))(functools.partial(start_chunk, b + 1, 0, (k + 1) % 2))
            for c in range(2):
                pltpu.make_async_copy(stg.at[slot, c], stg.at[slot, c], sem.at[slot]).wait()

            def to_token_major(p, c_):
                row = pl.multiple_of((k * cpg + p) * PAGE, PAGE)
                for sl in range(nslab):
                    c, half = divmod(sl, nslab // 2)
                    scr[sl, pl.ds(row, PAGE), :] = stg[slot, c, p, half * LANES:(half + 1) * LANES, :].T
                return c_

            lax.fori_loop(0, cpg, to_token_major, 0)

    gps = LANES // NSA_DH
    for c in range(2):
        acc_scr[...] = jnp.zeros_like(acc_scr)
        for o in range(CMP_STRIDE):
            for sl in range(nslab // 2):
                xo = scr[c * (nslab // 2) + sl, pl.ds(o, nsub, stride=CMP_STRIDE), :].astype(BF16)
                for gg in range(gps):
                    acc_scr[sl * gps + gg] += _dot(xo[:, gg * NSA_DH:(gg + 1) * NSA_DH], w1_ref[c, o])
        for g in range(NSA_KV):
            acc = acc_scr[g]
            pre = acc[:, :hid] + pltpu.roll(acc[:, hid:], nsub - 1, 0) + cv_scr[c, 0:1, :]
            o_ref[0, c, g] = _dot(_gelu_tanh(pre).astype(BF16), w2_ref[c]) + b2_ref[c]


def _compress(pages, page_table, pos, w1, b1, w2, b2, col0=0, token_minor=False):
    nb, npg = page_table.shape
    nsub = npg * PAGE // CMP_STRIDE
    hid = w1.shape[-1]
    pw = NSA_KV * NSA_DH
    w1cat = jnp.concatenate([w1[:, :CMP_STRIDE], w1[:, CMP_STRIDE:]], axis=-1).astype(BF16)
    posb = jnp.broadcast_to(pos[:, :, None, :], (2, CMP_BLK, SUBLANES, NSA_DH))
    chunk = 16
    assert not token_minor or (npg % chunk == 0 and (npg // chunk) % 2 == 0)
    grid_spec = pltpu.PrefetchScalarGridSpec(
        num_scalar_prefetch=1,
        grid=(nb,),
        in_specs=[pl.BlockSpec(memory_space=pl.ANY),
                  pl.BlockSpec((2, CMP_STRIDE, NSA_DH, 2 * hid), lambda b, pt: (0, 0, 0, 0)),
                  pl.BlockSpec((2, CMP_BLK, SUBLANES, NSA_DH), lambda b, pt: (0, 0, 0, 0)),
                  pl.BlockSpec((2, 1, hid), lambda b, pt: (0, 0, 0)),
                  pl.BlockSpec((2, hid, NSA_DH), lambda b, pt: (0, 0, 0)),
                  pl.BlockSpec((2, 1, NSA_DH), lambda b, pt: (0, 0, 0))],
        out_specs=pl.BlockSpec((1, 2, NSA_KV, nsub, NSA_DH), lambda b, pt: (b, 0, 0, 0, 0)),
        scratch_shapes=[pltpu.VMEM((2 * pw // LANES, npg * PAGE, LANES), F32),
                        pltpu.VMEM((NSA_KV, nsub, 2 * hid), F32),
                        pltpu.VMEM((2, SUBLANES, hid), F32),
                        pltpu.SemaphoreType.DMA((2,))]
                       + ([pltpu.VMEM((2, 2, chunk, pw, PAGE), F32)] if token_minor else []))
    return pl.pallas_call(
        functools.partial(_compress_kernel, npg=npg, col0=col0),
        grid_spec=grid_spec,
        out_shape=jax.ShapeDtypeStruct((nb, 2, NSA_KV, nsub, NSA_DH), F32),
        compiler_params=_cparams(("arbitrary",), 56),
        name="nsa_compress",
    )(page_table, pages, w1cat, posb, b1[:, None, :], w2.astype(BF16), b2[:, None, :])


def _masked_softmax(s, valid):
    s = jnp.where(valid, s, NEG)
    p = jnp.exp(s - jnp.max(s, -1, keepdims=True))
    return p / jnp.sum(p, -1, keepdims=True)


def _select_blocks(imp, ovl, tq, nbs):
    score = sum(_dot(t, ovl) for t in _split(imp, 3))
    jb = lax.broadcasted_iota(I32, score.shape, 1)
    ok = jb * SEL_BLK <= tq
    cur = tq // SEL_BLK
    forced = (jb == 0) | (jb == cur) | (jb == cur - 1)
    score = jnp.where(ok, score + jnp.where(forced, FORCE, 0.0), NEG)
    rank = jnp.zeros(score.shape, F32)
    for j2 in range(nbs):
        cj = score[:, j2:j2 + 1]
        beats = (cj > score) | ((cj == score) & (j2 < jb))
        rank = rank + jnp.where(beats, 1.0, 0.0)
    return jnp.where(ok & (rank < N_SEL), 1.0, 0.0)


def _expand_sel(sel, first_blk, nkeys):
    jj = lax.broadcasted_iota(I32, (sel.shape[1], nkeys), 0)
    kk = lax.broadcasted_iota(I32, (sel.shape[1], nkeys), 1)
    expand = jnp.where(kk // SEL_BLK + first_blk == jj, 1.0, 0.0).astype(BF16)
    return _dot(sel.astype(BF16), expand)


POS_SPLIT = 64
Q_EXTRA = 16


PROMPT_KT = 512


def _nsa_prompt_kernel(q_ref, gate_ref, slope_ref, kc_ref, vc_ref, ks_ref, vs_ref, kw_ref, vw_ref, ovl_ref, exp_ref,
                       o_ref, *, nbs, qblk, nrep):
    qb = pl.program_id(2)
    cols = qblk * nrep
    q = q_ref[0, 0, 0]
    q16 = q.astype(BF16)
    slope = slope_ref[0][0:1, :]
    tq = qb * qblk + lax.broadcasted_iota(I32, (1, cols), 1) % qblk
    tq_row = tq[:, 0:qblk]

    kc = kc_ref[0, 0]
    qh, ql = _split(q[:NSA_DH, :], 2)
    kh, kl = _split(kc, 2)
    s = _dot(kh, qh) + _dot(kh, ql) + _dot(kl, qh)
    e = lax.broadcasted_iota(I32, (kc.shape[0], 1), 0) * CMP_STRIDE + (CMP_BLK - 1)
    valid = e <= tq
    s = jnp.where(valid, s + slope * e.astype(F32), NEG)
    p = jnp.exp(s - jnp.max(s, 0, keepdims=True))
    p = jnp.where(valid, p / jnp.sum(p, 0, keepdims=True), 0.0)
    o_cmp = _dot(vc_ref[0, 0].astype(BF16), p.astype(BF16))
    imp = p[:, 0:qblk]
    for r in range(1, nrep):
        imp = imp + p[:, r * qblk:(r + 1) * qblk]

    score = sum(_dot(ovl_ref[...], t) for t in _split(imp, 3))
    jb = lax.broadcasted_iota(I32, score.shape, 0)
    ok = jb * SEL_BLK <= tq_row
    cur = tq_row // SEL_BLK
    forced = (jb == 0) | (jb == cur) | (jb == cur - 1)
    score = jnp.where(ok, score + jnp.where(forced, FORCE, 0.0), NEG)
    rank = jnp.zeros(score.shape, F32)
    for j2 in range(nbs):
        cj = score[j2:j2 + 1, :]
        beats = (cj > score) | ((cj == score) & (j2 < jb))
        rank = rank + jnp.where(beats, 1.0, 0.0)
    selneg = jnp.where(ok & (rank < N_SEL), 0.0, NEG).astype(BF16)

    def tile(k_ref, v_ref, bias_fn, kt, carry):
        m, l, acc = carry
        off = pl.multiple_of(kt * PROMPT_KT, PROMPT_KT)
        kpos = off + lax.broadcasted_iota(I32, (PROMPT_KT, 1), 0)
        s_ = _dot(k_ref[0, 0, pl.ds(off, PROMPT_KT), :], q16)
        s_ = s_ + jnp.concatenate([bias_fn(off, kpos)] * nrep, axis=1)
        m_new = jnp.maximum(m, jnp.max(s_, 0, keepdims=True))
        a = jnp.exp(m - m_new)
        p_ = jnp.exp(s_ - m_new)
        v = v_ref[0, 0, :, pl.ds(off, PROMPT_KT)].astype(BF16)
        return m_new, l * a + jnp.sum(p_, 0, keepdims=True), acc * a + _dot(v, p_.astype(BF16))

    def sel_bias(off, kpos):
        return jnp.where(kpos <= tq_row, _dot(exp_ref[pl.ds(off, PROMPT_KT), :], selneg), NEG)

    def win_bias(off, kpos):
        return jnp.where((kpos <= tq_row) & (kpos > tq_row - WINDOW), 0.0, NEG)

    kt_hi = (qb * qblk + qblk - 1) // PROMPT_KT + 1
    kt_win = jnp.maximum(qb * qblk - (WINDOW - 1), 0) // PROMPT_KT
    init = (jnp.full((1, cols), -jnp.inf, F32), jnp.zeros((1, cols), F32), jnp.zeros((NSA_DH, cols), F32))
    c_sel = lax.fori_loop(0, kt_win, lambda kt, c: tile(ks_ref, vs_ref, sel_bias, kt, c), init)
    c_sel, c_win = lax.fori_loop(
        kt_win, kt_hi,
        lambda kt, c: (tile(ks_ref, vs_ref, sel_bias, kt, c[0]), tile(kw_ref, vw_ref, win_bias, kt, c[1])),
        (c_sel, init))
    gate = gate_ref[0, 0, 0]
    o_ref[0, 0, 0] = (gate[0:1, :] * o_cmp + gate[1:2, :] * (c_sel[2] / c_sel[1])
                      + gate[2:3, :] * (c_win[2] / c_win[1]))


def _nsa_prompt(qt, gates, slopes, kc, vct, ksa, kwa, kvt, ovl_t, nbs):
    nb, ng, nqb, qw, cols = qt.shape
    dh = qw - Q_EXTRA
    seq = kvt.shape[3]
    nbc = kc.shape[2]
    qblk = seq // nqb
    nrep = cols // qblk
    qspec = lambda r: pl.BlockSpec((1, 1, 1, r, cols), lambda b, g, i: (b, g, i, 0, 0))
    per_g = lambda r, c: pl.BlockSpec((1, 1, r, c), lambda b, g, i: (b, g, 0, 0))
    tspec = lambda c: pl.BlockSpec((1, 1, dh, seq), lambda b, g, i: (b, c * ng + g, 0, 0))
    expand = (jnp.arange(seq)[:, None] // SEL_BLK == jnp.arange(ovl_t.shape[0])[None, :]).astype(BF16)
    return pl.pallas_call(
        functools.partial(_nsa_prompt_kernel, nbs=nbs, qblk=qblk, nrep=nrep),
        grid=(nb, ng, nqb),
        in_specs=[qspec(qw), qspec(SUBLANES),
                  pl.BlockSpec((1, SUBLANES, cols), lambda b, g, i: (g, 0, 0)),
                  per_g(nbc, dh), per_g(dh, nbc), per_g(seq, qw), tspec(3), per_g(seq, qw), tspec(5),
                  pl.BlockSpec(ovl_t.shape, lambda b, g, i: (0, 0)),
                  pl.BlockSpec(expand.shape, lambda b, g, i: (0, 0))],
        out_specs=qspec(dh),
        out_shape=jax.ShapeDtypeStruct((nb, ng, nqb, dh, cols), F32),
        compiler_params=_cparams(("parallel", "parallel", "arbitrary"), 48),
        name="nsa_prompt",
    )(qt, gates, slopes, kc, vct, ksa, kvt, kwa, kvt, ovl_t, expand)


def _nsa_sample_kernel(pt_ref, q_ref, gate_ref, slope_ref, kc_ref, vc_ref, *refs, nbs, past, nq, nrep, ppt):
    ks_pages, vs_pages = refs[:ppt], refs[ppt:2 * ppt]
    (ksn_ref, vsn_ref, kwp_ref, vwp_ref, kwn_ref, vwn_ref, ovl_ref, o_ref,
     sel_scr, m_scr, l_scr, acc_scr, part_scr) = refs[2 * ppt:]
    keys = ppt * PAGE
    t = pl.program_id(1)
    rows = q_ref.shape[1]
    pw = q_ref.shape[2]
    grows = rows // NSA_KV
    scale = NSA_DH ** -0.5
    q = q_ref[0]
    q16 = q.astype(BF16)
    slope = slope_ref[:, 0:1]
    tq = past + lax.broadcasted_iota(I32, (rows, 1), 0) % nq
    tqf = tq.astype(F32)
    gate = gate_ref[0]

    def logits(qk, kpos):
        return qk * scale - slope * (tqf - kpos.astype(F32))

    @pl.when(t == 0)
    def _():
        kc = kc_ref[0]
        qh, ql = _split(q, 2)
        kh, kl = _split(kc, 2)
        s = (_dot_nt(qh, kh) + _dot_nt(ql, kh) + _dot_nt(qh, kl)) * scale
        e = lax.broadcasted_iota(I32, (1, kc.shape[0]), 1) * CMP_STRIDE + (CMP_BLK - 1)
        valid = e <= tq
        p = jnp.where(valid, _masked_softmax(s - slope * (tqf - e.astype(F32)), valid), 0.0)
        o_cmp = _dot(p.astype(BF16), vc_ref[0].astype(BF16))
        imps = []
        for g in range(NSA_KV):
            a = p[g * grows:g * grows + nq]
            for r in range(1, nrep):
                a = a + p[g * grows + r * nq:g * grows + (r + 1) * nq]
            imps.append(a)
        imp = jnp.concatenate(imps, axis=0)
        tq_s = past + lax.broadcasted_iota(I32, (NSA_KV * nq, 1), 0) % nq
        sel_s = _select_blocks(imp, ovl_ref[...], tq_s, nbs)
        sel = jnp.concatenate([sel_s[g * nq:(g + 1) * nq] for g in range(NSA_KV) for _ in range(nrep)], axis=0)
        sel_scr[...] = sel

        wb = kwp_ref.shape[3]
        npad = LANES - nq
        zpad = jnp.zeros((npad, pw), F32)
        kwn = jnp.concatenate([kwn_ref[0], zpad], axis=0).astype(BF16)
        vwn = jnp.concatenate([vwn_ref[0], zpad], axis=0).astype(BF16)
        qk = jnp.concatenate([_dot(q16, kwp_ref[0, 0].astype(BF16)), _dot_nt(q16, kwn)], axis=1)
        idx = lax.broadcasted_iota(I32, (1, wb + LANES), 1)
        kpos = past - wb + idx
        valid = (idx < wb + nq) & (kpos <= tq) & (kpos > tq - WINDOW)
        pw_ = _masked_softmax(logits(qk, kpos), valid).astype(BF16)
        o_win = _dot_nt(pw_[:, :wb], vwp_ref[0, 0].astype(BF16)) + _dot(pw_[:, wb:], vwn)
        part_scr[...] = gate[:, 0:1] * o_cmp + gate[:, 2:3] * o_win

        kn = jnp.concatenate([ksn_ref[0], zpad], axis=0).astype(BF16)
        vn = jnp.concatenate([vsn_ref[0], zpad], axis=0).astype(BF16)
        idx = lax.broadcasted_iota(I32, (1, LANES), 1)
        kpos = past + idx
        blk = past // SEL_BLK
        valid = (idx < nq) & (kpos <= tq) & (sel[:, blk:blk + 1] > 0.5)
        s = jnp.where(valid, logits(_dot_nt(q16, kn), kpos), NEG)
        m = jnp.max(s, -1, keepdims=True)
        p = jnp.exp(s - m)
        m_scr[...] = m
        l_scr[...] = jnp.sum(p, -1, keepdims=True)
        acc_scr[...] = _dot(p.astype(BF16), vn)

    k = jnp.concatenate([r[0, 0] for r in ks_pages], axis=1).astype(BF16)
    v = jnp.concatenate([r[0, 0] for r in vs_pages], axis=1).astype(BF16)
    kpos = t * keys + lax.broadcasted_iota(I32, (1, keys), 1)
    selk = _expand_sel(sel_scr[...], t * (keys // SEL_BLK), keys)
    s = jnp.where(selk > 0.5, logits(_dot(q16, k), kpos), NEG)
    m = m_scr[...]
    m_new = jnp.maximum(m, jnp.max(s, -1, keepdims=True))
    a = jnp.exp(m - m_new)
    p = jnp.exp(s - m_new)
    m_scr[...] = m_new
    l_scr[...] = l_scr[...] * a + jnp.sum(p, -1, keepdims=True)
    acc_scr[...] = acc_scr[...] * a + _dot_nt(p.astype(BF16), v)

    @pl.when(t == pl.num_programs(1) - 1)
    def _():
        tot = part_scr[...] + gate[:, 1:2] * (acc_scr[...] / l_scr[...])
        o_ref[0] = jnp.concatenate([tot[g * grows:(g + 1) * grows, g * NSA_DH:(g + 1) * NSA_DH]
                                    for g in range(NSA_KV)], axis=0)


def _nsa_sample(qbd, gates, slopes, kc, vc, cache_t, page_table, kv_new, win_t, win_new, ovl, nbs, nq, nrep):
    nb, rows, pw = qbd.shape
    npg = page_table.shape[1]
    past = npg * PAGE
    ppt = max(p for p in (8, 4, 2, 1) if npg % p == 0)
    nt = npg // ppt
    nbc = kc.shape[1]
    wb = win_t.shape[3]
    const = lambda shape: pl.BlockSpec(shape, lambda b, t, pt: (0,) * len(shape))
    per_b = lambda n, w, col=0: pl.BlockSpec((1, n, w), lambda b, t, pt: (b, 0, col))
    page = lambda c, i: pl.BlockSpec((1, 1, pw, PAGE), lambda b, t, pt: (pt[b, ppt * t + i], c, 0, 0))
    wspec = lambda c: pl.BlockSpec((1, 1, pw, wb), lambda b, t, pt: (b, c, 0, 0))
    grid_spec = pltpu.PrefetchScalarGridSpec(
        num_scalar_prefetch=1,
        grid=(nb, nt),
        in_specs=[per_b(rows, pw), per_b(rows, SUBLANES), const((rows, SUBLANES)), per_b(nbc, pw), per_b(nbc, pw)]
                 + [page(2, i) for i in range(ppt)] + [page(3, i) for i in range(ppt)]
                 + [per_b(nq, pw, 2), per_b(nq, pw, 3), wspec(0), wspec(1),
                    per_b(nq, pw, 0), per_b(nq, pw, 1), const(ovl.shape)],
        out_specs=pl.BlockSpec((1, rows, NSA_DH), lambda b, t, pt: (b, 0, 0)),
        scratch_shapes=[pltpu.VMEM((rows, ovl.shape[1]), F32), pltpu.VMEM((rows, 1), F32), pltpu.VMEM((rows, 1), F32),
                        pltpu.VMEM((rows, pw), F32), pltpu.VMEM((rows, pw), F32)])
    return pl.pallas_call(
        functools.partial(_nsa_sample_kernel, nbs=nbs, past=past, nq=nq, nrep=nrep, ppt=ppt),
        grid_spec=grid_spec,
        out_shape=jax.ShapeDtypeStruct((nb, rows, NSA_DH), F32),
        compiler_params=_cparams(("parallel", "arbitrary"), 48),
        name="nsa_sample",
    )(page_table, qbd, gates, slopes, kc, vc, *([cache_t] * (2 * ppt)), kv_new, kv_new, win_t, win_t,
      win_new, win_new, ovl)


def _overlap(nbc, nbs_pad):
    ci = jnp.arange(nbc)[:, None] * CMP_STRIDE
    sj = jnp.arange(nbs_pad)[None, :] * SEL_BLK
    return ((ci < sj + SEL_BLK) & (ci + CMP_BLK > sj)).astype(BF16)


def _alibi_slopes(nh):
    return jnp.exp2(-8.0 * (jnp.arange(nh, dtype=F32) + 1.0) / nh)


def _nsa_in(x, w_in, nh):
    qw = nh * NSA_DH
    kvw = 6 * NSA_KV * NSA_DH
    proj = _matmul(x, w_in[:, :qw + kvw].astype(BF16), tn=512)
    ng = 3 * nh
    gates = _matmul(x, jnp.pad(w_in[:, qw + kvw:], ((0, 0), (0, LANES - ng))).astype(BF16), act="sigmoid")
    kv4w = 4 * NSA_KV * NSA_DH
    return proj[:, :qw], proj[:, qw:qw + kv4w], proj[:, qw + kv4w:], gates[:, :ng]


def _mm_t_kernel(w_ref, x_ref, o_ref):
    o_ref[0] = _dot_nt(w_ref[...], x_ref[...].astype(BF16))


def _matmul_t(x, wt, nb):
    m, k = x.shape
    n = wt.shape[0]
    seq = m // nb
    tl = _row_tile(seq, (1024, 512, 256, 128))
    tn = _row_tile(n, (512, 256, 128))
    return pl.pallas_call(
        _mm_t_kernel,
        grid=(nb, seq // tl, n // tn),
        in_specs=[pl.BlockSpec((tn, k), lambda b, l, j: (j, 0)),
                  pl.BlockSpec((tl, k), lambda b, l, j: (b * (seq // tl) + l, 0))],
        out_specs=pl.BlockSpec((1, tn, tl), lambda b, l, j: (b, j, l)),
        out_shape=jax.ShapeDtypeStruct((nb, n, seq), F32),
        compiler_params=_cparams(("parallel", "parallel", "arbitrary"), 48),
        name="matmul_t",
    )(wt, x)


def _nsa_prompt_mixer(x, nb, seq, prm):
    w_in, pos, w1, b1, w2, b2 = prm
    nh = (w_in.shape[1] - 6 * NSA_KV * NSA_DH) // (NSA_DH + 3)
    nrep = nh // NSA_KV
    pw = NSA_KV * NSA_DH
    qw = nh * NSA_DH
    qblk = 128
    nqb = seq // qblk
    rows = qblk * nrep
    w_tok = jnp.concatenate([w_in[:, :qw + 3 * pw], w_in[:, qw + 4 * pw:qw + 5 * pw]], axis=1)
    proj = _matmul(x, w_tok.astype(BF16), tn=512)
    kvt = _matmul_t(x, w_in[:, qw:qw + 6 * pw].T.astype(BF16), nb)
    ng = 3 * nh
    gates = _matmul(x, jnp.pad(w_in[:, qw + 6 * pw:], ((0, 0), (0, LANES - ng))).astype(BF16), act="sigmoid")[:, :ng]
    npg = seq // PAGE
    cmp_out = _compress(proj.reshape(nb * npg, PAGE, proj.shape[1]), jnp.arange(nb * npg, dtype=I32).reshape(nb, npg),
                        pos, w1, b1, w2, b2, col0=qw)
    q = proj[:, :qw] * (NSA_DH ** -0.5)
    qt = q.reshape(nb, nqb, qblk, NSA_KV, nrep, NSA_DH).transpose(0, 3, 1, 5, 4, 2).reshape(nb, NSA_KV, nqb, NSA_DH, rows)
    slf = jnp.repeat(_alibi_slopes(nh).reshape(NSA_KV, nrep), qblk, axis=1)
    s_hi = slf.astype(BF16).astype(F32)
    s_lo = (slf - s_hi).astype(BF16).astype(F32)
    extra = jnp.stack([POS_SPLIT * s_hi, s_hi, POS_SPLIT * s_lo, s_lo] + [jnp.zeros_like(slf)] * (Q_EXTRA - 4), axis=1)
    qt = jnp.concatenate([qt, jnp.broadcast_to(extra[None, :, None], (nb, NSA_KV, nqb, Q_EXTRA, rows))], axis=3)
    gs = gates.reshape(nb, nqb, qblk, 3, NSA_KV, nrep).transpose(0, 4, 1, 3, 5, 2).reshape(nb, NSA_KV, nqb, 3, rows)
    gs = jnp.pad(gs, ((0, 0),) * 3 + ((0, SUBLANES - 3), (0, 0)))
    sl = jnp.pad(slf[:, None, :], ((0, 0), (0, SUBLANES - 1), (0, 0)))
    t = jnp.arange(seq, dtype=I32)
    pos_cols = jnp.stack([t // POS_SPLIT, t % POS_SPLIT] * 2 + [jnp.zeros_like(t)] * (Q_EXTRA - 4), axis=1).astype(F32)
    pos_cols = jnp.broadcast_to(pos_cols[None, None], (nb, NSA_KV, seq, Q_EXTRA))
    with_pos = lambda cols: jnp.concatenate(
        [cols.reshape(nb, seq, NSA_KV, NSA_DH).transpose(0, 2, 1, 3), pos_cols], axis=-1).astype(BF16)
    ksa = with_pos(proj[:, qw + 2 * pw:qw + 3 * pw])
    kwa = with_pos(proj[:, qw + 3 * pw:qw + 4 * pw])
    nbs = -(-seq // SEL_BLK)
    nbs_r = -(-nbs // SUBLANES) * SUBLANES
    o = _nsa_prompt(qt, gs, sl, cmp_out[:, 0], cmp_out[:, 1].transpose(0, 1, 3, 2), ksa, kwa,
                    kvt.reshape(nb, 6 * NSA_KV, NSA_DH, seq), _overlap(seq // CMP_STRIDE, nbs_r).T, nbs)
    o = o.reshape(nb, NSA_KV, nqb, NSA_DH, nrep, qblk).transpose(0, 2, 5, 1, 4, 3).reshape(nb * seq, nh * NSA_DH)
    kv6 = kvt.reshape(nb, 6, NSA_KV, NSA_DH, seq).transpose(0, 4, 1, 2, 3)
    return o, kv6[:, :, :4], kv6[:, :, 4:]


def _nsa_sample_mixer(x, nb, nq, cache, page_table, win_buf, prm):
    w_in, pos, w1, b1, w2, b2 = prm
    nh = (w_in.shape[1] - 6 * NSA_KV * NSA_DH) // (NSA_DH + 3)
    nrep = nh // NSA_KV
    pw = NSA_KV * NSA_DH
    npg = page_table.shape[1]
    past = npg * PAGE
    q, kv4, win, gates = _nsa_in(x, w_in, nh)
    n_pool = cache.shape[0]
    cache_t = cache.transpose(0, 2, 3, 4, 1).reshape(n_pool, 4, pw, PAGE)
    win_t = win_buf.transpose(0, 2, 3, 4, 1).reshape(nb, 2, pw, win_buf.shape[1])
    cmp_out = _compress(cache_t, page_table, pos, w1, b1, w2, b2, token_minor=True)
    nsub = cmp_out.shape[3]
    kc = cmp_out[:, 0].transpose(0, 2, 1, 3).reshape(nb, nsub, pw)
    vc = cmp_out[:, 1].transpose(0, 2, 1, 3).reshape(nb, nsub, pw)
    rows = NSA_KV * nrep * nq
    q5 = q.reshape(nb, nq, NSA_KV, nrep, NSA_DH).transpose(0, 2, 3, 1, 4)
    qbd = jnp.einsum("bgrqd,gh->bgrqhd", q5, jnp.eye(NSA_KV, dtype=F32)).reshape(nb, rows, pw)
    gs = gates.reshape(nb, nq, 3, NSA_KV, nrep).transpose(0, 3, 4, 1, 2).reshape(nb, rows, 3)
    gs = jnp.pad(gs, ((0, 0), (0, 0), (0, SUBLANES - 3)))
    sl = jnp.repeat(_alibi_slopes(nh), nq)[:, None]
    sl = jnp.pad(sl, ((0, 0), (0, SUBLANES - 1)))
    nbs = -(-(past + nq) // SEL_BLK)
    nbs_pad = -(-nbs // LANES) * LANES
    o = _nsa_sample(qbd, gs, sl, kc, vc, cache_t, page_table, kv4.reshape(nb, nq, 4 * pw),
                    win_t, win.reshape(nb, nq, 2 * pw), _overlap(nsub, nbs_pad), nbs, nq, nrep)
    o = o.reshape(nb, NSA_KV, nrep, nq, NSA_DH).transpose(0, 3, 1, 2, 4).reshape(nb * nq, nh * NSA_DH)
    return o, kv4.reshape(nb, nq, 4, NSA_KV, NSA_DH), win.reshape(nb, nq, 2, NSA_KV, NSA_DH)


def kernel(x_prompt, x_sample, state_ssm, state_conv, cache_kv, state_win, page_table, ln_mix_g, ln_mix_b, ln_ffn_g, ln_ffn_b, ssd_w_in, ssd_conv_w, ssd_conv_b, ssd_dt_bias, ssd_a_log, ssd_d, ssd_norm_w, ssd_w_out, gm_w_in, gm_b_in, gm_ln_g, gm_ln_b, gm_w_s, gm_b_s, gm_w_out, nsa_w_in, nsa_cmp_pos, nsa_cmp_w1, nsa_cmp_b1, nsa_cmp_w2, nsa_cmp_b2, nsa_w_out, ffn_w_gate, ffn_w_up, ffn_w_down, moe_w_router, moe_b_router, moe_w_gate, moe_w_up, moe_w_down):
    depth = ln_mix_g.shape[0]
    alpha = (2 * depth) ** 0.25
    bp, seq, d = x_prompt.shape
    db, dseq, _ = x_sample.shape
    xp = x_prompt.reshape(bp * seq, d)
    xs = x_sample.reshape(db * dseq, d)
    outs = {k: [] for k in ("ssm_p", "conv_p", "ssm_s", "conv_s", "gmv_s", "kv_p", "win_p", "kv_s", "win_s")}
    row = lambda a, i: a[i][None, :]
    for l in range(depth):
        kind, j = l % N_MIXERS, l // N_MIXERS
        mg, mb = row(ln_mix_g, l), row(ln_mix_b, l)
        if kind == 0:
            prm = (ssd_w_in[j], ssd_conv_w[j], ssd_conv_b[j], ssd_dt_bias[j], ssd_a_log[j], ssd_d[j], ssd_norm_w[j])
            cdim = ssd_conv_w.shape[2]
            w_out = ssd_w_out[j].astype(BF16)
            yp, cp, hp = _ssd_mixer(xp, bp, SSD_CHUNK, jnp.zeros((bp, 8, cdim), F32),
                                    jnp.zeros((bp,) + state_ssm.shape[2:], F32), prm)
            ys, cs, hs = _ssd_mixer(xs, db, dseq, jnp.pad(state_conv[j], ((0, 0), (8 - (SSD_CONV - 1), 0), (0, 0))),
                                    state_ssm[j], prm)
            outs["ssm_p"].append(hp), outs["conv_p"].append(cp), outs["ssm_s"].append(hs), outs["conv_s"].append(cs)
            xp = _mm_res_ln(yp, xp, w_out, mg, mb, alpha)
            xs = _mm_res_ln(ys, xs, w_out, mg, mb, alpha)
        elif kind == 1:
            prm = (gm_w_in[j], gm_b_in[j], gm_ln_g[j], gm_ln_b[j], gm_w_s[j], gm_b_s[j], gm_w_out[j])
            xp, _ = _gmlp_mixer(xp, bp, seq, prm, mg, mb, alpha)
            xs, vs = _gmlp_mixer(xs, db, dseq, prm, mg, mb, alpha)
            outs["gmv_s"].append(vs.reshape(db, dseq, -1))
        else:
            prm = (nsa_w_in[j], nsa_cmp_pos[j], nsa_cmp_w1[j], nsa_cmp_b1[j], nsa_cmp_w2[j], nsa_cmp_b2[j])
            w_out = nsa_w_out[j].astype(BF16)
            op, kvp, wp = _nsa_prompt_mixer(xp, bp, seq, prm)
            os_, kvs, wsn = _nsa_sample_mixer(xs, db, dseq, cache_kv[j], page_table, state_win[j], prm)
            outs["kv_p"].append(kvp), outs["win_p"].append(wp[:, seq - min(WINDOW, seq):])
            outs["kv_s"].append(kvs), outs["win_s"].append(wsn)
            xp = _mm_res_ln(op, xp, w_out, mg, mb, alpha)
            xs = _mm_res_ln(os_, xs, w_out, mg, mb, alpha)
        f = l // 2
        fg, fb = row(ln_ffn_g, l), row(ln_ffn_b, l)
        if l % 2 == 0:
            wg, wu, wd = ffn_w_gate[f].astype(BF16), ffn_w_up[f].astype(BF16), ffn_w_down[f].astype(BF16)
            xp = _swiglu_res_ln(xp, wg, wu, wd, fg, fb, alpha)
            xs = _swiglu_res_ln(xs, wg, wu, wd, fg, fb, alpha)
        else:
            wg, wu, wd = moe_w_gate[f].astype(BF16), moe_w_up[f].astype(BF16), moe_w_down[f].astype(BF16)
            xp = _moe_res_ln(xp, moe_w_router[f], moe_b_router[f], wg, wu, wd, fg, fb, alpha)
            xs = _moe_res_ln(xs, moe_w_router[f], moe_b_router[f], wg, wu, wd, fg, fb, alpha)
    st = lambda k: jnp.stack(outs[k])
    return (xp.reshape(bp, seq, d), xs.reshape(db, dseq, d), st("ssm_p"), st("conv_p"), st("ssm_s"), st("conv_s"),
            st("gmv_s"), st("kv_p"), st("win_p"), st("kv_s"), st("win_s"))
```

```python
import functools

import jax
import jax.numpy as jnp
from jax import lax
from jax.experimental import pallas as pl
from jax.experimental.pallas import tpu as pltpu

F32 = jnp.float32
BF16 = jnp.bfloat16
I32 = jnp.int32

LN_EPS = 1e-5
NEG = -1e30
N_MIXERS = 3
SSD_P = 64
SSD_N = 128
SSD_G = 4
SSD_CONV = 4
SSD_CHUNK = 128
GM_CHUNK = 128
GM_G = 8
NSA_KV = 4
NSA_DH = 64
CMP_BLK = 32
CMP_STRIDE = 16
SEL_BLK = 64
N_SEL = 16
WINDOW = 512
FORCE = 1e4
TOP_K = 2

LANES = 128
SUBLANES = 8
MIB = 1 << 20


def _cparams(sem, vmem_mib):
    return pltpu.CompilerParams(dimension_semantics=sem, vmem_limit_bytes=vmem_mib * MIB)


def _row_tile(m, cands=(1024, 512, 256, 128, 64, 32, 16, 8)):
    for c in cands:
        if m % c == 0:
            return c
    raise ValueError(f"no row tile for {m}")


def _ln(v, g, b):
    mu = jnp.mean(v, -1, keepdims=True)
    d = v - mu
    var = jnp.mean(d * d, -1, keepdims=True)
    return d * lax.rsqrt(var + LN_EPS) * g + b


def _split(x, n):
    out = []
    r = x
    for k in range(n):
        h = r.astype(BF16)
        out.append(h)
        if k + 1 < n:
            r = r - h.astype(F32)
    return out


def _dot(a, b):
    return jnp.dot(a, b, preferred_element_type=F32)


def _dot_nt(a, b):
    return lax.dot_general(a, b, (((1,), (1,)), ((), ())), preferred_element_type=F32)


def _dot_tn(a, b):
    return lax.dot_general(a, b, (((0,), (0,)), ((), ())), preferred_element_type=F32)


def _silu(x):
    return x * jax.nn.sigmoid(x)


def _gelu_tanh(x):
    return 0.5 * x * (1.0 + jnp.tanh(0.7978845608028654 * (x + 0.044715 * (x * x * x))))


def _mm_kernel(x_ref, w_ref, b_ref, o_ref, *, act):
    acc = _dot(x_ref[...].astype(BF16), w_ref[...]) + b_ref[...]
    if act == "sigmoid":
        acc = jax.nn.sigmoid(acc)
    o_ref[...] = acc.astype(o_ref.dtype)


def _matmul(x, w, b=None, act=None, tn=None):
    m, k = x.shape
    n = w.shape[1]
    tm = _row_tile(m)
    tn = n if tn is None else tn
    if b is None:
        b = jnp.zeros((1, n), F32)
    return pl.pallas_call(
        functools.partial(_mm_kernel, act=act),
        grid=(m // tm, n // tn),
        in_specs=[pl.BlockSpec((tm, k), lambda i, j: (i, 0)),
                  pl.BlockSpec((k, tn), lambda i, j: (0, j)),
                  pl.BlockSpec((1, tn), lambda i, j: (0, j))],
        out_specs=pl.BlockSpec((tm, tn), lambda i, j: (i, j)),
        out_shape=jax.ShapeDtypeStruct((m, n), F32),
        compiler_params=_cparams(("parallel", "arbitrary"), 48),
        name="matmul",
    )(x, w, b)


def _mm_res_ln_kernel(y_ref, x_ref, w_ref, g_ref, b_ref, o_ref, *, alpha):
    f = _dot(y_ref[...].astype(BF16), w_ref[...])
    o_ref[...] = _ln(alpha * x_ref[...] + f, g_ref[...], b_ref[...])


def _mm_res_ln(y, x, w, g, b, alpha):
    m, k = y.shape
    d = x.shape[1]
    tm = _row_tile(m, (512, 256, 128))
    return pl.pallas_call(
        functools.partial(_mm_res_ln_kernel, alpha=alpha),
        grid=(m // tm,),
        in_specs=[pl.BlockSpec((tm, k), lambda i: (i, 0)),
                  pl.BlockSpec((tm, d), lambda i: (i, 0)),
                  pl.BlockSpec((k, d), lambda i: (0, 0)),
                  pl.BlockSpec((1, d), lambda i: (0, 0)),
                  pl.BlockSpec((1, d), lambda i: (0, 0))],
        out_specs=pl.BlockSpec((tm, d), lambda i: (i, 0)),
        out_shape=jax.ShapeDtypeStruct((m, d), F32),
        compiler_params=_cparams(("parallel",), 48),
        name="mm_res_ln",
    )(y, x, w, g, b)


def _swiglu_kernel(x_ref, wg_ref, wu_ref, wd_ref, g_ref, b_ref, o_ref, acc_ref, xb_ref, *, alpha):
    f = pl.program_id(1)

    @pl.when(f == 0)
    def _():
        acc_ref[...] = jnp.zeros_like(acc_ref)
        xb_ref[...] = x_ref[...].astype(BF16)

    xb = xb_ref[...]
    h = _silu(_dot(xb, wg_ref[...])) * _dot(xb, wu_ref[...])
    acc_ref[...] += _dot(h.astype(BF16), wd_ref[...])

    @pl.when(f == pl.num_programs(1) - 1)
    def _():
        o_ref[...] = _ln(alpha * x_ref[...] + acc_ref[...], g_ref[...], b_ref[...])


def _ff_tile(dff):
    for c in (512, 256, 128):
        if dff % c == 0:
            return c
    return dff


def _swiglu_res_ln(x, wg, wu, wd, g, b, alpha):
    m, d = x.shape
    dff = wg.shape[1]
    tm = _row_tile(m)
    tf = _ff_tile(dff)
    return pl.pallas_call(
        functools.partial(_swiglu_kernel, alpha=alpha),
        grid=(m // tm, dff // tf),
        in_specs=[pl.BlockSpec((tm, d), lambda i, f: (i, 0)),
                  pl.BlockSpec((d, tf), lambda i, f: (0, f)),
                  pl.BlockSpec((d, tf), lambda i, f: (0, f)),
                  pl.BlockSpec((tf, d), lambda i, f: (f, 0)),
                  pl.BlockSpec((1, d), lambda i, f: (0, 0)),
                  pl.BlockSpec((1, d), lambda i, f: (0, 0))],
        out_specs=pl.BlockSpec((tm, d), lambda i, f: (i, 0)),
        out_shape=jax.ShapeDtypeStruct((m, d), F32),
        scratch_shapes=[pltpu.VMEM((tm, d), F32), pltpu.VMEM((tm, d), BF16)],
        compiler_params=_cparams(("parallel", "arbitrary"), 48),
        name="swiglu_res_ln",
    )(x, wg, wu, wd, g, b)


def _ssd_kernel(xbc_ref, z_ref, dt_ref, conv0_ref, h0_ref, cw_ref, cb_ref, dtb_ref, alog_ref,
                dexp_ref, nw_ref, e_ref, ltri_ref, y_ref, h_ref, xp_scr, dtp_scr, yacc_scr,
                *, lv, n_heads):
    q = SSD_CHUNK
    di = n_heads * SSD_P
    gn = SSD_G * SSD_N
    hpg = n_heads // SSD_G
    gw = hpg * SSD_P
    ci = pl.program_id(1)

    @pl.when(ci == 0)
    def _():
        xp_scr[0:8, :] = conv0_ref[0]
        h_ref[0] = h0_ref[0]

    @pl.when(ci > 0)
    def _():
        xp_scr[0:8, :] = xp_scr[q:q + 8, :]

    if lv < q:
        xp_scr[8 + lv:, :] = jnp.zeros((q - lv, xp_scr.shape[1]), F32)
        dtp_scr[...] = jnp.zeros_like(dtp_scr)
        dtp_scr[0:lv, :] = dt_ref[...]
        dt_raw = dtp_scr[...]
    else:
        dt_raw = dt_ref[...]
    xp_scr[8:8 + lv, :] = xbc_ref[...]

    conv = cb_ref[...]
    for k in range(SSD_CONV):
        conv = conv + cw_ref[k:k + 1, :] * xp_scr[5 + k:5 + k + q, :]
    xc = _silu(conv)
    row = lax.broadcasted_iota(I32, (q, LANES), 0)
    lane = lax.broadcasted_iota(I32, (q, LANES), 1)
    if lv < q:
        xc = jnp.where(lax.broadcasted_iota(I32, xc.shape, 0) < lv, xc, 0.0)

    v = dt_raw + dtb_ref[...]
    dt = jnp.maximum(v, 0.0) + jnp.log1p(jnp.exp(-jnp.abs(v)))
    dt = jnp.where((row < lv) & (lane < n_heads), dt, 0.0)
    adt = dt * (-jnp.exp(alog_ref[...]))
    acum = sum(_dot(ltri_ref[...], t) for t in _split(adt, 3))
    acum_t = acum.T
    dt_t = dt.T
    ea = jnp.exp(acum)
    w = dt * jnp.exp(acum[q - 1:q, :] - acum)
    w_exp = sum(_dot(t, e_ref[...]) for t in _split(w, 2))
    ea_exp = sum(_dot(t, e_ref[...]) for t in _split(ea[:lv], 2))
    cdm = jnp.exp(jnp.broadcast_to(acum_t[:, q - 1:q], (LANES, LANES)))

    xs = xc[:, :di]
    xd = (xs * w_exp).astype(BF16)
    causal = (lax.broadcasted_iota(I32, (lv, q), 0) >= lax.broadcasted_iota(I32, (lv, q), 1))
    lo = lane < SSD_P

    for g in range(SSD_G):
        bg = xc[:, di + g * SSD_N:di + (g + 1) * SSD_N].astype(BF16)
        cg = xc[:lv, di + gn + g * SSD_N:di + gn + (g + 1) * SSD_N].astype(BF16)
        cbm = _dot_nt(cg, bg)
        hg = h_ref[0, g * gw:(g + 1) * gw, :].astype(BF16)
        yoff = _dot_nt(cg, hg)
        st = _dot_tn(xd[:, g * gw:(g + 1) * gw], bg)
        for pr in range(hpg // 2):
            col = g * gw + pr * LANES
            ms = []
            for hh in (2 * pr, 2 * pr + 1):
                h = g * hpg + hh
                seg = acum[:lv, h:h + 1] - acum_t[h:h + 1, :]
                dec = jnp.exp(jnp.where(causal, seg, -jnp.inf))
                ms.append((cbm * dec * dt_t[h:h + 1, :]).astype(BF16))
            xpair = xs[:, col:col + LANES]
            rhs = jnp.concatenate([jnp.where(lo, xpair, 0.0), jnp.where(lo, 0.0, xpair)], axis=0).astype(BF16)
            yd = _dot(jnp.concatenate(ms, axis=1), rhs)
            yacc_scr[:, col:col + LANES] = (yd + yoff[:, pr * LANES:(pr + 1) * LANES] * ea_exp[:, col:col + LANES]
                                            + dexp_ref[:, col:col + LANES] * xs[:lv, col:col + LANES])
        for hh in range(hpg):
            h = g * hpg + hh
            r0 = h * SSD_P
            h_ref[0, r0:r0 + SSD_P, :] = (h_ref[0, r0:r0 + SSD_P, :] * cdm[h:h + 1, :]
                                          + st[hh * SSD_P:(hh + 1) * SSD_P, :])

    y = yacc_scr[...] * _silu(z_ref[...])
    ng = di // SSD_G
    for g in range(SSD_G):
        yg = y[:, g * ng:(g + 1) * ng]
        ms_ = jnp.mean(yg * yg, -1, keepdims=True)
        yn = yg * lax.rsqrt(ms_ + LN_EPS) * nw_ref[:, g * ng:(g + 1) * ng]
        y_ref[:, g * ng:(g + 1) * ng] = yn.astype(y_ref.dtype)


def _ssd_scan(xbc, z, dt, conv0, h0, cw, cb, dtb, alog, dexp, nw, e, ltri, nb, lv):
    m, cdim = xbc.shape
    di = z.shape[1]
    n_heads = di // SSD_P
    nc = m // (nb * lv)
    y_dtype = BF16 if lv % (2 * SUBLANES) == 0 else F32
    return pl.pallas_call(
        functools.partial(_ssd_kernel, lv=lv, n_heads=n_heads),
        grid=(nb, nc),
        in_specs=[pl.BlockSpec((lv, cdim), lambda b, c: (b * nc + c, 0)),
                  pl.BlockSpec((lv, di), lambda b, c: (b * nc + c, 0)),
                  pl.BlockSpec((lv, LANES), lambda b, c: (b * nc + c, 0)),
                  pl.BlockSpec((1, 8, cdim), lambda b, c: (b, 0, 0)),
                  pl.BlockSpec((1, di, SSD_N), lambda b, c: (b, 0, 0)),
                  pl.BlockSpec((SSD_CONV, cdim), lambda b, c: (0, 0)),
                  pl.BlockSpec((1, cdim), lambda b, c: (0, 0)),
                  pl.BlockSpec((1, LANES), lambda b, c: (0, 0)),
                  pl.BlockSpec((1, LANES), lambda b, c: (0, 0)),
                  pl.BlockSpec((1, di), lambda b, c: (0, 0)),
                  pl.BlockSpec((1, di), lambda b, c: (0, 0)),
                  pl.BlockSpec((LANES, di), lambda b, c: (0, 0)),
                  pl.BlockSpec((SSD_CHUNK, SSD_CHUNK), lambda b, c: (0, 0))],
        out_specs=[pl.BlockSpec((lv, di), lambda b, c: (b * nc + c, 0)),
                   pl.BlockSpec((1, di, SSD_N), lambda b, c: (b, 0, 0))],
        out_shape=[jax.ShapeDtypeStruct((m, di), y_dtype),
                   jax.ShapeDtypeStruct((nb, di, SSD_N), F32)],
        scratch_shapes=[pltpu.VMEM((SSD_CHUNK + 8, cdim), F32),
                        pltpu.VMEM((SSD_CHUNK, LANES), F32),
                        pltpu.VMEM((lv, di), F32)],
        compiler_params=_cparams(("parallel", "arbitrary"), 56),
        name="ssd_scan",
    )(xbc, z, dt, conv0, h0, cw, cb, dtb, alog, dexp, nw, e, ltri)


def _ssd_mixer(x, nb, lv, conv0, h0, prm):
    w_in, conv_w, conv_b, dt_bias, a_log, d_skip, norm_w = prm
    di = norm_w.shape[0]
    cdim = conv_w.shape[1]
    n_heads = di // SSD_P
    seq = x.shape[0] // nb
    w_z = w_in[:, :di].astype(BF16)
    w_x = w_in[:, di:di + cdim].astype(BF16)
    w_dt = jnp.pad(w_in[:, di + cdim:], ((0, 0), (0, LANES - n_heads))).astype(BF16)
    z = _matmul(x, w_z, tn=1024)
    xbc = _matmul(x, w_x, tn=1024)
    dt = _matmul(x, w_dt)
    pad1 = lambda a: jnp.pad(a.astype(F32), (0, LANES - n_heads))[None, :]
    e = (jnp.arange(di)[None, :] // SSD_P == jnp.arange(LANES)[:, None]).astype(BF16)
    ltri = (jnp.arange(SSD_CHUNK)[:, None] >= jnp.arange(SSD_CHUNK)[None, :]).astype(BF16)
    y, h_new = _ssd_scan(xbc, z, dt, conv0, h0.reshape(nb, di, SSD_N), conv_w, conv_b[None, :],
                         pad1(dt_bias), pad1(a_log), jnp.repeat(d_skip, SSD_P)[None, :], norm_w[None, :],
                         e, ltri, nb, lv)
    conv_new = xbc.reshape(nb, seq, cdim)[:, seq - (SSD_CONV - 1):]
    return y, conv_new, h_new.reshape(nb, n_heads, SSD_P, SSD_N)


def _gm_in_kernel(x_ref, w_ref, b_ref, g_ref, bb_ref, u_ref, v_ref):
    h = _gelu_tanh(_dot(x_ref[...].astype(BF16), w_ref[...]) + b_ref[...])
    d = u_ref.shape[1]
    u_ref[...] = h[:, :d]
    v_ref[...] = _ln(h[:, d:], g_ref[...], bb_ref[...])


def _gm_in(x, w, b, g, bb):
    m, k = x.shape
    d = w.shape[1] // 2
    tm = _row_tile(m, (512, 256))
    return pl.pallas_call(
        _gm_in_kernel,
        grid=(m // tm,),
        in_specs=[pl.BlockSpec((tm, k), lambda i: (i, 0)),
                  pl.BlockSpec((k, 2 * d), lambda i: (0, 0)),
                  pl.BlockSpec((1, 2 * d), lambda i: (0, 0)),
                  pl.BlockSpec((1, d), lambda i: (0, 0)),
                  pl.BlockSpec((1, d), lambda i: (0, 0))],
        out_specs=[pl.BlockSpec((tm, d), lambda i: (i, 0)), pl.BlockSpec((tm, d), lambda i: (i, 0))],
        out_shape=[jax.ShapeDtypeStruct((m, d), F32), jax.ShapeDtypeStruct((m, d), F32)],
        compiler_params=_cparams(("parallel",), 48),
        name="gm_in",
    )(x, w, b, g, bb)


def _gm_out_kernel(u_ref, v_ref, x_ref, s_ref, sb_ref, w_ref, g_ref, b_ref, o_ref, gated_scr, *, alpha, r):
    tb = u_ref.shape[0]
    gd = u_ref.shape[1] // GM_G
    for s in range(tb // r):
        rows = slice(s * r, (s + 1) * r)
        for g in range(GM_G):
            cols = slice(g * gd, (g + 1) * gd)
            mixed = _dot(s_ref[g], v_ref[rows, cols].astype(BF16)) + sb_ref[rows, cols]
            gated_scr[rows, cols] = (u_ref[rows, cols] * mixed).astype(BF16)
    f = _dot(gated_scr[...], w_ref[...])
    o_ref[...] = _ln(alpha * x_ref[...] + f, g_ref[...], b_ref[...])


def _gm_out(u, v, x, smat, sbias, w, g, b, alpha, tb):
    m, d = u.shape
    r = smat.shape[1]
    return pl.pallas_call(
        functools.partial(_gm_out_kernel, alpha=alpha, r=r),
        grid=(m // tb,),
        in_specs=[pl.BlockSpec((tb, d), lambda i: (i, 0)),
                  pl.BlockSpec((tb, d), lambda i: (i, 0)),
                  pl.BlockSpec((tb, d), lambda i: (i, 0)),
                  pl.BlockSpec((GM_G, r, r), lambda i: (0, 0, 0)),
                  pl.BlockSpec((tb, d), lambda i: (0, 0)),
                  pl.BlockSpec((d, d), lambda i: (0, 0)),
                  pl.BlockSpec((1, d), lambda i: (0, 0)),
                  pl.BlockSpec((1, d), lambda i: (0, 0))],
        out_specs=pl.BlockSpec((tb, d), lambda i: (i, 0)),
        out_shape=jax.ShapeDtypeStruct((m, d), F32),
        scratch_shapes=[pltpu.VMEM((tb, d), BF16)],
        compiler_params=_cparams(("parallel",), 48),
        name="gm_out",
    )(u, v, x, smat, sbias, w, g, b)


def _gmlp_mixer(x, nb, seq, prm, ln_g, ln_b, alpha):
    w_in, b_in, g_in, bb_in, w_s, b_s, w_out = prm
    d = w_out.shape[0]
    gd = d // GM_G
    u, v = _gm_in(x, w_in.astype(BF16), b_in[None, :], g_in[None, :], bb_in[None, :])
    ws = jnp.tril(w_s)
    if seq % GM_CHUNK == 0:
        tb = 512
        smat = ws.astype(BF16)
        bias_rows = jnp.repeat(b_s.T, gd, axis=1)
        sbias = jnp.tile(bias_rows, (tb // GM_CHUNK, 1))
    else:
        tb = nb * seq
        smat = jnp.einsum("ab,gts->gatbs", jnp.eye(nb, dtype=F32), ws[:, :seq, :seq]).reshape(GM_G, tb, tb).astype(BF16)
        sbias = jnp.tile(jnp.repeat(b_s.T[:seq], gd, axis=1), (nb, 1))
    out = _gm_out(u, v, x, smat, sbias, w_out.astype(BF16), ln_g, ln_b, alpha, tb)
    return out, v


def _router_kernel(x_ref, wh_ref, wl_ref, b_ref, eid_ref, gate_ref):
    x = x_ref[...]
    xh, xl = _split(x, 2)
    logits = _dot(xh, wh_ref[...]) + _dot(xl, wh_ref[...]) + _dot(xh, wl_ref[...]) + b_ref[...]
    lane = lax.broadcasted_iota(I32, logits.shape, 1)
    m1 = jnp.max(logits, -1, keepdims=True)
    i1 = jnp.min(jnp.where(logits == m1, lane, LANES), -1, keepdims=True)
    rest = jnp.where(lane == i1, NEG * 2, logits)
    m2 = jnp.max(rest, -1, keepdims=True)
    i2 = jnp.min(jnp.where(rest == m2, lane, LANES), -1, keepdims=True)
    e = jnp.exp(m2 - m1)
    g1 = 1.0 / (1.0 + e)
    eid_ref[...] = jnp.where(lane == 0, i1, jnp.where(lane == 1, i2, 0))
    gate_ref[...] = jnp.where(lane == 0, g1, jnp.where(lane == 1, e * g1, 0.0))


def _router(x, w_router, b_router):
    m, d = x.shape
    ne = w_router.shape[1]
    w = jnp.pad(w_router, ((0, 0), (0, LANES - ne)))
    wh = w.astype(BF16)
    wl = (w - wh.astype(F32)).astype(BF16)
    b = jnp.pad(b_router.astype(F32), (0, LANES - ne), constant_values=NEG)[None, :]
    tm = _row_tile(m, (512, 256))
    return pl.pallas_call(
        _router_kernel,
        grid=(m // tm,),
        in_specs=[pl.BlockSpec((tm, d), lambda i: (i, 0)),
                  pl.BlockSpec((d, LANES), lambda i: (0, 0)),
                  pl.BlockSpec((d, LANES), lambda i: (0, 0)),
                  pl.BlockSpec((1, LANES), lambda i: (0, 0))],
        out_specs=[pl.BlockSpec((tm, LANES), lambda i: (i, 0)), pl.BlockSpec((tm, LANES), lambda i: (i, 0))],
        out_shape=[jax.ShapeDtypeStruct((m, LANES), I32), jax.ShapeDtypeStruct((m, LANES), F32)],
        compiler_params=_cparams(("parallel",), 32),
        name="router",
    )(x, wh, wl, b)


def _row_copy(src_hbm, dst, s_row, d_row, sem):
    return pltpu.make_async_copy(src_hbm.at[pl.ds(s_row, 1), :], dst.at[pl.ds(d_row, 1), :], sem)


def _row_gather(src_hbm, dst, idx_ref, n, per_row, sem, start):
    if not start:
        for k in range(per_row):
            pltpu.make_async_copy(src_hbm.at[pl.ds(0, n), :], dst.at[k], sem).wait()
        return

    def body(j, c):
        for k in range(per_row):
            _row_copy(src_hbm, dst.at[k], idx_ref[0, 0, per_row * j + k], j, sem).start()
        return c

    lax.fori_loop(0, n, body, 0, unroll=8)


def _moe_ffn_kernel(be_ref, nu_ref, idx_ref, idxn_ref, x_hbm, wg_ref, wu_ref, wd_ref, o_ref,
                    acc_ref, xg_ref, xb_ref, sem, *, tb):
    i = pl.program_id(0)
    f = pl.program_id(1)
    last = pl.num_programs(1) - 1
    n_used = nu_ref[0]
    used = i < n_used
    slot = i % 2

    @pl.when((f == 0) & (i == 0))
    def _():
        _row_gather(x_hbm, xg_ref.at[0], idx_ref, tb, 1, sem.at[0], True)

    @pl.when((f == 0) & (i + 1 < n_used))
    def _():
        _row_gather(x_hbm, xg_ref.at[1 - slot], idxn_ref, tb, 1, sem.at[1 - slot], True)

    @pl.when(used & (f == 0))
    def _():
        _row_gather(x_hbm, xg_ref.at[slot], idx_ref, tb, 1, sem.at[slot], False)
        acc_ref[...] = jnp.zeros_like(acc_ref)
        xb_ref[...] = xg_ref[slot, 0].astype(BF16)

    @pl.when(used)
    def _():
        xb = xb_ref[...]
        h = _silu(_dot(xb, wg_ref[0])) * _dot(xb, wu_ref[0])
        acc_ref[...] += _dot(h.astype(BF16), wd_ref[0])

    @pl.when(used & (f == last))
    def _():
        o_ref[...] = acc_ref[...]

    @pl.when(jnp.logical_not(used) & (f == last))
    def _():
        o_ref[...] = jnp.zeros_like(o_ref)


def _moe_ffn(x, buf_tok, blk_e, n_used, wg, wu, wd, tb):
    d = x.shape[1]
    nblk = buf_tok.shape[0] // tb
    dff = wg.shape[2]
    tf = _ff_tile(dff)
    nf = dff // tf

    def fe(i, f, be, nu):
        return jnp.where(i < nu[0], f, nf - 1)

    grid_spec = pltpu.PrefetchScalarGridSpec(
        num_scalar_prefetch=2,
        grid=(nblk, nf),
        in_specs=[pl.BlockSpec((1, 1, tb), lambda i, f, be, nu: (i, 0, 0), memory_space=pltpu.SMEM),
                  pl.BlockSpec((1, 1, tb), lambda i, f, be, nu: (jnp.minimum(i + 1, nblk - 1), 0, 0),
                               memory_space=pltpu.SMEM),
                  pl.BlockSpec(memory_space=pl.ANY),
                  pl.BlockSpec((1, d, tf), lambda i, f, be, nu: (be[i], 0, fe(i, f, be, nu))),
                  pl.BlockSpec((1, d, tf), lambda i, f, be, nu: (be[i], 0, fe(i, f, be, nu))),
                  pl.BlockSpec((1, tf, d), lambda i, f, be, nu: (be[i], fe(i, f, be, nu), 0))],
        out_specs=pl.BlockSpec((tb, d), lambda i, f, be, nu: (i, 0)),
        scratch_shapes=[pltpu.VMEM((tb, d), F32), pltpu.VMEM((2, 1, tb, d), F32), pltpu.VMEM((tb, d), BF16),
                        pltpu.SemaphoreType.DMA((2,))])
    idx = buf_tok.reshape(nblk, 1, tb)
    return pl.pallas_call(
        functools.partial(_moe_ffn_kernel, tb=tb),
        grid_spec=grid_spec,
        out_shape=jax.ShapeDtypeStruct((nblk * tb, d), F32),
        compiler_params=_cparams(("arbitrary", "arbitrary"), 48),
        name="moe_ffn",
    )(blk_e, n_used, idx, idx, x, wg, wu, wd)


def _combine_kernel(idx_ref, idxn_ref, yb_hbm, x_ref, gate_ref, g_ref, b_ref, o_ref, ybuf, sem, *, alpha, tm):
    i = pl.program_id(0)
    slot = i % 2

    @pl.when(i == 0)
    def _():
        _row_gather(yb_hbm, ybuf.at[0], idx_ref, tm, TOP_K, sem.at[0], True)

    @pl.when(i + 1 < pl.num_programs(0))
    def _():
        _row_gather(yb_hbm, ybuf.at[1 - slot], idxn_ref, tm, TOP_K, sem.at[1 - slot], True)

    _row_gather(yb_hbm, ybuf.at[slot], idx_ref, tm, TOP_K, sem.at[slot], False)
    gate = gate_ref[...]
    y = gate[:, 0:1] * ybuf[slot, 0] + gate[:, 1:2] * ybuf[slot, 1]
    o_ref[...] = _ln(alpha * x_ref[...] + y, g_ref[...], b_ref[...])


def _moe_combine(yb, dest, x, gate, g, b, alpha):
    m, d = x.shape
    tm = _row_tile(m, (256, 128))
    nblk = m // tm
    idx = dest.reshape(nblk, 1, TOP_K * tm)
    return pl.pallas_call(
        functools.partial(_combine_kernel, alpha=alpha, tm=tm),
        grid=(nblk,),
        in_specs=[pl.BlockSpec((1, 1, TOP_K * tm), lambda i: (i, 0, 0), memory_space=pltpu.SMEM),
                  pl.BlockSpec((1, 1, TOP_K * tm), lambda i: (jnp.minimum(i + 1, nblk - 1), 0, 0),
                               memory_space=pltpu.SMEM),
                  pl.BlockSpec(memory_space=pl.ANY),
                  pl.BlockSpec((tm, d), lambda i: (i, 0)),
                  pl.BlockSpec((tm, LANES), lambda i: (i, 0)),
                  pl.BlockSpec((1, d), lambda i: (0, 0)),
                  pl.BlockSpec((1, d), lambda i: (0, 0))],
        out_specs=pl.BlockSpec((tm, d), lambda i: (i, 0)),
        out_shape=jax.ShapeDtypeStruct((m, d), F32),
        scratch_shapes=[pltpu.VMEM((2, TOP_K, tm, d), F32), pltpu.SemaphoreType.DMA((2,))],
        compiler_params=_cparams(("arbitrary",), 32),
        name="moe_combine",
    )(idx, idx, yb, x, gate, g, b)


def _moe_res_ln(x, w_router, b_router, wg, wu, wd, g, b, alpha):
    m, d = x.shape
    ne = w_router.shape[1]
    eid, gate = _router(x, w_router, b_router)
    tk = m * TOP_K
    tb = 1024 if tk >= 16384 else 256
    ef = eid[:, :TOP_K].reshape(-1)
    onehot = (ef[:, None] == jnp.arange(ne, dtype=I32)[None, :]).astype(I32)
    csum = jnp.cumsum(onehot, axis=0)
    rank = jnp.sum(csum * onehot, axis=1) - 1
    counts = csum[-1]
    padded = (counts + tb - 1) // tb * tb
    pend = jnp.cumsum(padded)
    dest = (pend - padded)[ef] + rank
    nblk = -(-tk // tb) + ne
    buf_tok = jnp.zeros((nblk * tb,), I32).at[dest].set(jnp.arange(tk, dtype=I32) // TOP_K)
    blk_start = jnp.arange(nblk, dtype=I32) * tb
    blk_e = jnp.minimum(jnp.sum((pend[None, :] <= blk_start[:, None]).astype(I32), axis=1), ne - 1).astype(I32)
    n_used = (pend[-1:] // tb).astype(I32)
    yb = _moe_ffn(x, buf_tok, blk_e, n_used, wg, wu, wd, tb)
    return _moe_combine(yb, dest.astype(I32), x, gate, g, b, alpha)


PAGE = 128


def _compress_kernel(pt_ref, pages_hbm, w1_ref, pos_ref, b1_ref, w2_ref, b2_ref, o_ref, scr, acc_scr, cv_scr, sem,
                     *stage, npg, col0):
    b = pl.program_id(0)
    pw = NSA_KV * NSA_DH
    nsub = npg * PAGE // CMP_STRIDE
    hid = b1_ref.shape[2]

    nslab = 2 * pw // LANES

    def position_term():
        for c in range(2):
            cv = jnp.zeros((SUBLANES, hid), F32) + b1_ref[c]
            for o in range(CMP_STRIDE):
                cv = cv + _dot(pos_ref[c, o].astype(BF16), w1_ref[c, o])[:, :hid]
                cv = cv + _dot(pos_ref[c, CMP_STRIDE + o].astype(BF16), w1_ref[c, o])[:, hid:]
            cv_scr[c] = cv

    if not stage:
        def start(p, c_):
            row = pl.multiple_of(p * PAGE, PAGE)
            for sl in range(nslab):
                pltpu.make_async_copy(pages_hbm.at[pt_ref[b, p], :, pl.ds(col0 + sl * LANES, LANES)],
                                      scr.at[sl, pl.ds(row, PAGE), :], sem.at[0]).start()
            return c_

        lax.fori_loop(0, npg, start, 0)
        pl.when(b == 0)(position_term)
        for sl in range(nslab):
            pltpu.make_async_copy(scr.at[sl], scr.at[sl], sem.at[0]).wait()
    else:
        stg = stage[0]
        cpg = stg.shape[2]
        nch = npg // cpg

        def start_chunk(bb, k, slot):
            def body(p, c_):
                for c in range(2):
                    pltpu.make_async_copy(pages_hbm.at[pt_ref[bb, k * cpg + p], c], stg.at[slot, c, p],
                                          sem.at[slot]).start()
                return c_

            lax.fori_loop(0, cpg, body, 0)

        @pl.when(b == 0)
        def _():
            start_chunk(b, 0, 0)
            position_term()

        for k in range(nch):
            slot = k % 2
            if k + 1 < nch:
                start_chunk(b, k + 1, 1 - slot)
            else:
                pl.when(b + 1 < pl.num_programs(0))(functools.partial(start_chunk, b + 1, 0, (k + 1) % 2))
            for c in range(2):
                pltpu.make_async_copy(stg.at[slot, c], stg.at[slot, c], sem.at[slot]).wait()

            def to_token_major(p, c_):
                row = pl.multiple_of((k * cpg + p) * PAGE, PAGE)
                for sl in range(nslab):
                    c, half = divmod(sl, nslab // 2)
                    scr[sl, pl.ds(row, PAGE), :] = stg[slot, c, p, half * LANES:(half + 1) * LANES, :].T
                return c_

            lax.fori_loop(0, cpg, to_token_major, 0)

    gps = LANES // NSA_DH
    for c in range(2):
        acc_scr[...] = jnp.zeros_like(acc_scr)
        for o in range(CMP_STRIDE):
            for sl in range(nslab // 2):
                xo = scr[c * (nslab // 2) + sl, pl.ds(o, nsub, stride=CMP_STRIDE), :].astype(BF16)
                for gg in range(gps):
                    acc_scr[sl * gps + gg] += _dot(xo[:, gg * NSA_DH:(gg + 1) * NSA_DH], w1_ref[c, o])
        for g in range(NSA_KV):
            acc = acc_scr[g]
            pre = acc[:, :hid] + pltpu.roll(acc[:, hid:], nsub - 1, 0) + cv_scr[c, 0:1, :]
            o_ref[0, c, g] = _dot(_gelu_tanh(pre).astype(BF16), w2_ref[c]) + b2_ref[c]


def _compress(pages, page_table, pos, w1, b1, w2, b2, col0=0, token_minor=False):
    nb, npg = page_table.shape
    nsub = npg * PAGE // CMP_STRIDE
    hid = w1.shape[-1]
    pw = NSA_KV * NSA_DH
    w1cat = jnp.concatenate([w1[:, :CMP_STRIDE], w1[:, CMP_STRIDE:]], axis=-1).astype(BF16)
    posb = jnp.broadcast_to(pos[:, :, None, :], (2, CMP_BLK, SUBLANES, NSA_DH))
    chunk = 16
    assert not token_minor or (npg % chunk == 0 and (npg // chunk) % 2 == 0)
    grid_spec = pltpu.PrefetchScalarGridSpec(
        num_scalar_prefetch=1,
        grid=(nb,),
        in_specs=[pl.BlockSpec(memory_space=pl.ANY),
                  pl.BlockSpec((2, CMP_STRIDE, NSA_DH, 2 * hid), lambda b, pt: (0, 0, 0, 0)),
                  pl.BlockSpec((2, CMP_BLK, SUBLANES, NSA_DH), lambda b, pt: (0, 0, 0, 0)),
                  pl.BlockSpec((2, 1, hid), lambda b, pt: (0, 0, 0)),
                  pl.BlockSpec((2, hid, NSA_DH), lambda b, pt: (0, 0, 0)),
                  pl.BlockSpec((2, 1, NSA_DH), lambda b, pt: (0, 0, 0))],
        out_specs=pl.BlockSpec((1, 2, NSA_KV, nsub, NSA_DH), lambda b, pt: (b, 0, 0, 0, 0)),
        scratch_shapes=[pltpu.VMEM((2 * pw // LANES, npg * PAGE, LANES), F32),
                        pltpu.VMEM((NSA_KV, nsub, 2 * hid), F32),
                        pltpu.VMEM((2, SUBLANES, hid), F32),
                        pltpu.SemaphoreType.DMA((2,))]
                       + ([pltpu.VMEM((2, 2, chunk, pw, PAGE), F32)] if token_minor else []))
    return pl.pallas_call(
        functools.partial(_compress_kernel, npg=npg, col0=col0),
        grid_spec=grid_spec,
        out_shape=jax.ShapeDtypeStruct((nb, 2, NSA_KV, nsub, NSA_DH), F32),
        compiler_params=_cparams(("arbitrary",), 56),
        name="nsa_compress",
    )(page_table, pages, w1cat, posb, b1[:, None, :], w2.astype(BF16), b2[:, None, :])


def _masked_softmax(s, valid):
    s = jnp.where(valid, s, NEG)
    p = jnp.exp(s - jnp.max(s, -1, keepdims=True))
    return p / jnp.sum(p, -1, keepdims=True)


def _select_blocks(imp, ovl, tq, nbs):
    score = sum(_dot(t, ovl) for t in _split(imp, 3))
    jb = lax.broadcasted_iota(I32, score.shape, 1)
    ok = jb * SEL_BLK <= tq
    cur = tq // SEL_BLK
    forced = (jb == 0) | (jb == cur) | (jb == cur - 1)
    score = jnp.where(ok, score + jnp.where(forced, FORCE, 0.0), NEG)
    rank = jnp.zeros(score.shape, F32)
    for j2 in range(nbs):
        cj = score[:, j2:j2 + 1]
        beats = (cj > score) | ((cj == score) & (j2 < jb))
        rank = rank + jnp.where(beats, 1.0, 0.0)
    return jnp.where(ok & (rank < N_SEL), 1.0, 0.0)


def _expand_sel(sel, first_blk, nkeys):
    jj = lax.broadcasted_iota(I32, (sel.shape[1], nkeys), 0)
    kk = lax.broadcasted_iota(I32, (sel.shape[1], nkeys), 1)
    expand = jnp.where(kk // SEL_BLK + first_blk == jj, 1.0, 0.0).astype(BF16)
    return _dot(sel.astype(BF16), expand)


POS_SPLIT = 64
Q_EXTRA = 16


PROMPT_KT = 512


def _nsa_prompt_kernel(q_ref, gate_ref, slope_ref, kc_ref, vc_ref, ks_ref, vs_ref, kw_ref, vw_ref, ovl_ref, exp_ref,
                       o_ref, *, nbs, qblk, nrep):
    qb = pl.program_id(2)
    cols = qblk * nrep
    q = q_ref[0, 0, 0]
    q16 = q.astype(BF16)
    slope = slope_ref[0][0:1, :]
    tq = qb * qblk + lax.broadcasted_iota(I32, (1, cols), 1) % qblk
    tq_row = tq[:, 0:qblk]

    kc = kc_ref[0, 0]
    qh, ql = _split(q[:NSA_DH, :], 2)
    kh, kl = _split(kc, 2)
    s = _dot(kh, qh) + _dot(kh, ql) + _dot(kl, qh)
    e = lax.broadcasted_iota(I32, (kc.shape[0], 1), 0) * CMP_STRIDE + (CMP_BLK - 1)
    valid = e <= tq
    s = jnp.where(valid, s + slope * e.astype(F32), NEG)
    p = jnp.exp(s - jnp.max(s, 0, keepdims=True))
    p = jnp.where(valid, p / jnp.sum(p, 0, keepdims=True), 0.0)
    o_cmp = _dot(vc_ref[0, 0].astype(BF16), p.astype(BF16))
    imp = p[:, 0:qblk]
    for r in range(1, nrep):
        imp = imp + p[:, r * qblk:(r + 1) * qblk]

    score = sum(_dot(ovl_ref[...], t) for t in _split(imp, 3))
    jb = lax.broadcasted_iota(I32, score.shape, 0)
    ok = jb * SEL_BLK <= tq_row
    cur = tq_row // SEL_BLK
    forced = (jb == 0) | (jb == cur) | (jb == cur - 1)
    score = jnp.where(ok, score + jnp.where(forced, FORCE, 0.0), NEG)
    rank = jnp.zeros(score.shape, F32)
    for j2 in range(nbs):
        cj = score[j2:j2 + 1, :]
        beats = (cj > score) | ((cj == score) & (j2 < jb))
        rank = rank + jnp.where(beats, 1.0, 0.0)
    selneg = jnp.where(ok & (rank < N_SEL), 0.0, NEG).astype(BF16)

    def tile(k_ref, v_ref, bias_fn, kt, carry):
        m, l, acc = carry
        off = pl.multiple_of(kt * PROMPT_KT, PROMPT_KT)
        kpos = off + lax.broadcasted_iota(I32, (PROMPT_KT, 1), 0)
        s_ = _dot(k_ref[0, 0, pl.ds(off, PROMPT_KT), :], q16)
        s_ = s_ + jnp.concatenate([bias_fn(off, kpos)] * nrep, axis=1)
        m_new = jnp.maximum(m, jnp.max(s_, 0, keepdims=True))
        a = jnp.exp(m - m_new)
        p_ = jnp.exp(s_ - m_new)
        v = v_ref[0, 0, :, pl.ds(off, PROMPT_KT)].astype(BF16)
        return m_new, l * a + jnp.sum(p_, 0, keepdims=True), acc * a + _dot(v, p_.astype(BF16))

    def sel_bias(off, kpos):
        return jnp.where(kpos <= tq_row, _dot(exp_ref[pl.ds(off, PROMPT_KT), :], selneg), NEG)

    def win_bias(off, kpos):
        return jnp.where((kpos <= tq_row) & (kpos > tq_row - WINDOW), 0.0, NEG)

    kt_hi = (qb * qblk + qblk - 1) // PROMPT_KT + 1
    kt_win = jnp.maximum(qb * qblk - (WINDOW - 1), 0) // PROMPT_KT
    init = (jnp.full((1, cols), -jnp.inf, F32), jnp.zeros((1, cols), F32), jnp.zeros((NSA_DH, cols), F32))
    c_sel = lax.fori_loop(0, kt_win, lambda kt, c: tile(ks_ref, vs_ref, sel_bias, kt, c), init)
    c_sel, c_win = lax.fori_loop(
        kt_win, kt_hi,
        lambda kt, c: (tile(ks_ref, vs_ref, sel_bias, kt, c[0]), tile(kw_ref, vw_ref, win_bias, kt, c[1])),
        (c_sel, init))
    gate = gate_ref[0, 0, 0]
    o_ref[0, 0, 0] = (gate[0:1, :] * o_cmp + gate[1:2, :] * (c_sel[2] / c_sel[1])
                      + gate[2:3, :] * (c_win[2] / c_win[1]))


def _nsa_prompt(qt, gates, slopes, kc, vct, ksa, kwa, kvt, ovl_t, nbs):
    nb, ng, nqb, qw, cols = qt.shape
    dh = qw - Q_EXTRA
    seq = kvt.shape[3]
    nbc = kc.shape[2]
    qblk = seq // nqb
    nrep = cols // qblk
    qspec = lambda r: pl.BlockSpec((1, 1, 1, r, cols), lambda b, g, i: (b, g, i, 0, 0))
    per_g = lambda r, c: pl.BlockSpec((1, 1, r, c), lambda b, g, i: (b, g, 0, 0))
    tspec = lambda c: pl.BlockSpec((1, 1, dh, seq), lambda b, g, i: (b, c * ng + g, 0, 0))
    expand = (jnp.arange(seq)[:, None] // SEL_BLK == jnp.arange(ovl_t.shape[0])[None, :]).astype(BF16)
    return pl.pallas_call(
        functools.partial(_nsa_prompt_kernel, nbs=nbs, qblk=qblk, nrep=nrep),
        grid=(nb, ng, nqb),
        in_specs=[qspec(qw), qspec(SUBLANES),
                  pl.BlockSpec((1, SUBLANES, cols), lambda b, g, i: (g, 0, 0)),
                  per_g(nbc, dh), per_g(dh, nbc), per_g(seq, qw), tspec(3), per_g(seq, qw), tspec(5),
                  pl.BlockSpec(ovl_t.shape, lambda b, g, i: (0, 0)),
                  pl.BlockSpec(expand.shape, lambda b, g, i: (0, 0))],
        out_specs=qspec(dh),
        out_shape=jax.ShapeDtypeStruct((nb, ng, nqb, dh, cols), F32),
        compiler_params=_cparams(("parallel", "parallel", "arbitrary"), 48),
        name="nsa_prompt",
    )(qt, gates, slopes, kc, vct, ksa, kvt, kwa, kvt, ovl_t, expand)


def _nsa_sample_kernel(pt_ref, q_ref, gate_ref, slope_ref, kc_ref, vc_ref, *refs, nbs, past, nq, nrep, ppt):
    ks_pages, vs_pages = refs[:ppt], refs[ppt:2 * ppt]
    (ksn_ref, vsn_ref, kwp_ref, vwp_ref, kwn_ref, vwn_ref, ovl_ref, o_ref,
     sel_scr, m_scr, l_scr, acc_scr, part_scr) = refs[2 * ppt:]
    keys = ppt * PAGE
    t = pl.program_id(1)
    rows = q_ref.shape[1]
    pw = q_ref.shape[2]
    grows = rows // NSA_KV
    scale = NSA_DH ** -0.5
    q = q_ref[0]
    q16 = q.astype(BF16)
    slope = slope_ref[:, 0:1]
    tq = past + lax.broadcasted_iota(I32, (rows, 1), 0) % nq
    tqf = tq.astype(F32)
    gate = gate_ref[0]

    def logits(qk, kpos):
        return qk * scale - slope * (tqf - kpos.astype(F32))

    @pl.when(t == 0)
    def _():
        kc = kc_ref[0]
        qh, ql = _split(q, 2)
        kh, kl = _split(kc, 2)
        s = (_dot_nt(qh, kh) + _dot_nt(ql, kh) + _dot_nt(qh, kl)) * scale
        e = lax.broadcasted_iota(I32, (1, kc.shape[0]), 1) * CMP_STRIDE + (CMP_BLK - 1)
        valid = e <= tq
        p = jnp.where(valid, _masked_softmax(s - slope * (tqf - e.astype(F32)), valid), 0.0)
        o_cmp = _dot(p.astype(BF16), vc_ref[0].astype(BF16))
        imps = []
        for g in range(NSA_KV):
            a = p[g * grows:g * grows + nq]
            for r in range(1, nrep):
                a = a + p[g * grows + r * nq:g * grows + (r + 1) * nq]
            imps.append(a)
        imp = jnp.concatenate(imps, axis=0)
        tq_s = past + lax.broadcasted_iota(I32, (NSA_KV * nq, 1), 0) % nq
        sel_s = _select_blocks(imp, ovl_ref[...], tq_s, nbs)
        sel = jnp.concatenate([sel_s[g * nq:(g + 1) * nq] for g in range(NSA_KV) for _ in range(nrep)], axis=0)
        sel_scr[...] = sel

        wb = kwp_ref.shape[3]
        npad = LANES - nq
        zpad = jnp.zeros((npad, pw), F32)
        kwn = jnp.concatenate([kwn_ref[0], zpad], axis=0).astype(BF16)
        vwn = jnp.concatenate([vwn_ref[0], zpad], axis=0).astype(BF16)
        qk = jnp.concatenate([_dot(q16, kwp_ref[0, 0].astype(BF16)), _dot_nt(q16, kwn)], axis=1)
        idx = lax.broadcasted_iota(I32, (1, wb + LANES), 1)
        kpos = past - wb + idx
        valid = (idx < wb + nq) & (kpos <= tq) & (kpos > tq - WINDOW)
        pw_ = _masked_softmax(logits(qk, kpos), valid).astype(BF16)
        o_win = _dot_nt(pw_[:, :wb], vwp_ref[0, 0].astype(BF16)) + _dot(pw_[:, wb:], vwn)
        part_scr[...] = gate[:, 0:1] * o_cmp + gate[:, 2:3] * o_win

        kn = jnp.concatenate([ksn_ref[0], zpad], axis=0).astype(BF16)
        vn = jnp.concatenate([vsn_ref[0], zpad], axis=0).astype(BF16)
        idx = lax.broadcasted_iota(I32, (1, LANES), 1)
        kpos = past + idx
        blk = past // SEL_BLK
        valid = (idx < nq) & (kpos <= tq) & (sel[:, blk:blk + 1] > 0.5)
        s = jnp.where(valid, logits(_dot_nt(q16, kn), kpos), NEG)
        m = jnp.max(s, -1, keepdims=True)
        p = jnp.exp(s - m)
        m_scr[...] = m
        l_scr[...] = jnp.sum(p, -1, keepdims=True)
        acc_scr[...] = _dot(p.astype(BF16), vn)

    k = jnp.concatenate([r[0, 0] for r in ks_pages], axis=1).astype(BF16)
    v = jnp.concatenate([r[0, 0] for r in vs_pages], axis=1).astype(BF16)
    kpos = t * keys + lax.broadcasted_iota(I32, (1, keys), 1)
    selk = _expand_sel(sel_scr[...], t * (keys // SEL_BLK), keys)
    s = jnp.where(selk > 0.5, logits(_dot(q16, k), kpos), NEG)
    m = m_scr[...]
    m_new = jnp.maximum(m, jnp.max(s, -1, keepdims=True))
    a = jnp.exp(m - m_new)
    p = jnp.exp(s - m_new)
    m_scr[...] = m_new
    l_scr[...] = l_scr[...] * a + jnp.sum(p, -1, keepdims=True)
    acc_scr[...] = acc_scr[...] * a + _dot_nt(p.astype(BF16), v)

    @pl.when(t == pl.num_programs(1) - 1)
    def _():
        tot = part_scr[...] + gate[:, 1:2] * (acc_scr[...] / l_scr[...])
        o_ref[0] = jnp.concatenate([tot[g * grows:(g + 1) * grows, g * NSA_DH:(g + 1) * NSA_DH]
                                    for g in range(NSA_KV)], axis=0)


def _nsa_sample(qbd, gates, slopes, kc, vc, cache_t, page_table, kv_new, win_t, win_new, ovl, nbs, nq, nrep):
    nb, rows, pw = qbd.shape
    npg = page_table.shape[1]
    past = npg * PAGE
    ppt = max(p for p in (8, 4, 2, 1) if npg % p == 0)
    nt = npg // ppt
    nbc = kc.shape[1]
    wb = win_t.shape[3]
    const = lambda shape: pl.BlockSpec(shape, lambda b, t, pt: (0,) * len(shape))
    per_b = lambda n, w, col=0: pl.BlockSpec((1, n, w), lambda b, t, pt: (b, 0, col))
    page = lambda c, i: pl.BlockSpec((1, 1, pw, PAGE), lambda b, t, pt: (pt[b, ppt * t + i], c, 0, 0))
    wspec = lambda c: pl.BlockSpec((1, 1, pw, wb), lambda b, t, pt: (b, c, 0, 0))
    grid_spec = pltpu.PrefetchScalarGridSpec(
        num_scalar_prefetch=1,
        grid=(nb, nt),
        in_specs=[per_b(rows, pw), per_b(rows, SUBLANES), const((rows, SUBLANES)), per_b(nbc, pw), per_b(nbc, pw)]
                 + [page(2, i) for i in range(ppt)] + [page(3, i) for i in range(ppt)]
                 + [per_b(nq, pw, 2), per_b(nq, pw, 3), wspec(0), wspec(1),
                    per_b(nq, pw, 0), per_b(nq, pw, 1), const(ovl.shape)],
        out_specs=pl.BlockSpec((1, rows, NSA_DH), lambda b, t, pt: (b, 0, 0)),
        scratch_shapes=[pltpu.VMEM((rows, ovl.shape[1]), F32), pltpu.VMEM((rows, 1), F32), pltpu.VMEM((rows, 1), F32),
                        pltpu.VMEM((rows, pw), F32), pltpu.VMEM((rows, pw), F32)])
    return pl.pallas_call(
        functools.partial(_nsa_sample_kernel, nbs=nbs, past=past, nq=nq, nrep=nrep, ppt=ppt),
        grid_spec=grid_spec,
        out_shape=jax.ShapeDtypeStruct((nb, rows, NSA_DH), F32),
        compiler_params=_cparams(("parallel", "arbitrary"), 48),
        name="nsa_sample",
    )(page_table, qbd, gates, slopes, kc, vc, *([cache_t] * (2 * ppt)), kv_new, kv_new, win_t, win_t,
      win_new, win_new, ovl)


def _overlap(nbc, nbs_pad):
    ci = jnp.arange(nbc)[:, None] * CMP_STRIDE
    sj = jnp.arange(nbs_pad)[None, :] * SEL_BLK
    return ((ci < sj + SEL_BLK) & (ci + CMP_BLK > sj)).astype(BF16)


def _alibi_slopes(nh):
    return jnp.exp2(-8.0 * (jnp.arange(nh, dtype=F32) + 1.0) / nh)


def _nsa_in(x, w_in, nh):
    qw = nh * NSA_DH
    kvw = 6 * NSA_KV * NSA_DH
    proj = _matmul(x, w_in[:, :qw + kvw].astype(BF16), tn=512)
    ng = 3 * nh
    gates = _matmul(x, jnp.pad(w_in[:, qw + kvw:], ((0, 0), (0, LANES - ng))).astype(BF16), act="sigmoid")
    kv4w = 4 * NSA_KV * NSA_DH
    return proj[:, :qw], proj[:, qw:qw + kv4w], proj[:, qw + kv4w:], gates[:, :ng]


def _mm_t_kernel(w_ref, x_ref, o_ref):
    o_ref[0] = _dot_nt(w_ref[...], x_ref[...].astype(BF16))


def _matmul_t(x, wt, nb):
    m, k = x.shape
    n = wt.shape[0]
    seq = m // nb
    tl = _row_tile(seq, (1024, 512, 256, 128))
    tn = _row_tile(n, (512, 256, 128))
    return pl.pallas_call(
        _mm_t_kernel,
        grid=(nb, seq // tl, n // tn),
        in_specs=[pl.BlockSpec((tn, k), lambda b, l, j: (j, 0)),
                  pl.BlockSpec((tl, k), lambda b, l, j: (b * (seq // tl) + l, 0))],
        out_specs=pl.BlockSpec((1, tn, tl), lambda b, l, j: (b, j, l)),
        out_shape=jax.ShapeDtypeStruct((nb, n, seq), F32),
        compiler_params=_cparams(("parallel", "parallel", "arbitrary"), 48),
        name="matmul_t",
    )(wt, x)


def _nsa_prompt_mixer(x, nb, seq, prm):
    w_in, pos, w1, b1, w2, b2 = prm
    nh = (w_in.shape[1] - 6 * NSA_KV * NSA_DH) // (NSA_DH + 3)
    nrep = nh // NSA_KV
    pw = NSA_KV * NSA_DH
    qw = nh * NSA_DH
    qblk = 128
    nqb = seq // qblk
    rows = qblk * nrep
    w_tok = jnp.concatenate([w_in[:, :qw + 3 * pw], w_in[:, qw + 4 * pw:qw + 5 * pw]], axis=1)
    proj = _matmul(x, w_tok.astype(BF16), tn=512)
    kvt = _matmul_t(x, w_in[:, qw:qw + 6 * pw].T.astype(BF16), nb)
    ng = 3 * nh
    gates = _matmul(x, jnp.pad(w_in[:, qw + 6 * pw:], ((0, 0), (0, LANES - ng))).astype(BF16), act="sigmoid")[:, :ng]
    npg = seq // PAGE
    cmp_out = _compress(proj.reshape(nb * npg, PAGE, proj.shape[1]), jnp.arange(nb * npg, dtype=I32).reshape(nb, npg),
                        pos, w1, b1, w2, b2, col0=qw)
    q = proj[:, :qw] * (NSA_DH ** -0.5)
    qt = q.reshape(nb, nqb, qblk, NSA_KV, nrep, NSA_DH).transpose(0, 3, 1, 5, 4, 2).reshape(nb, NSA_KV, nqb, NSA_DH, rows)
    slf = jnp.repeat(_alibi_slopes(nh).reshape(NSA_KV, nrep), qblk, axis=1)
    s_hi = slf.astype(BF16).astype(F32)
    s_lo = (slf - s_hi).astype(BF16).astype(F32)
    extra = jnp.stack([POS_SPLIT * s_hi, s_hi, POS_SPLIT * s_lo, s_lo] + [jnp.zeros_like(slf)] * (Q_EXTRA - 4), axis=1)
    qt = jnp.concatenate([qt, jnp.broadcast_to(extra[None, :, None], (nb, NSA_KV, nqb, Q_EXTRA, rows))], axis=3)
    gs = gates.reshape(nb, nqb, qblk, 3, NSA_KV, nrep).transpose(0, 4, 1, 3, 5, 2).reshape(nb, NSA_KV, nqb, 3, rows)
    gs = jnp.pad(gs, ((0, 0),) * 3 + ((0, SUBLANES - 3), (0, 0)))
    sl = jnp.pad(slf[:, None, :], ((0, 0), (0, SUBLANES - 1), (0, 0)))
    t = jnp.arange(seq, dtype=I32)
    pos_cols = jnp.stack([t // POS_SPLIT, t % POS_SPLIT] * 2 + [jnp.zeros_like(t)] * (Q_EXTRA - 4), axis=1).astype(F32)
    pos_cols = jnp.broadcast_to(pos_cols[None, None], (nb, NSA_KV, seq, Q_EXTRA))
    with_pos = lambda cols: jnp.concatenate(
        [cols.reshape(nb, seq, NSA_KV, NSA_DH).transpose(0, 2, 1, 3), pos_cols], axis=-1).astype(BF16)
    ksa = with_pos(proj[:, qw + 2 * pw:qw + 3 * pw])
    kwa = with_pos(proj[:, qw + 3 * pw:qw + 4 * pw])
    nbs = -(-seq // SEL_BLK)
    nbs_r = -(-nbs // SUBLANES) * SUBLANES
    o = _nsa_prompt(qt, gs, sl, cmp_out[:, 0], cmp_out[:, 1].transpose(0, 1, 3, 2), ksa, kwa,
                    kvt.reshape(nb, 6 * NSA_KV, NSA_DH, seq), _overlap(seq // CMP_STRIDE, nbs_r).T, nbs)
    o = o.reshape(nb, NSA_KV, nqb, NSA_DH, nrep, qblk).transpose(0, 2, 5, 1, 4, 3).reshape(nb * seq, nh * NSA_DH)
    kv6 = kvt.reshape(nb, 6, NSA_KV, NSA_DH, seq).transpose(0, 4, 1, 2, 3)
    return o, kv6[:, :, :4], kv6[:, :, 4:]


def _nsa_sample_mixer(x, nb, nq, cache, page_table, win_buf, prm):
    w_in, pos, w1, b1, w2, b2 = prm
    nh = (w_in.shape[1] - 6 * NSA_KV * NSA_DH) // (NSA_DH + 3)
    nrep = nh // NSA_KV
    pw = NSA_KV * NSA_DH
    npg = page_table.shape[1]
    past = npg * PAGE
    q, kv4, win, gates = _nsa_in(x, w_in, nh)
    n_pool = cache.shape[0]
    cache_t = cache.transpose(0, 2, 3, 4, 1).reshape(n_pool, 4, pw, PAGE)
    win_t = win_buf.transpose(0, 2, 3, 4, 1).reshape(nb, 2, pw, win_buf.shape[1])
    cmp_out = _compress(cache_t, page_table, pos, w1, b1, w2, b2, token_minor=True)
    nsub = cmp_out.shape[3]
    kc = cmp_out[:, 0].transpose(0, 2, 1, 3).reshape(nb, nsub, pw)
    vc = cmp_out[:, 1].transpose(0, 2, 1, 3).reshape(nb, nsub, pw)
    rows = NSA_KV * nrep * nq
    q5 = q.reshape(nb, nq, NSA_KV, nrep, NSA_DH).transpose(0, 2, 3, 1, 4)
    qbd = jnp.einsum("bgrqd,gh->bgrqhd", q5, jnp.eye(NSA_KV, dtype=F32)).reshape(nb, rows, pw)
    gs = gates.reshape(nb, nq, 3, NSA_KV, nrep).transpose(0, 3, 4, 1, 2).reshape(nb, rows, 3)
    gs = jnp.pad(gs, ((0, 0), (0, 0), (0, SUBLANES - 3)))
    sl = jnp.repeat(_alibi_slopes(nh), nq)[:, None]
    sl = jnp.pad(sl, ((0, 0), (0, SUBLANES - 1)))
    nbs = -(-(past + nq) // SEL_BLK)
    nbs_pad = -(-nbs // LANES) * LANES
    o = _nsa_sample(qbd, gs, sl, kc, vc, cache_t, page_table, kv4.reshape(nb, nq, 4 * pw),
                    win_t, win.reshape(nb, nq, 2 * pw), _overlap(nsub, nbs_pad), nbs, nq, nrep)
    o = o.reshape(nb, NSA_KV, nrep, nq, NSA_DH).transpose(0, 3, 1, 2, 4).reshape(nb * nq, nh * NSA_DH)
    return o, kv4.reshape(nb, nq, 4, NSA_KV, NSA_DH), win.reshape(nb, nq, 2, NSA_KV, NSA_DH)


def kernel(x_prompt, x_sample, state_ssm, state_conv, cache_kv, state_win, page_table, ln_mix_g, ln_mix_b, ln_ffn_g, ln_ffn_b, ssd_w_in, ssd_conv_w, ssd_conv_b, ssd_dt_bias, ssd_a_log, ssd_d, ssd_norm_w, ssd_w_out, gm_w_in, gm_b_in, gm_ln_g, gm_ln_b, gm_w_s, gm_b_s, gm_w_out, nsa_w_in, nsa_cmp_pos, nsa_cmp_w1, nsa_cmp_b1, nsa_cmp_w2, nsa_cmp_b2, nsa_w_out, ffn_w_gate, ffn_w_up, ffn_w_down, moe_w_router, moe_b_router, moe_w_gate, moe_w_up, moe_w_down):
    depth = ln_mix_g.shape[0]
    alpha = (2 * depth) ** 0.25
    bp, seq, d = x_prompt.shape
    db, dseq, _ = x_sample.shape
    xp = x_prompt.reshape(bp * seq, d)
    xs = x_sample.reshape(db * dseq, d)
    outs = {k: [] for k in ("ssm_p", "conv_p", "ssm_s", "conv_s", "gmv_s", "kv_p", "win_p", "kv_s", "win_s")}
    row = lambda a, i: a[i][None, :]
    for l in range(depth):
        kind, j = l % N_MIXERS, l // N_MIXERS
        mg, mb = row(ln_mix_g, l), row(ln_mix_b, l)
        if kind == 0:
            prm = (ssd_w_in[j], ssd_conv_w[j], ssd_conv_b[j], ssd_dt_bias[j], ssd_a_log[j], ssd_d[j], ssd_norm_w[j])
            cdim = ssd_conv_w.shape[2]
            w_out = ssd_w_out[j].astype(BF16)
            yp, cp, hp = _ssd_mixer(xp, bp, SSD_CHUNK, jnp.zeros((bp, 8, cdim), F32),
                                    jnp.zeros((bp,) + state_ssm.shape[2:], F32), prm)
            ys, cs, hs = _ssd_mixer(xs, db, dseq, jnp.pad(state_conv[j], ((0, 0), (8 - (SSD_CONV - 1), 0), (0, 0))),
                                    state_ssm[j], prm)
            outs["ssm_p"].append(hp), outs["conv_p"].append(cp), outs["ssm_s"].append(hs), outs["conv_s"].append(cs)
            xp = _mm_res_ln(yp, xp, w_out, mg, mb, alpha)
            xs = _mm_res_ln(ys, xs, w_out, mg, mb, alpha)
        elif kind == 1:
            prm = (gm_w_in[j], gm_b_in[j], gm_ln_g[j], gm_ln_b[j], gm_w_s[j], gm_b_s[j], gm_w_out[j])
            xp, _ = _gmlp_mixer(xp, bp, seq, prm, mg, mb, alpha)
            xs, vs = _gmlp_mixer(xs, db, dseq, prm, mg, mb, alpha)
            outs["gmv_s"].append(vs.reshape(db, dseq, -1))
        else:
            prm = (nsa_w_in[j], nsa_cmp_pos[j], nsa_cmp_w1[j], nsa_cmp_b1[j], nsa_cmp_w2[j], nsa_cmp_b2[j])
            w_out = nsa_w_out[j].astype(BF16)
            op, kvp, wp = _nsa_prompt_mixer(xp, bp, seq, prm)
            os_, kvs, wsn = _nsa_sample_mixer(xs, db, dseq, cache_kv[j], page_table, state_win[j], prm)
            outs["kv_p"].append(kvp), outs["win_p"].append(wp[:, seq - min(WINDOW, seq):])
            outs["kv_s"].append(kvs), outs["win_s"].append(wsn)
            xp = _mm_res_ln(op, xp, w_out, mg, mb, alpha)
            xs = _mm_res_ln(os_, xs, w_out, mg, mb, alpha)
        f = l // 2
        fg, fb = row(ln_ffn_g, l), row(ln_ffn_b, l)
        if l % 2 == 0:
            wg, wu, wd = ffn_w_gate[f].astype(BF16), ffn_w_up[f].astype(BF16), ffn_w_down[f].astype(BF16)
            xp = _swiglu_res_ln(xp, wg, wu, wd, fg, fb, alpha)
            xs = _swiglu_res_ln(xs, wg, wu, wd, fg, fb, alpha)
        else:
            wg, wu, wd = moe_w_gate[f].astype(BF16), moe_w_up[f].astype(BF16), moe_w_down[f].astype(BF16)
            xa = _moe_res_ln(jnp.concatenate([xp, xs], axis=0), moe_w_router[f], moe_b_router[f], wg, wu, wd,
                             fg, fb, alpha)
            xp, xs = xa[:bp * seq], xa[bp * seq:]
    st = lambda k: jnp.stack(outs[k])
    return (xp.reshape(bp, seq, d), xs.reshape(db, dseq, d), st("ssm_p"), st("conv_p"), st("ssm_s"), st("conv_s"),
            st("gmv_s"), st("kv_p"), st("win_p"), st("kv_s"), st("win_s"))
```
